```python
import jax, jax.numpy as jnp
from jax import lax
import numpy as np

D_MODEL = 1024
BATCH = 32
SEQ = 256
DEPTH = 1
DEC_BATCH = 4
DEC_SEQ = 2048
PAST_LEN = 512

GRID_W = 64
H_A = 8
DK = 64
DV = 64
W_A = H_A * DV
W_B = 512
CONV_W = 3
CHUNK = 64
D_FF = ((8 * D_MODEL // 3 + 255) // 256) * 256
N_IN = 3 * H_A * DK + W_A + 2 * H_A + 2 * H_A + 3 * W_B + 2 * D_MODEL
SPLITS = tuple(np.cumsum([3 * H_A * DK, W_A, 2 * H_A, 2 * H_A, 3 * W_B]).tolist())
EPS = 1e-6

kernel_name = "hybrid_deltanet_shortconv_diffusion_step"


def rms_norm(x, g):
    xf = x.astype(jnp.float32)
    y = xf * lax.rsqrt(jnp.mean(xf * xf, axis=-1, keepdims=True) + EPS)
    return (y * g.astype(jnp.float32)).astype(x.dtype)


def l2norm(t):
    return t * lax.rsqrt(jnp.sum(t * t, axis=-1, keepdims=True) + EPS)


def conv_centred(x, w, n_rows):
    b, t, ch = x.shape
    xr = x.reshape(b * n_rows, t // n_rows, ch)
    y = lax.conv_general_dilated(xr, w[:, None, :].astype(x.dtype), window_strides=(1,),
                                 padding=((CONV_W // 2, CONV_W // 2),),
                                 dimension_numbers=('NWC', 'WIO', 'NWC'),
                                 feature_group_count=ch)
    return y.reshape(b, t, ch)


def gated_delta_chunked(q, k, v, beta, g, s0):
    b, h, t, _ = q.shape
    n = t // CHUNK
    rs = lambda a: a.reshape(b, h, n, CHUNK, *a.shape[3:])
    q, k, v, beta, g = rs(q), rs(k), rs(v), rs(beta), rs(g)
    g = jnp.cumsum(g, axis=-1)
    tri_incl = jnp.tril(jnp.ones((CHUNK, CHUNK), bool))
    tri_strict = jnp.tril(jnp.ones((CHUNK, CHUNK), bool), -1)
    diff = g[..., :, None] - g[..., None, :]
    decay = jnp.where(tri_incl, jnp.exp(jnp.where(tri_incl, diff, 0.0)), 0.0)
    kb = k * beta[..., None]
    lower = jnp.where(tri_strict, jnp.einsum('bhncd,bhned->bhnce', kb, k) * decay, 0.0)
    eye = jnp.eye(CHUNK, dtype=jnp.float32)
    tmat = lax.linalg.triangular_solve(eye + lower, jnp.broadcast_to(eye, lower.shape),
                                       left_side=True, lower=True)
    u = jnp.einsum('bhnce,bhned->bhncd', tmat, v * beta[..., None])
    w = jnp.einsum('bhnce,bhned->bhncd', tmat, kb * jnp.exp(g)[..., None])
    attn = jnp.where(tri_incl, jnp.einsum('bhncd,bhned->bhnce', q, k) * decay, 0.0)
    g_last = g[..., -1]
    k_tail = k * jnp.exp(g_last[..., None] - g)[..., None]
    q_head = q * jnp.exp(g)[..., None]

    def step(s, xs):
        q_c, w_c, u_c, a_c, k_c, gl = xs
        v_new = u_c - jnp.einsum('bhcd,bhde->bhce', w_c, s)
        o = jnp.einsum('bhcd,bhde->bhce', q_c, s) + jnp.einsum('bhce,bhed->bhcd', a_c, v_new)
        s = s * jnp.exp(gl)[..., None, None] + jnp.einsum('bhcd,bhce->bhde', k_c, v_new)
        return s, o

    mv = lambda a: jnp.moveaxis(a, 2, 0)
    s_fin, o = lax.scan(step, s0, (mv(q_head), mv(w), mv(u), mv(attn), mv(k_tail), mv(g_last)))
    o = jnp.moveaxis(o, 0, 2).reshape(b, h, t, -1)
    return o, s_fin


def mixing(h, s0, n_rows, w_in, conv_qkv, a_log, dt_bias, onorm, conv_sc, w_br_a, w_br_b, w_out):
    b, t, _ = h.shape
    p = h @ w_in
    qkv, z, beta, a, sc, gates = jnp.split(p, SPLITS, axis=-1)
    qkv = jax.nn.silu(conv_centred(qkv, conv_qkv, n_rows))
    q, k, v = jnp.split(qkv, 3, axis=-1)
    heads = lambda x, d: x.reshape(b, t, H_A, d).transpose(0, 2, 1, 3).astype(jnp.float32)
    q = l2norm(heads(q, DK)) * (DK ** -0.5)
    k = l2norm(heads(k, DK))
    v = heads(v, DV)
    dirs = lambda x: x.astype(jnp.float32).reshape(b, t, 2, H_A).transpose(2, 0, 3, 1)
    beta = jax.nn.sigmoid(dirs(beta))
    g = -jnp.exp(a_log.astype(jnp.float32))[:, None, :, None] * jax.nn.softplus(
        dirs(a) + dt_bias.astype(jnp.float32)[:, None, :, None])
    s0 = s0.astype(jnp.float32)
    o_f, s_f = gated_delta_chunked(q, k, v, beta[0], g[0], s0[:, 0])
    fl = lambda x: jnp.flip(x, axis=2)
    o_b, s_b = gated_delta_chunked(fl(q), fl(k), fl(v), fl(beta[1]), fl(g[1]), s0[:, 1])
    o = (o_f + fl(o_b)).transpose(0, 2, 1, 3)
    o = o * lax.rsqrt(jnp.mean(o * o, axis=-1, keepdims=True) + EPS) * onorm.astype(jnp.float32)
    o = (o * jax.nn.silu(z.astype(jnp.float32).reshape(b, t, H_A, DV))).reshape(b, t, W_A)
    y_a = o.astype(h.dtype) @ w_br_a
    bg, cg, xs = jnp.split(sc, 3, axis=-1)
    y_b = (bg * conv_centred(cg * xs, conv_sc, n_rows)) @ w_br_b
    ga, gb = jnp.split(gates, 2, axis=-1)
    m = jax.nn.sigmoid(ga) * y_a + jax.nn.sigmoid(gb) * y_b
    return m @ w_out, jnp.stack([s_f, s_b], axis=1)


def layer(x, mod, s0, n_rows, norm_gains, w_in, conv_qkv, a_log, dt_bias, onorm, conv_sc,
          w_br_a, w_br_b, w_out, w_ffn_in, w_ffn_out):
    sh1, sc1, gt1, sh2, sc2, gt2 = jnp.split(mod[:, None, :], 6, axis=-1)
    h = rms_norm(x, norm_gains[0]) * (1 + sc1) + sh1
    y, s_fin = mixing(h, s0, n_rows, w_in, conv_qkv, a_log, dt_bias, onorm, conv_sc,
                      w_br_a, w_br_b, w_out)
    x = x + gt1 * rms_norm(y, norm_gains[1])
    h = rms_norm(x, norm_gains[2]) * (1 + sc2) + sh2
    gate, up = jnp.split(h @ w_ffn_in, 2, axis=-1)
    y = (jax.nn.silu(gate) * up) @ w_ffn_out
    x = x + gt2 * rms_norm(y, norm_gains[3])
    return x, s_fin


def setup_inputs(seed: int = 0) -> dict:
    key = jax.random.key(seed)
    ks = jax.random.split(key, 20)
    nrm = lambda k, shape, s: jax.random.normal(k, shape, jnp.float32) * s
    dt = jnp.exp(jax.random.uniform(ks[10], (DEPTH, 2, H_A), jnp.float32,
                                    np.log(1e-3), np.log(1e-1)))
    return {
        "x_prompt": nrm(ks[0], (BATCH, SEQ, D_MODEL), 1.0),
        "x_sample": nrm(ks[1], (DEC_BATCH, DEC_SEQ, D_MODEL), 1.0),
        "state_delta": nrm(ks[2], (DEC_BATCH, DEPTH, 2, H_A, DK, DV), 0.05),
        "c": nrm(ks[3], (DEC_BATCH, D_MODEL), 1.0),
        "c_ctx": nrm(ks[4], (D_MODEL,), 1.0),
        "w_mod": nrm(ks[5], (DEPTH, D_MODEL, 6 * D_MODEL), 0.5 * D_MODEL ** -0.5),
        "b_mod": nrm(ks[6], (DEPTH, 6 * D_MODEL), 0.02),
        "norm_gains": 1.0 + nrm(ks[7], (DEPTH, 4, D_MODEL), 0.02),
        "w_in": nrm(ks[8], (DEPTH, D_MODEL, N_IN), D_MODEL ** -0.5),
        "conv_qkv": nrm(ks[9], (DEPTH, CONV_W, 3 * H_A * DK), CONV_W ** -0.5),
        "a_log": jnp.log(jax.random.uniform(ks[11], (DEPTH, 2, H_A), jnp.float32, 1.0, 16.0)),
        "dt_bias": dt + jnp.log(-jnp.expm1(-dt)),
        "onorm": 1.0 + nrm(ks[12], (DEPTH, DV), 0.02),
        "conv_sc": nrm(ks[13], (DEPTH, CONV_W, W_B), CONV_W ** -0.5),
        "w_branch_a": nrm(ks[14], (DEPTH, W_A, D_MODEL), W_A ** -0.5),
        "w_branch_b": nrm(ks[15], (DEPTH, W_B, D_MODEL), W_B ** -0.5),
        "w_out": nrm(ks[16], (DEPTH, D_MODEL, D_MODEL), D_MODEL ** -0.5),
        "w_ffn_in": nrm(ks[17], (DEPTH, D_MODEL, 2 * D_FF), D_MODEL ** -0.5),
        "w_ffn_out": nrm(ks[18], (DEPTH, D_FF, D_MODEL), D_FF ** -0.5),
    }


def reference(x_prompt, x_sample, state_delta, c, c_ctx, w_mod, b_mod, norm_gains, w_in,
              conv_qkv, a_log, dt_bias, onorm, conv_sc, w_branch_a, w_branch_b, w_out,
              w_ffn_in, w_ffn_out):
    rows = x_sample.shape[1] // GRID_W
    y_p = x_prompt
    s_zero = jnp.zeros((x_prompt.shape[0], 2, H_A, DK, DV), jnp.float32)
    new_states = []
    for l in range(DEPTH):
        mod_ctx = (jax.nn.silu(c_ctx) @ w_mod[l] + b_mod[l])[None, :]
        y_p, s_fin = layer(y_p, mod_ctx, s_zero, 1, norm_gains[l], w_in[l], conv_qkv[l],
                           a_log[l], dt_bias[l], onorm[l], conv_sc[l], w_branch_a[l],
                           w_branch_b[l], w_out[l], w_ffn_in[l], w_ffn_out[l])
        new_states.append(s_fin)
    new_state_delta = jnp.stack(new_states, axis=1).astype(x_prompt.dtype)
    y_s = x_sample
    for l in range(DEPTH):
        mod_lat = jax.nn.silu(c) @ w_mod[l] + b_mod[l]
        y_s, _ = layer(y_s, mod_lat, state_delta[:, l], rows, norm_gains[l], w_in[l],
                       conv_qkv[l], a_log[l], dt_bias[l], onorm[l], conv_sc[l],
                       w_branch_a[l], w_branch_b[l], w_out[l], w_ffn_in[l], w_ffn_out[l])
    return (y_p, y_s, new_state_delta)
```

```python
import functools

import jax
import jax.numpy as jnp
from jax import lax
from jax.experimental import pallas as pl
from jax.experimental.pallas import tpu as pltpu

D_MODEL = 1024
H_A = 8
DK = 64
DV = 64
W_A = H_A * DV
W_B = 512
CHUNK = 64
GRID_W = 64
EPS = 1e-6
N_QKV = 3 * H_A * DK
D_FF = 2816
INV_BLOCK = 16
MOD_ROWS = 8
TOKEN_TILE = 256
VMEM_LIMIT = 56 * 1024 * 1024

F32 = jnp.float32
BF16 = jnp.bfloat16
HIGHEST = lax.Precision.HIGHEST


def _dot(a, b, precision=None):
    return jnp.dot(a, b, preferred_element_type=F32, precision=precision)


def _dot_nt(a, b, precision=None):
    return lax.dot_general(a, b, (((1,), (1,)), ((), ())), preferred_element_type=F32,
                           precision=precision)


def _dot_tn(a, b, precision=None):
    return lax.dot_general(a, b, (((0,), (0,)), ((), ())), preferred_element_type=F32,
                           precision=precision)


def _silu(x):
    return x * jax.nn.sigmoid(x)


def _rms(x, gain):
    return x * lax.rsqrt(jnp.mean(x * x, axis=-1, keepdims=True) + EPS) * gain


def _conv3_rows(x, w_ref, row_len):
    n = x.shape[0]
    pos = lax.broadcasted_iota(jnp.int32, (n, 1), 0) & (row_len - 1)
    prev = jnp.where(pos == 0, 0.0, pltpu.roll(x, 1, 0))
    nxt = jnp.where(pos == row_len - 1, 0.0, pltpu.roll(x, n - 1, 0))
    return prev * w_ref[0:1, :] + x * w_ref[1:2, :] + nxt * w_ref[2:3, :]


def _mod_kernel(c_ref, w_ref, b_ref, o_ref):
    o_ref[...] = _dot(_silu(c_ref[...]), w_ref[...], HIGHEST) + b_ref[...]


def _modulation(cvec, w_mod, b_mod):
    n_out = w_mod.shape[1]
    blk = D_MODEL
    return pl.pallas_call(
        _mod_kernel,
        grid=(n_out // blk,),
        in_specs=[pl.BlockSpec((MOD_ROWS, D_MODEL), lambda j: (0, 0)),
                  pl.BlockSpec((D_MODEL, blk), lambda j: (0, j)),
                  pl.BlockSpec((1, blk), lambda j: (0, j))],
        out_specs=pl.BlockSpec((MOD_ROWS, blk), lambda j: (0, j)),
        out_shape=jax.ShapeDtypeStruct((MOD_ROWS, n_out), F32),
        name="modulation",
    )(cvec, w_mod, b_mod)


def _inproj_kernel(x_ref, mod_ref, gain_ref, wqkv_ref, wba_ref, wbat_ref, conv_ref,
                   qkv_ref, ba_ref, bat_ref, *, row_len):
    x = x_ref[...]
    h = _rms(x, gain_ref[0:1, :]) * (1.0 + mod_ref[0, 1:2, :]) + mod_ref[0, 0:1, :]
    hb = h.astype(BF16)
    qkv = _conv3_rows(_dot(hb, wqkv_ref[...]), conv_ref, row_len)
    qkv_ref[...] = _silu(qkv)
    ba_ref[...] = _dot(hb, wba_ref[...])
    for ci in range(x.shape[0] // CHUNK):
        bat_ref[ci] = _dot_nt(wbat_ref[...], hb[ci * CHUNK:(ci + 1) * CHUNK, :])


def _inproj(x, mod3, gains, w_qkv, w_ba, w_bat, conv_qkv, *, mod_row_fn, row_len):
    n_tok = x.shape[0]
    tm = TOKEN_TILE
    const = lambda i: (0, 0)
    return pl.pallas_call(
        functools.partial(_inproj_kernel, row_len=row_len),
        grid=(n_tok // tm,),
        in_specs=[pl.BlockSpec((tm, D_MODEL), lambda i: (i, 0)),
                  pl.BlockSpec((1, 6, D_MODEL), lambda i: (mod_row_fn(i), 0, 0)),
                  pl.BlockSpec((4, D_MODEL), const),
                  pl.BlockSpec((D_MODEL, N_QKV), const),
                  pl.BlockSpec((D_MODEL, 4 * H_A), const),
                  pl.BlockSpec((4 * H_A, D_MODEL), const),
                  pl.BlockSpec((3, N_QKV), const)],
        out_specs=[pl.BlockSpec((tm, N_QKV), lambda i: (i, 0)),
                   pl.BlockSpec((tm, 4 * H_A), lambda i: (i, 0)),
                   pl.BlockSpec((tm // CHUNK, 4 * H_A, CHUNK), lambda i: (i, 0, 0))],
        out_shape=[jax.ShapeDtypeStruct((n_tok, N_QKV), F32),
                   jax.ShapeDtypeStruct((n_tok, 4 * H_A), F32),
                   jax.ShapeDtypeStruct((n_tok // CHUNK, 4 * H_A, CHUNK), F32)],
        compiler_params=pltpu.CompilerParams(dimension_semantics=("arbitrary",),
                                             vmem_limit_bytes=VMEM_LIMIT),
        name="inproj",
    )(x, mod3, gains, w_qkv, w_ba, w_bat, conv_qkv)


def _unit_tri_inverse(low):
    c = low.shape[0]
    bi = lax.broadcasted_iota(jnp.int32, (c, c), 0) // INV_BLOCK
    bj = lax.broadcasted_iota(jnp.int32, (c, c), 1) // INV_BLOCK
    eye = (lax.broadcasted_iota(jnp.int32, (c, c), 0)
           == lax.broadcasted_iota(jnp.int32, (c, c), 1)).astype(F32)
    d1 = jnp.where(bi == bj, low, 0.0)
    d2 = _dot(d1, d1, HIGHEST)
    d4 = _dot(d2, d2, HIGHEST)
    d8 = _dot(d4, d4, HIGHEST)
    p = eye - d1
    p = p + _dot(p, d2, HIGHEST)
    p = p + _dot(p, d4, HIGHEST)
    p = p + _dot(p, d8, HIGHEST)
    e1 = jnp.where((bi // 2 == bj // 2) & (bi != bj), low, 0.0)
    p = p - _dot(_dot(p, e1, HIGHEST), p, HIGHEST)
    e2 = jnp.where(bi // 2 != bj // 2, low, 0.0)
    return p - _dot(_dot(p, e2, HIGHEST), p, HIGHEST)


def _delta_kernel(*refs, zero_init, emit_state):
    (qkv_f_ref, qkv_b_ref, ba_f_ref, ba_b_ref, bat_f_ref, bat_b_ref,
     alog_ref, dtb_ref, alog_t_ref, dtb_t_ref) = refs[:10]
    rest = list(refs[10:])
    s0_ref = None if zero_init else rest.pop(0)
    o_f_ref, o_b_ref = rest.pop(0), rest.pop(0)
    sfin_ref = rest.pop(0) if emit_state else None
    s_ref = rest.pop(0)

    n = pl.program_id(1)
    n_chunks = pl.num_programs(1)

    @pl.when(n == 0)
    def _():
        if zero_init:
            s_ref[...] = jnp.zeros(s_ref.shape, F32)
        else:
            s_ref[...] = s0_ref[0]

    c = CHUNK
    row = lax.broadcasted_iota(jnp.int32, (c, c), 0)
    col = lax.broadcasted_iota(jnp.int32, (c, c), 1)
    tril = (row >= col).astype(F32)
    triu = (row <= col).astype(F32)

    for d, (qkv_ref, ba_ref, bat_ref, o_ref) in enumerate(
            ((qkv_f_ref, ba_f_ref, bat_f_ref, o_f_ref),
             (qkv_b_ref, ba_b_ref, bat_b_ref, o_b_ref))):
        fwd = d == 0
        incl = (row >= col) if fwd else (row <= col)
        strict = (row > col) if fwd else (row < col)
        csum, csum_t = (tril, triu) if fwd else (triu, tril)
        last = c - 1 if fwd else 0

        ba = ba_ref[0]
        bat = bat_ref[0]
        beta_all = jax.nn.sigmoid(ba[:, 0:2 * H_A])
        g_all = -jnp.exp(alog_ref[...]) * jax.nn.softplus(ba[:, 2 * H_A:] + dtb_ref[...])
        g_all_t = -jnp.exp(alog_t_ref[...]) * jax.nn.softplus(bat[2 * H_A:, :] + dtb_t_ref[...])
        gc_all = _dot(csum, g_all, HIGHEST)
        gc_all_t = _dot(g_all_t, csum_t, HIGHEST)

        for h in range(H_A):
            j = d * H_A + h
            q = qkv_ref[0, :, h * DK:(h + 1) * DK]
            k = qkv_ref[0, :, W_A + h * DK:W_A + (h + 1) * DK]
            v = qkv_ref[0, :, 2 * W_A + h * DV:2 * W_A + (h + 1) * DV]
            q = q * lax.rsqrt(jnp.sum(q * q, axis=-1, keepdims=True) + EPS) * (DK ** -0.5)
            k = k * lax.rsqrt(jnp.sum(k * k, axis=-1, keepdims=True) + EPS)
            beta = beta_all[:, j:j + 1]
            gcol = gc_all[:, j:j + 1]
            grow = gc_all_t[j:j + 1, :]
            glast = gc_all[last:last + 1, j:j + 1]
            decay = jnp.where(incl, jnp.exp(jnp.where(incl, gcol - grow, 0.0)), 0.0)
            kb = k * beta
            low = jnp.where(strict, _dot_nt(kb, k, HIGHEST) * decay, 0.0)
            tmat = _unit_tri_inverse(low)
            u = _dot(tmat, v * beta, HIGHEST)
            w = _dot(tmat, kb * jnp.exp(gcol), HIGHEST)
            attn = jnp.where(incl, _dot_nt(q, k, HIGHEST) * decay, 0.0)
            k_tail = k * jnp.exp(glast - gcol)
            q_head = q * jnp.exp(gcol)

            s = s_ref[d, h]
            v_new = u - _dot(w, s, HIGHEST)
            o = _dot(q_head, s, HIGHEST) + _dot(attn, v_new, HIGHEST)
            s_ref[d, h] = s * jnp.exp(glast) + _dot_tn(k_tail, v_new, HIGHEST)
            o_ref[0, :, h * DV:(h + 1) * DV] = o

    if emit_state:
        @pl.when(n == n_chunks - 1)
        def _():
            sfin_ref[0] = s_ref[...]


def _delta_scan(qkv, ba, bat, a_log, dt_bias, s0, *, batch, emit_state):
    n_tok = qkv.shape[0]
    t = n_tok // batch
    nc = t // CHUNK
    qkv = qkv.reshape(batch, t, N_QKV)
    ba = ba.reshape(batch, t, 4 * H_A)
    zero_init = s0 is None
    alog = a_log.reshape(1, 2 * H_A)
    dtb = dt_bias.reshape(1, 2 * H_A)
    alog_t = a_log.reshape(2 * H_A, 1)
    dtb_t = dt_bias.reshape(2 * H_A, 1)

    fw = lambda b, n: (b, n, 0)
    bw = lambda b, n: (b, nc - 1 - n, 0)
    small = lambda b, n: (0, 0)
    in_specs = [pl.BlockSpec((1, CHUNK, N_QKV), fw),
                pl.BlockSpec((1, CHUNK, N_QKV), bw),
                pl.BlockSpec((1, CHUNK, 4 * H_A), fw),
                pl.BlockSpec((1, CHUNK, 4 * H_A), bw),
                pl.BlockSpec((1, 4 * H_A, CHUNK), lambda b, n: (b * nc + n, 0, 0)),
                pl.BlockSpec((1, 4 * H_A, CHUNK), lambda b, n: (b * nc + nc - 1 - n, 0, 0)),
                pl.BlockSpec((1, 2 * H_A), small),
                pl.BlockSpec((1, 2 * H_A), small),
                pl.BlockSpec((2 * H_A, 1), small),
                pl.BlockSpec((2 * H_A, 1), small)]
    args = [qkv, qkv, ba, ba, bat, bat, alog, dtb, alog_t, dtb_t]
    state_spec = pl.BlockSpec((1, 2, H_A, DK, DV), lambda b, n: (b, 0, 0, 0, 0))
    if not zero_init:
        in_specs.append(state_spec)
        args.append(s0)
    out_specs = [pl.BlockSpec((1, CHUNK, W_A), fw), pl.BlockSpec((1, CHUNK, W_A), bw)]
    out_shape = [jax.ShapeDtypeStruct((batch, t, W_A), F32),
                 jax.ShapeDtypeStruct((batch, t, W_A), F32)]
    if emit_state:
        out_specs.append(state_spec)
        out_shape.append(jax.ShapeDtypeStruct((batch, 2, H_A, DK, DV), F32))
    return pl.pallas_call(
        functools.partial(_delta_kernel, zero_init=zero_init, emit_state=emit_state),
        grid=(batch, nc),
        in_specs=in_specs,
        out_specs=out_specs,
        out_shape=out_shape,
        scratch_shapes=[pltpu.VMEM((2, H_A, DK, DV), F32)],
        compiler_params=pltpu.CompilerParams(dimension_semantics=("arbitrary", "arbitrary"),
                                             vmem_limit_bytes=VMEM_LIMIT),
        name="delta_scan",
    )(*args)


def _mix_kernel(x_ref, mod_ref, gain_ref, of_ref, ob_ref, wz_ref, wsc_ref, wg_ref, conv_ref,
                onorm_ref, hsum_ref, wa_ref, wb_ref, wo_ref, out_ref, *, row_len):
    x = x_ref[...]
    h = _rms(x, gain_ref[0:1, :]) * (1.0 + mod_ref[0, 1:2, :]) + mod_ref[0, 0:1, :]
    hb = h.astype(BF16)
    o = of_ref[...] + ob_ref[...]
    ms = _dot(o * o, hsum_ref[...], HIGHEST) * (1.0 / DV)
    z = _dot(hb, wz_ref[...])
    oa = o * lax.rsqrt(ms + EPS) * onorm_ref[...] * _silu(z)
    y_a = _dot(oa.astype(BF16), wa_ref[...])
    sc = _dot(hb, wsc_ref[...])
    bg, cg, xs = sc[:, 0:W_B], sc[:, W_B:2 * W_B], sc[:, 2 * W_B:3 * W_B]
    yb_in = bg * _conv3_rows(cg * xs, conv_ref, row_len)
    y_b = _dot(yb_in.astype(BF16), wb_ref[...])
    gates = _dot(hb, wg_ref[...])
    m = jax.nn.sigmoid(gates[:, 0:D_MODEL]) * y_a + jax.nn.sigmoid(gates[:, D_MODEL:]) * y_b
    y = _dot(m.astype(BF16), wo_ref[...])
    out_ref[...] = x + mod_ref[0, 2:3, :] * _rms(y, gain_ref[1:2, :])


def _mix(x, mod3, gains, o_f, o_b, w_z, w_sc, w_g, conv_sc, onorm_t, head_sum, w_a, w_b, w_o,
         *, mod_row_fn, row_len):
    n_tok = x.shape[0]
    tm = TOKEN_TILE
    const = lambda i: (0, 0)
    tok = lambda i: (i, 0)
    return pl.pallas_call(
        functools.partial(_mix_kernel, row_len=row_len),
        grid=(n_tok // tm,),
        in_specs=[pl.BlockSpec((tm, D_MODEL), tok),
                  pl.BlockSpec((1, 6, D_MODEL), lambda i: (mod_row_fn(i), 0, 0)),
                  pl.BlockSpec((4, D_MODEL), const),
                  pl.BlockSpec((tm, W_A), tok),
                  pl.BlockSpec((tm, W_A), tok),
                  pl.BlockSpec((D_MODEL, W_A), const),
                  pl.BlockSpec((D_MODEL, 3 * W_B), const),
                  pl.BlockSpec((D_MODEL, 2 * D_MODEL), const),
                  pl.BlockSpec((3, W_B), const),
                  pl.BlockSpec((1, W_A), const),
                  pl.BlockSpec((W_A, W_A), const),
                  pl.BlockSpec((W_A, D_MODEL), const),
                  pl.BlockSpec((W_B, D_MODEL), const),
                  pl.BlockSpec((D_MODEL, D_MODEL), const)],
        out_specs=pl.BlockSpec((tm, D_MODEL), tok),
        out_shape=jax.ShapeDtypeStruct((n_tok, D_MODEL), F32),
        compiler_params=pltpu.CompilerParams(dimension_semantics=("arbitrary",),
                                             vmem_limit_bytes=VMEM_LIMIT),
        name="mix",
    )(x, mod3, gains, o_f, o_b, w_z, w_sc, w_g, conv_sc, onorm_t, head_sum, w_a, w_b, w_o)


def _ffn_kernel(x_ref, mod_ref, gain_ref, win_ref, wout_ref, out_ref):
    x = x_ref[...]
    h = _rms(x, gain_ref[2:3, :]) * (1.0 + mod_ref[0, 4:5, :]) + mod_ref[0, 3:4, :]
    hb = h.astype(BF16)
    gate = _dot(hb, win_ref[:, 0:D_FF])
    up = _dot(hb, win_ref[:, D_FF:])
    y = _dot((_silu(gate) * up).astype(BF16), wout_ref[...])
    out_ref[...] = x + mod_ref[0, 5:6, :] * _rms(y, gain_ref[3:4, :])


def _ffn(x, mod3, gains, w_in, w_out, *, mod_row_fn):
    n_tok = x.shape[0]
    tm = TOKEN_TILE
    const = lambda i: (0, 0)
    tok = lambda i: (i, 0)
    return pl.pallas_call(
        _ffn_kernel,
        grid=(n_tok // tm,),
        in_specs=[pl.BlockSpec((tm, D_MODEL), tok),
                  pl.BlockSpec((1, 6, D_MODEL), lambda i: (mod_row_fn(i), 0, 0)),
                  pl.BlockSpec((4, D_MODEL), const),
                  pl.BlockSpec((D_MODEL, 2 * D_FF), const),
                  pl.BlockSpec((D_FF, D_MODEL), const)],
        out_specs=pl.BlockSpec((tm, D_MODEL), tok),
        out_shape=jax.ShapeDtypeStruct((n_tok, D_MODEL), F32),
        compiler_params=pltpu.CompilerParams(dimension_semantics=("arbitrary",),
                                             vmem_limit_bytes=VMEM_LIMIT),
        name="ffn",
    )(x, mod3, gains, w_in, w_out)


def kernel(x_prompt, x_sample, state_delta, c, c_ctx, w_mod, b_mod, norm_gains, w_in, conv_qkv,
           a_log, dt_bias, onorm, conv_sc, w_branch_a, w_branch_b, w_out, w_ffn_in, w_ffn_out):
    depth = w_in.shape[0]
    assert depth == 1
    b_ctx, t_ctx, _ = x_prompt.shape
    b_lat, t_lat, _ = x_sample.shape
    assert t_ctx % TOKEN_TILE == 0 and t_lat % TOKEN_TILE == 0 and TOKEN_TILE % GRID_W == 0
    assert b_lat + 1 <= MOD_ROWS
    layer = 0

    cvec = jnp.zeros((MOD_ROWS, D_MODEL), F32).at[0].set(c_ctx).at[1:1 + b_lat].set(c)
    mod3 = _modulation(cvec, w_mod[layer], b_mod[layer][None, :]).reshape(MOD_ROWS, 6, D_MODEL)

    o_qkv, o_z = N_QKV, N_QKV + W_A
    o_beta, o_sc = o_z + 4 * H_A, o_z + 4 * H_A + 3 * W_B
    wi = w_in[layer]
    w_qkv = wi[:, :o_qkv].astype(BF16)
    w_z = wi[:, o_qkv:o_z].astype(BF16)
    w_ba = wi[:, o_z:o_beta].astype(BF16)
    w_bat = w_ba.T
    w_sc = wi[:, o_beta:o_sc].astype(BF16)
    w_g = wi[:, o_sc:].astype(BF16)
    w_a = w_branch_a[layer].astype(BF16)
    w_b = w_branch_b[layer].astype(BF16)
    w_o = w_out[layer].astype(BF16)
    w_f1 = w_ffn_in[layer].astype(BF16)
    w_f2 = w_ffn_out[layer].astype(BF16)
    gains = norm_gains[layer]
    onorm_t = jnp.tile(onorm[layer], H_A)[None, :]
    lane_head = jnp.arange(W_A) // DV
    head_sum = (lane_head[:, None] == lane_head[None, :]).astype(F32)

    def run_group(x3, s0, mod_row_fn, row_len, emit_state):
        batch, t, _ = x3.shape
        x = x3.reshape(batch * t, D_MODEL)
        qkv, ba, bat = _inproj(x, mod3, gains, w_qkv, w_ba, w_bat, conv_qkv[layer],
                               mod_row_fn=mod_row_fn, row_len=row_len)
        res = _delta_scan(qkv, ba, bat, a_log[layer], dt_bias[layer], s0,
                          batch=batch, emit_state=emit_state)
        o_f = res[0].reshape(batch * t, W_A)
        o_b = res[1].reshape(batch * t, W_A)
        x1 = _mix(x, mod3, gains, o_f, o_b, w_z, w_sc, w_g, conv_sc[layer], onorm_t, head_sum,
                  w_a, w_b, w_o, mod_row_fn=mod_row_fn, row_len=row_len)
        x2 = _ffn(x1, mod3, gains, w_f1, w_f2, mod_row_fn=mod_row_fn)
        return x2.reshape(batch, t, D_MODEL), (res[2] if emit_state else None)

    tiles_per_lat_seq = t_lat // TOKEN_TILE
    y_p, s_fin = run_group(x_prompt, None, lambda i: 0, t_ctx, True)
    y_s, _ = run_group(x_sample, state_delta[:, layer], lambda i: 1 + i // tiles_per_lat_seq,
                       GRID_W, False)
    new_state = s_fin[:, None].astype(x_prompt.dtype)
    return (y_p, y_s, new_state)
```

```python
import functools

import jax
import jax.numpy as jnp
from jax import lax
from jax.experimental import pallas as pl
from jax.experimental.pallas import tpu as pltpu

D_MODEL = 1024
H_A = 8
DK = 64
DV = 64
W_A = H_A * DV
W_B = 512
CHUNK = 64
GRID_W = 64
EPS = 1e-6
N_QKV = 3 * H_A * DK
D_FF = 2816
INV_BLOCK = 16
INV_PASSES = 3
MOD_ROWS = 8
TOKEN_TILE = 256
DELTA_BATCH = 1
VMEM_LIMIT = 56 * 1024 * 1024

F32 = jnp.float32
BF16 = jnp.bfloat16
HIGHEST = lax.Precision.HIGHEST


def _dot(a, b, precision=None):
    return jnp.dot(a, b, preferred_element_type=F32, precision=precision)


def _dot_nt(a, b, precision=None):
    return lax.dot_general(a, b, (((1,), (1,)), ((), ())), preferred_element_type=F32,
                           precision=precision)


def _dot_tn(a, b, precision=None):
    return lax.dot_general(a, b, (((0,), (0,)), ((), ())), preferred_element_type=F32,
                           precision=precision)


def _silu(x):
    return x * jax.nn.sigmoid(x)


def _rms(x, gain):
    return x * lax.rsqrt(jnp.mean(x * x, axis=-1, keepdims=True) + EPS) * gain


def _conv3_rows(x, w_ref, row_len):
    n = x.shape[0]
    pos = lax.broadcasted_iota(jnp.int32, (n, 1), 0) & (row_len - 1)
    prev = jnp.where(pos == 0, 0.0, pltpu.roll(x, 1, 0))
    nxt = jnp.where(pos == row_len - 1, 0.0, pltpu.roll(x, n - 1, 0))
    return prev * w_ref[0:1, :] + x * w_ref[1:2, :] + nxt * w_ref[2:3, :]


def _mod_kernel(c_ref, w_ref, b_ref, o_ref):
    o_ref[...] = _dot(_silu(c_ref[...]), w_ref[...], HIGHEST) + b_ref[...]


def _modulation(cvec, w_mod, b_mod):
    n_out = w_mod.shape[1]
    blk = D_MODEL
    return pl.pallas_call(
        _mod_kernel,
        grid=(n_out // blk,),
        in_specs=[pl.BlockSpec((MOD_ROWS, D_MODEL), lambda j: (0, 0)),
                  pl.BlockSpec((D_MODEL, blk), lambda j: (0, j)),
                  pl.BlockSpec((1, blk), lambda j: (0, j))],
        out_specs=pl.BlockSpec((MOD_ROWS, blk), lambda j: (0, j)),
        out_shape=jax.ShapeDtypeStruct((MOD_ROWS, n_out), F32),
        name="modulation",
    )(cvec, w_mod, b_mod)


def _inproj_kernel(x_ref, mod_ref, gain_ref, wqkv_ref, wba_ref, wbat_ref, conv_ref,
                   qkv_ref, ba_ref, bat_ref, *, row_len):
    x = x_ref[...]
    h = _rms(x, gain_ref[0:1, :]) * (1.0 + mod_ref[0, 1:2, :]) + mod_ref[0, 0:1, :]
    hb = h.astype(BF16)
    qkv = _conv3_rows(_dot(hb, wqkv_ref[...]), conv_ref, row_len)
    qkv_ref[...] = _silu(qkv)
    ba_ref[...] = _dot(hb, wba_ref[...])
    for ci in range(x.shape[0] // CHUNK):
        bat_ref[ci] = _dot_nt(wbat_ref[...], hb[ci * CHUNK:(ci + 1) * CHUNK, :])


def _inproj(x, mod3, gains, w_qkv, w_ba, w_bat, conv_qkv, *, mod_row_fn, row_len):
    n_tok = x.shape[0]
    tm = TOKEN_TILE
    const = lambda i: (0, 0)
    return pl.pallas_call(
        functools.partial(_inproj_kernel, row_len=row_len),
        grid=(n_tok // tm,),
        in_specs=[pl.BlockSpec((tm, D_MODEL), lambda i: (i, 0)),
                  pl.BlockSpec((1, 6, D_MODEL), lambda i: (mod_row_fn(i), 0, 0)),
                  pl.BlockSpec((4, D_MODEL), const),
                  pl.BlockSpec((D_MODEL, N_QKV), const),
                  pl.BlockSpec((D_MODEL, 4 * H_A), const),
                  pl.BlockSpec((4 * H_A, D_MODEL), const),
                  pl.BlockSpec((3, N_QKV), const)],
        out_specs=[pl.BlockSpec((tm, N_QKV), lambda i: (i, 0)),
                   pl.BlockSpec((tm, 4 * H_A), lambda i: (i, 0)),
                   pl.BlockSpec((tm // CHUNK, 4 * H_A, CHUNK), lambda i: (i, 0, 0))],
        out_shape=[jax.ShapeDtypeStruct((n_tok, N_QKV), F32),
                   jax.ShapeDtypeStruct((n_tok, 4 * H_A), F32),
                   jax.ShapeDtypeStruct((n_tok // CHUNK, 4 * H_A, CHUNK), F32)],
        compiler_params=pltpu.CompilerParams(dimension_semantics=("arbitrary",),
                                             vmem_limit_bytes=VMEM_LIMIT),
        name="inproj",
    )(x, mod3, gains, w_qkv, w_ba, w_bat, conv_qkv)


def _parts(x):
    hi = x.astype(BF16)
    if INV_PASSES == 1:
        return (hi,)
    return (hi, (x - hi.astype(F32)).astype(BF16))


def _mm_parts(a, b):
    acc = _dot(a[0], b[0])
    if len(a) > 1:
        acc = acc + (_dot(a[1], b[0]) + _dot(a[0], b[1]))
    return acc


def _stack_parts(a, b):
    return tuple(jnp.concatenate([x, y], axis=0) for x, y in zip(a, b))


def _unit_tri_inverse_all(lows):
    c = CHUNK
    ri = lax.broadcasted_iota(jnp.int32, (c, c), 0)
    ci = lax.broadcasted_iota(jnp.int32, (c, c), 1)
    bi, bj = ri // INV_BLOCK, ci // INV_BLOCK
    eye = (ri == ci).astype(F32)
    diag_blk = bi == bj
    pair_blk = (bi // 2 == bj // 2) & (bi != bj)
    far_blk = bi // 2 != bj // 2
    n = len(lows)
    rng = range(n)

    d1 = [jnp.where(diag_blk, lows[i], 0.0) for i in rng]
    d1p = [_parts(d1[i]) for i in rng]
    p0 = [eye - d1[i] for i in rng]
    d2 = [_mm_parts(d1p[i], d1p[i]) for i in rng]
    d2p = [_parts(d2[i]) for i in rng]
    r = [_mm_parts(_stack_parts(d2p[i], _parts(p0[i])), d2p[i]) for i in rng]
    d4p = [_parts(r[i][:c]) for i in rng]
    p1 = [p0[i] + r[i][c:] for i in rng]
    r = [_mm_parts(_stack_parts(d4p[i], _parts(p1[i])), d4p[i]) for i in rng]
    d8p = [_parts(r[i][:c]) for i in rng]
    p2 = [p1[i] + r[i][c:] for i in rng]
    p3 = [p2[i] + _mm_parts(_parts(p2[i]), d8p[i]) for i in rng]

    e1p = [_parts(jnp.where(pair_blk, lows[i], 0.0)) for i in rng]
    p3p = [_parts(p3[i]) for i in rng]
    a = [_mm_parts(p3p[i], e1p[i]) for i in rng]
    p4 = [p3[i] - _mm_parts(_parts(a[i]), p3p[i]) for i in rng]
    e2p = [_parts(jnp.where(far_blk, lows[i], 0.0)) for i in rng]
    p4p = [_parts(p4[i]) for i in rng]
    b = [_mm_parts(p4p[i], e2p[i]) for i in rng]
    return [p4[i] - _mm_parts(_parts(b[i]), p4p[i]) for i in rng]


def _delta_kernel(*refs, zero_init, emit_state):
    (qkv_f_ref, qkv_b_ref, ba_f_ref, ba_b_ref, bat_f_ref, bat_b_ref,
     alog_ref, dtb_ref, alog_t_ref, dtb_t_ref) = refs[:10]
    rest = list(refs[10:])
    s0_ref = None if zero_init else rest.pop(0)
    o_f_ref, o_b_ref = rest.pop(0), rest.pop(0)
    sfin_ref = rest.pop(0) if emit_state else None
    s_ref = rest.pop(0)

    n = pl.program_id(1)
    n_chunks = pl.num_programs(1)

    @pl.when(n == 0)
    def _():
        if zero_init:
            s_ref[...] = jnp.zeros(s_ref.shape, F32)
        else:
            s_ref[...] = s0_ref[...]

    c = CHUNK
    row = lax.broadcasted_iota(jnp.int32, (c, c), 0)
    col = lax.broadcasted_iota(jnp.int32, (c, c), 1)
    tril = (row >= col).astype(F32)
    triu = (row <= col).astype(F32)

    inst = []
    for bb in range(s_ref.shape[0]):
        for d, (qkv_ref, ba_ref, bat_ref, o_ref) in enumerate(
                ((qkv_f_ref, ba_f_ref, bat_f_ref, o_f_ref),
                 (qkv_b_ref, ba_b_ref, bat_b_ref, o_b_ref))):
            fwd = d == 0
            incl = (row >= col) if fwd else (row <= col)
            strict = (row > col) if fwd else (row < col)
            csum, csum_t = (tril, triu) if fwd else (triu, tril)
            last = c - 1 if fwd else 0

            ba = ba_ref[bb]
            bat = bat_ref[bb, 0]
            beta_all = jax.nn.sigmoid(ba[:, 0:2 * H_A])
            g_all = -jnp.exp(alog_ref[...]) * jax.nn.softplus(ba[:, 2 * H_A:] + dtb_ref[...])
            g_all_t = -jnp.exp(alog_t_ref[...]) * jax.nn.softplus(
                bat[2 * H_A:, :] + dtb_t_ref[...])
            gc_all = _dot(csum, g_all, HIGHEST)
            gc_all_t = _dot(g_all_t, csum_t, HIGHEST)
            glast_all = gc_all[last:last + 1, :]
            eg_all = jnp.exp(gc_all)
            tail_all = jnp.exp(glast_all - gc_all)
            eglast_all = jnp.exp(glast_all)

            for h in range(H_A):
                j = d * H_A + h
                q = qkv_ref[bb, :, h * DK:(h + 1) * DK]
                k = qkv_ref[bb, :, W_A + h * DK:W_A + (h + 1) * DK]
                v = qkv_ref[bb, :, 2 * W_A + h * DV:2 * W_A + (h + 1) * DV]
                q = q * (lax.rsqrt(jnp.sum(q * q, axis=-1, keepdims=True) + EPS) * (DK ** -0.5))
                k = k * lax.rsqrt(jnp.sum(k * k, axis=-1, keepdims=True) + EPS)
                beta = jnp.broadcast_to(beta_all[:, j:j + 1], (c, DK))
                eg = jnp.broadcast_to(eg_all[:, j:j + 1], (c, DK))
                tail = jnp.broadcast_to(tail_all[:, j:j + 1], (c, DK))
                diff = gc_all[:, j:j + 1] - gc_all_t[j:j + 1, :]
                decay = jnp.where(incl, jnp.exp(jnp.where(incl, diff, 0.0)), 0.0)
                kb = k * beta
                inst.append(dict(
                    bb=bb, d=d, h=h, o_ref=o_ref, strict=strict, decay=decay,
                    k16=k.astype(BF16),
                    kbq16=jnp.concatenate([kb, q], axis=0).astype(BF16),
                    vb16=(v * beta).astype(BF16),
                    kbg16=(kb * eg).astype(BF16),
                    qh=q * eg,
                    kt16=(k * tail).astype(BF16),
                    eglast=eglast_all[:, j:j + 1]))
    rng = range(len(inst))

    kq = [_dot_nt(inst[i]["kbq16"], inst[i]["k16"]) for i in rng]
    lows = [jnp.where(inst[i]["strict"], kq[i][:c] * inst[i]["decay"], 0.0) for i in rng]
    attn16 = [(kq[i][c:] * inst[i]["decay"]).astype(BF16) for i in rng]
    tmat16 = [t.astype(BF16) for t in _unit_tri_inverse_all(lows)]
    u = [_dot(tmat16[i], inst[i]["vb16"]) for i in rng]
    w = [_dot(tmat16[i], inst[i]["kbg16"]) for i in rng]

    s_old = [s_ref[inst[i]["bb"], inst[i]["d"], inst[i]["h"]] for i in rng]
    wq = [_dot(jnp.concatenate([w[i], inst[i]["qh"]], axis=0).astype(BF16),
               s_old[i].astype(BF16)) for i in rng]
    v_new16 = [(u[i] - wq[i][:c]).astype(BF16) for i in rng]
    o = [wq[i][c:] + _dot(attn16[i], v_new16[i]) for i in rng]
    s_new = [s_old[i] * inst[i]["eglast"] + _dot_tn(inst[i]["kt16"], v_new16[i]) for i in rng]
    for i in rng:
        it = inst[i]
        s_ref[it["bb"], it["d"], it["h"]] = s_new[i]
        it["o_ref"][it["bb"], :, it["h"] * DV:(it["h"] + 1) * DV] = o[i]

    if emit_state:
        @pl.when(n == n_chunks - 1)
        def _():
            sfin_ref[...] = s_ref[...]


def _delta_scan(qkv, ba, bat, a_log, dt_bias, s0, *, batch, emit_state):
    n_tok = qkv.shape[0]
    t = n_tok // batch
    nc = t // CHUNK
    nb = DELTA_BATCH
    assert batch % nb == 0
    qkv = qkv.reshape(batch, t, N_QKV)
    ba = ba.reshape(batch, t, 4 * H_A)
    bat = bat.reshape(batch, nc, 4 * H_A, CHUNK)
    zero_init = s0 is None
    alog = a_log.reshape(1, 2 * H_A)
    dtb = dt_bias.reshape(1, 2 * H_A)
    alog_t = a_log.reshape(2 * H_A, 1)
    dtb_t = dt_bias.reshape(2 * H_A, 1)

    fw = lambda b, n: (b, n, 0)
    bw = lambda b, n: (b, nc - 1 - n, 0)
    small = lambda b, n: (0, 0)
    in_specs = [pl.BlockSpec((nb, CHUNK, N_QKV), fw),
                pl.BlockSpec((nb, CHUNK, N_QKV), bw),
                pl.BlockSpec((nb, CHUNK, 4 * H_A), fw),
                pl.BlockSpec((nb, CHUNK, 4 * H_A), bw),
                pl.BlockSpec((nb, 1, 4 * H_A, CHUNK), lambda b, n: (b, n, 0, 0)),
                pl.BlockSpec((nb, 1, 4 * H_A, CHUNK), lambda b, n: (b, nc - 1 - n, 0, 0)),
                pl.BlockSpec((1, 2 * H_A), small),
                pl.BlockSpec((1, 2 * H_A), small),
                pl.BlockSpec((2 * H_A, 1), small),
                pl.BlockSpec((2 * H_A, 1), small)]
    args = [qkv, qkv, ba, ba, bat, bat, alog, dtb, alog_t, dtb_t]
    state_spec = pl.BlockSpec((nb, 2, H_A, DK, DV), lambda b, n: (b, 0, 0, 0, 0))
    if not zero_init:
        in_specs.append(state_spec)
        args.append(s0)
    out_specs = [pl.BlockSpec((nb, CHUNK, W_A), fw), pl.BlockSpec((nb, CHUNK, W_A), bw)]
    out_shape = [jax.ShapeDtypeStruct((batch, t, W_A), F32),
                 jax.ShapeDtypeStruct((batch, t, W_A), F32)]
    if emit_state:
        out_specs.append(state_spec)
        out_shape.append(jax.ShapeDtypeStruct((batch, 2, H_A, DK, DV), F32))
    return pl.pallas_call(
        functools.partial(_delta_kernel, zero_init=zero_init, emit_state=emit_state),
        grid=(batch // nb, nc),
        in_specs=in_specs,
        out_specs=out_specs,
        out_shape=out_shape,
        scratch_shapes=[pltpu.VMEM((nb, 2, H_A, DK, DV), F32)],
        compiler_params=pltpu.CompilerParams(dimension_semantics=("arbitrary", "arbitrary"),
                                             vmem_limit_bytes=VMEM_LIMIT),
        name="delta_scan",
    )(*args)


def _mix_kernel(x_ref, mod_ref, gain_ref, of_ref, ob_ref, wz_ref, wsc_ref, wg_ref, conv_ref,
                onorm_ref, hsum_ref, wa_ref, wb_ref, wo_ref, out_ref, *, row_len):
    x = x_ref[...]
    h = _rms(x, gain_ref[0:1, :]) * (1.0 + mod_ref[0, 1:2, :]) + mod_ref[0, 0:1, :]
    hb = h.astype(BF16)
    o = of_ref[...] + ob_ref[...]
    ms = _dot(o * o, hsum_ref[...], HIGHEST) * (1.0 / DV)
    z = _dot(hb, wz_ref[...])
    oa = o * lax.rsqrt(ms + EPS) * onorm_ref[...] * _silu(z)
    y_a = _dot(oa.astype(BF16), wa_ref[...])
    sc = _dot(hb, wsc_ref[...])
    bg, cg, xs = sc[:, 0:W_B], sc[:, W_B:2 * W_B], sc[:, 2 * W_B:3 * W_B]
    yb_in = bg * _conv3_rows(cg * xs, conv_ref, row_len)
    y_b = _dot(yb_in.astype(BF16), wb_ref[...])
    gates = _dot(hb, wg_ref[...])
    m = jax.nn.sigmoid(gates[:, 0:D_MODEL]) * y_a + jax.nn.sigmoid(gates[:, D_MODEL:]) * y_b
    y = _dot(m.astype(BF16), wo_ref[...])
    out_ref[...] = x + mod_ref[0, 2:3, :] * _rms(y, gain_ref[1:2, :])


def _mix(x, mod3, gains, o_f, o_b, w_z, w_sc, w_g, conv_sc, onorm_t, head_sum, w_a, w_b, w_o,
         *, mod_row_fn, row_len):
    n_tok = x.shape[0]
    tm = TOKEN_TILE
    const = lambda i: (0, 0)
    tok = lambda i: (i, 0)
    return pl.pallas_call(
        functools.partial(_mix_kernel, row_len=row_len),
        grid=(n_tok // tm,),
        in_specs=[pl.BlockSpec((tm, D_MODEL), tok),
                  pl.BlockSpec((1, 6, D_MODEL), lambda i: (mod_row_fn(i), 0, 0)),
                  pl.BlockSpec((4, D_MODEL), const),
                  pl.BlockSpec((tm, W_A), tok),
                  pl.BlockSpec((tm, W_A), tok),
                  pl.BlockSpec((D_MODEL, W_A), const),
                  pl.BlockSpec((D_MODEL, 3 * W_B), const),
                  pl.BlockSpec((D_MODEL, 2 * D_MODEL), const),
                  pl.BlockSpec((3, W_B), const),
                  pl.BlockSpec((1, W_A), const),
                  pl.BlockSpec((W_A, W_A), const),
                  pl.BlockSpec((W_A, D_MODEL), const),
                  pl.BlockSpec((W_B, D_MODEL), const),
                  pl.BlockSpec((D_MODEL, D_MODEL), const)],
        out_specs=pl.BlockSpec((tm, D_MODEL), tok),
        out_shape=jax.ShapeDtypeStruct((n_tok, D_MODEL), F32),
        compiler_params=pltpu.CompilerParams(dimension_semantics=("arbitrary",),
                                             vmem_limit_bytes=VMEM_LIMIT),
        name="mix",
    )(x, mod3, gains, o_f, o_b, w_z, w_sc, w_g, conv_sc, onorm_t, head_sum, w_a, w_b, w_o)


def _ffn_kernel(x_ref, mod_ref, gain_ref, win_ref, wout_ref, out_ref):
    x = x_ref[...]
    h = _rms(x, gain_ref[2:3, :]) * (1.0 + mod_ref[0, 4:5, :]) + mod_ref[0, 3:4, :]
    hb = h.astype(BF16)
    gate = _dot(hb, win_ref[:, 0:D_FF])
    up = _dot(hb, win_ref[:, D_FF:])
    y = _dot((_silu(gate) * up).astype(BF16), wout_ref[...])
    out_ref[...] = x + mod_ref[0, 5:6, :] * _rms(y, gain_ref[3:4, :])


def _ffn(x, mod3, gains, w_in, w_out, *, mod_row_fn):
    n_tok = x.shape[0]
    tm = TOKEN_TILE
    const = lambda i: (0, 0)
    tok = lambda i: (i, 0)
    return pl.pallas_call(
        _ffn_kernel,
        grid=(n_tok // tm,),
        in_specs=[pl.BlockSpec((tm, D_MODEL), tok),
                  pl.BlockSpec((1, 6, D_MODEL), lambda i: (mod_row_fn(i), 0, 0)),
                  pl.BlockSpec((4, D_MODEL), const),
                  pl.BlockSpec((D_MODEL, 2 * D_FF), const),
                  pl.BlockSpec((D_FF, D_MODEL), const)],
        out_specs=pl.BlockSpec((tm, D_MODEL), tok),
        out_shape=jax.ShapeDtypeStruct((n_tok, D_MODEL), F32),
        compiler_params=pltpu.CompilerParams(dimension_semantics=("arbitrary",),
                                             vmem_limit_bytes=VMEM_LIMIT),
        name="ffn",
    )(x, mod3, gains, w_in, w_out)


def kernel(x_prompt, x_sample, state_delta, c, c_ctx, w_mod, b_mod, norm_gains, w_in, conv_qkv,
           a_log, dt_bias, onorm, conv_sc, w_branch_a, w_branch_b, w_out, w_ffn_in, w_ffn_out):
    depth = w_in.shape[0]
    assert depth == 1
    b_ctx, t_ctx, _ = x_prompt.shape
    b_lat, t_lat, _ = x_sample.shape
    assert t_ctx % TOKEN_TILE == 0 and t_lat % TOKEN_TILE == 0 and TOKEN_TILE % GRID_W == 0
    assert t_ctx & (t_ctx - 1) == 0 and TOKEN_TILE % t_ctx == 0
    assert b_lat + 1 <= MOD_ROWS
    layer = 0

    cvec = jnp.zeros((MOD_ROWS, D_MODEL), F32).at[0].set(c_ctx).at[1:1 + b_lat].set(c)
    mod3 = _modulation(cvec, w_mod[layer], b_mod[layer][None, :]).reshape(MOD_ROWS, 6, D_MODEL)

    o_qkv, o_z = N_QKV, N_QKV + W_A
    o_beta, o_sc = o_z + 4 * H_A, o_z + 4 * H_A + 3 * W_B
    wi = w_in[layer]
    w_qkv = wi[:, :o_qkv].astype(BF16)
    w_z = wi[:, o_qkv:o_z].astype(BF16)
    w_ba = wi[:, o_z:o_beta].astype(BF16)
    w_bat = w_ba.T
    w_sc = wi[:, o_beta:o_sc].astype(BF16)
    w_g = wi[:, o_sc:].astype(BF16)
    w_a = w_branch_a[layer].astype(BF16)
    w_b = w_branch_b[layer].astype(BF16)
    w_o = w_out[layer].astype(BF16)
    w_f1 = w_ffn_in[layer].astype(BF16)
    w_f2 = w_ffn_out[layer].astype(BF16)
    gains = norm_gains[layer]
    onorm_t = jnp.tile(onorm[layer], H_A)[None, :]
    lane_head = jnp.arange(W_A) // DV
    head_sum = (lane_head[:, None] == lane_head[None, :]).astype(F32)

    def run_group(x3, s0, mod_row_fn, row_len, emit_state):
        batch, t, _ = x3.shape
        x = x3.reshape(batch * t, D_MODEL)
        qkv, ba, bat = _inproj(x, mod3, gains, w_qkv, w_ba, w_bat, conv_qkv[layer],
                               mod_row_fn=mod_row_fn, row_len=row_len)
        res = _delta_scan(qkv, ba, bat, a_log[layer], dt_bias[layer], s0,
                          batch=batch, emit_state=emit_state)
        o_f = res[0].reshape(batch * t, W_A)
        o_b = res[1].reshape(batch * t, W_A)
        x1 = _mix(x, mod3, gains, o_f, o_b, w_z, w_sc, w_g, conv_sc[layer], onorm_t, head_sum,
                  w_a, w_b, w_o, mod_row_fn=mod_row_fn, row_len=row_len)
        x2 = _ffn(x1, mod3, gains, w_f1, w_f2, mod_row_fn=mod_row_fn)
        return x2.reshape(batch, t, D_MODEL), (res[2] if emit_state else None)

    tiles_per_lat_seq = t_lat // TOKEN_TILE
    y_p, s_fin = run_group(x_prompt, None, lambda i: 0, t_ctx, True)
    y_s, _ = run_group(x_sample, state_delta[:, layer], lambda i: 1 + i // tiles_per_lat_seq,
                       GRID_W, False)
    new_state = s_fin[:, None].astype(x_prompt.dtype)
    return (y_p, y_s, new_state)
```

```python
import functools

import jax
import jax.numpy as jnp
from jax import lax
from jax.experimental import pallas as pl
from jax.experimental.pallas import tpu as pltpu

D_MODEL = 1024
H_A = 8
DK = 64
DV = 64
W_A = H_A * DV
W_B = 512
CHUNK = 64
GRID_W = 64
EPS = 1e-6
N_QKV = 3 * H_A * DK
D_FF = 2816
MOD_ROWS = 8
TOKEN_TILE = 256
DELTA_BATCH = 2
VMEM_LIMIT = 56 * 1024 * 1024

F32 = jnp.float32
BF16 = jnp.bfloat16
HIGHEST = lax.Precision.HIGHEST


def _dot(a, b, precision=None):
    return jnp.dot(a, b, preferred_element_type=F32, precision=precision)


def _dot_nt(a, b, precision=None):
    return lax.dot_general(a, b, (((1,), (1,)), ((), ())), preferred_element_type=F32,
                           precision=precision)


def _dot_tn(a, b, precision=None):
    return lax.dot_general(a, b, (((0,), (0,)), ((), ())), preferred_element_type=F32,
                           precision=precision)


def _silu(x):
    return x * jax.nn.sigmoid(x)


def _rms(x, gain):
    return x * lax.rsqrt(jnp.mean(x * x, axis=-1, keepdims=True) + EPS) * gain


def _conv3_rows(x, w_ref, row_len):
    n = x.shape[0]
    pos = lax.broadcasted_iota(jnp.int32, (n, 1), 0) & (row_len - 1)
    prev = jnp.where(pos == 0, 0.0, pltpu.roll(x, 1, 0))
    nxt = jnp.where(pos == row_len - 1, 0.0, pltpu.roll(x, n - 1, 0))
    return prev * w_ref[0:1, :] + x * w_ref[1:2, :] + nxt * w_ref[2:3, :]


def _mod_kernel(c_ref, w_ref, b_ref, o_ref):
    o_ref[...] = _dot(_silu(c_ref[...]), w_ref[...], HIGHEST) + b_ref[...]


def _modulation(cvec, w_mod, b_mod):
    n_out = w_mod.shape[1]
    blk = D_MODEL
    return pl.pallas_call(
        _mod_kernel,
        grid=(n_out // blk,),
        in_specs=[pl.BlockSpec((MOD_ROWS, D_MODEL), lambda j: (0, 0)),
                  pl.BlockSpec((D_MODEL, blk), lambda j: (0, j)),
                  pl.BlockSpec((1, blk), lambda j: (0, j))],
        out_specs=pl.BlockSpec((MOD_ROWS, blk), lambda j: (0, j)),
        out_shape=jax.ShapeDtypeStruct((MOD_ROWS, n_out), F32),
        name="modulation",
    )(cvec, w_mod, b_mod)


def _inproj_kernel(x_ref, mod_ref, gain_ref, wqkv_ref, wba_ref, wbat_ref, conv_ref,
                   qkv_ref, ba_ref, bat_ref, *, row_len):
    x = x_ref[...]
    h = _rms(x, gain_ref[0:1, :]) * (1.0 + mod_ref[0, 1:2, :]) + mod_ref[0, 0:1, :]
    hb = h.astype(BF16)
    qkv = _conv3_rows(_dot(hb, wqkv_ref[...]), conv_ref, row_len)
    qkv_ref[...] = _silu(qkv)
    ba_ref[...] = _dot(hb, wba_ref[...])
    for ci in range(x.shape[0] // CHUNK):
        bat_ref[ci] = _dot_nt(wbat_ref[...], hb[ci * CHUNK:(ci + 1) * CHUNK, :])


def _inproj(x, mod3, gains, w_qkv, w_ba, w_bat, conv_qkv, *, mod_row_fn, row_len):
    n_tok = x.shape[0]
    tm = TOKEN_TILE
    const = lambda i: (0, 0)
    return pl.pallas_call(
        functools.partial(_inproj_kernel, row_len=row_len),
        grid=(n_tok // tm,),
        in_specs=[pl.BlockSpec((tm, D_MODEL), lambda i: (i, 0)),
                  pl.BlockSpec((1, 6, D_MODEL), lambda i: (mod_row_fn(i), 0, 0)),
                  pl.BlockSpec((4, D_MODEL), const),
                  pl.BlockSpec((D_MODEL, N_QKV), const),
                  pl.BlockSpec((D_MODEL, 4 * H_A), const),
                  pl.BlockSpec((4 * H_A, D_MODEL), const),
                  pl.BlockSpec((3, N_QKV), const)],
        out_specs=[pl.BlockSpec((tm, N_QKV), lambda i: (i, 0)),
                   pl.BlockSpec((tm, 4 * H_A), lambda i: (i, 0)),
                   pl.BlockSpec((tm // CHUNK, 4 * H_A, CHUNK), lambda i: (i, 0, 0))],
        out_shape=[jax.ShapeDtypeStruct((n_tok, N_QKV), F32),
                   jax.ShapeDtypeStruct((n_tok, 4 * H_A), F32),
                   jax.ShapeDtypeStruct((n_tok // CHUNK, 4 * H_A, CHUNK), F32)],
        compiler_params=pltpu.CompilerParams(dimension_semantics=("arbitrary",),
                                             vmem_limit_bytes=VMEM_LIMIT),
        name="inproj",
    )(x, mod3, gains, w_qkv, w_ba, w_bat, conv_qkv)


def _unit_tri_inverse_all(lows):
    c = CHUNK
    ri = lax.broadcasted_iota(jnp.int32, (c, c), 0)
    ci = lax.broadcasted_iota(jnp.int32, (c, c), 1)
    eye = (ri == ci).astype(F32)
    rng = range(len(lows))
    joins = lambda b: ((ri // (2 * b)) == (ci // (2 * b))) & ((ri // b) != (ci // b))
    t = [eye - jnp.where(joins(1), lows[i], 0.0) for i in rng]
    b = 2
    while b < c:
        mask = joins(b)
        t16 = [t[i].astype(BF16) for i in rng]
        x = [_dot(t16[i], jnp.where(mask, lows[i], 0.0).astype(BF16)) for i in rng]
        t = [t[i] - _dot(x[i].astype(BF16), t16[i]) for i in rng]
        b *= 2
    return t


def _delta_kernel(*refs, zero_init, emit_state):
    (qkv_f_ref, qkv_b_ref, ba_f_ref, ba_b_ref, bat_f_ref, bat_b_ref,
     alog_ref, dtb_ref, alog_t_ref, dtb_t_ref) = refs[:10]
    rest = list(refs[10:])
    s0_ref = None if zero_init else rest.pop(0)
    o_f_ref, o_b_ref = rest.pop(0), rest.pop(0)
    sfin_ref = rest.pop(0) if emit_state else None
    s_ref = rest.pop(0)

    n = pl.program_id(1)
    n_chunks = pl.num_programs(1)
    nb = s_ref.shape[0]
    state_at = lambda h: (0, DV) if h % 2 else (DK, 0)

    @pl.when(n == 0)
    def _():
        s_ref[...] = jnp.zeros(s_ref.shape, F32)
        if not zero_init:
            for bb in range(nb):
                for d in range(2):
                    for h in range(H_A):
                        r0, l0 = state_at(h)
                        s_ref[bb, d, h, r0:r0 + DK, l0:l0 + DV] = s0_ref[bb, d, h]

    c = CHUNK
    row = lax.broadcasted_iota(jnp.int32, (c, c), 0)
    col = lax.broadcasted_iota(jnp.int32, (c, c), 1)
    tril = (row >= col).astype(F32)
    triu = (row <= col).astype(F32)
    lane = lax.broadcasted_iota(jnp.int32, (c, 2 * DK), 1)

    inst, kq = [], []
    for bb in range(nb):
        for d, (qkv_ref, ba_ref, bat_ref, o_ref) in enumerate(
                ((qkv_f_ref, ba_f_ref, bat_f_ref, o_f_ref),
                 (qkv_b_ref, ba_b_ref, bat_b_ref, o_b_ref))):
            fwd = d == 0
            incl = (row >= col) if fwd else (row <= col)
            strict = (row > col) if fwd else (row < col)
            csum, csum_t = (tril, triu) if fwd else (triu, tril)
            last = c - 1 if fwd else 0

            ba = ba_ref[bb]
            bat = bat_ref[bb, 0]
            beta_all = jax.nn.sigmoid(ba[:, 0:2 * H_A])
            g_all = -jnp.exp(alog_ref[...]) * jax.nn.softplus(ba[:, 2 * H_A:] + dtb_ref[...])
            g_all_t = -jnp.exp(alog_t_ref[...]) * jax.nn.softplus(
                bat[2 * H_A:, :] + dtb_t_ref[...])
            gc_all = _dot(csum, g_all, HIGHEST)
            gc_all_t = _dot(g_all_t, csum_t, HIGHEST)
            glast_all = gc_all[last:last + 1, :]
            eg_all = jnp.exp(gc_all)
            tail_all = jnp.exp(glast_all - gc_all)
            eglast_all = jnp.exp(glast_all)

            for h in range(H_A):
                j = d * H_A + h
                kmask = (lane < DK) if h % 2 else (lane >= DK)
                kv = qkv_ref[bb, :, W_A + h * 2 * DK:W_A + (h + 1) * 2 * DK]
                qs = qkv_ref[bb, :, (h // 2) * 2 * DK:(h // 2 + 1) * 2 * DK]
                rk = lax.rsqrt(jnp.sum(jnp.where(kmask, kv * kv, 0.0), axis=-1, keepdims=True)
                               + EPS)
                rq = lax.rsqrt(jnp.sum(jnp.where(kmask, qs * qs, 0.0), axis=-1, keepdims=True)
                               + EPS) * (DK ** -0.5)
                beta = beta_all[:, j:j + 1]
                eg = eg_all[:, j:j + 1]
                kvn = kv * jnp.where(kmask, rk, 1.0)
                kvb = kvn * beta
                kn = jnp.where(kmask, kvn, 0.0)
                qm = jnp.where(kmask, qs * rq, 0.0)
                diff = gc_all[:, j:j + 1] - gc_all_t[j:j + 1, :]
                inst.append(dict(
                    bb=bb, d=d, h=h, o_ref=o_ref, strict=strict, vmask=~kmask,
                    decay=jnp.where(incl, jnp.exp(jnp.where(incl, diff, 0.0)), 0.0),
                    uw_rhs16=(kvb * jnp.where(kmask, eg, 1.0)).astype(BF16),
                    qh=qm * eg,
                    kt16=(kn * tail_all[:, j:j + 1]).astype(BF16),
                    eglast=eglast_all[:, j:j + 1]))
                kbq16 = jnp.concatenate([jnp.where(kmask, kvb, 0.0), qm], axis=0).astype(BF16)
                kq.append(_dot_nt(kbq16, kn.astype(BF16)))
    rng = range(len(inst))

    lows = [jnp.where(inst[i]["strict"], kq[i][:c] * inst[i]["decay"], 0.0) for i in rng]
    attn16 = [(kq[i][c:] * inst[i]["decay"]).astype(BF16) for i in rng]
    tmat16 = [t.astype(BF16) for t in _unit_tri_inverse_all(lows)]
    uw = [_dot(tmat16[i], inst[i]["uw_rhs16"]) for i in rng]

    s_old = [s_ref[inst[i]["bb"], inst[i]["d"], inst[i]["h"]] for i in rng]
    wq = [_dot(jnp.concatenate([uw[i], inst[i]["qh"]], axis=0).astype(BF16),
               s_old[i].astype(BF16)) for i in rng]
    v_new16 = [jnp.where(inst[i]["vmask"], uw[i] - wq[i][:c], 0.0).astype(BF16) for i in rng]
    o = [wq[i][c:] + _dot(attn16[i], v_new16[i]) for i in rng]
    s_new = [s_old[i] * inst[i]["eglast"] + _dot_tn(inst[i]["kt16"], v_new16[i]) for i in rng]
    for i in rng:
        it = inst[i]
        s_ref[it["bb"], it["d"], it["h"]] = s_new[i]
    for i in range(0, len(inst), 2):
        it = inst[i]
        it["o_ref"][it["bb"], :, it["h"] * DV:(it["h"] + 2) * DV] = o[i] + o[i + 1]

    if emit_state:
        @pl.when(n == n_chunks - 1)
        def _():
            for bb in range(nb):
                for d in range(2):
                    for h in range(H_A):
                        r0, l0 = state_at(h)
                        sfin_ref[bb, d, h] = s_ref[bb, d, h, r0:r0 + DK, l0:l0 + DV]


def _delta_scan(qkv, ba, bat, a_log, dt_bias, s0, *, batch, emit_state):
    n_tok = qkv.shape[0]
    t = n_tok // batch
    nc = t // CHUNK
    nb = DELTA_BATCH
    assert batch % nb == 0
    qkv = qkv.reshape(batch, t, N_QKV)
    ba = ba.reshape(batch, t, 4 * H_A)
    bat = bat.reshape(batch, nc, 4 * H_A, CHUNK)
    zero_init = s0 is None
    alog = a_log.reshape(1, 2 * H_A)
    dtb = dt_bias.reshape(1, 2 * H_A)
    alog_t = a_log.reshape(2 * H_A, 1)
    dtb_t = dt_bias.reshape(2 * H_A, 1)

    fw = lambda b, n: (b, n, 0)
    bw = lambda b, n: (b, nc - 1 - n, 0)
    small = lambda b, n: (0, 0)
    in_specs = [pl.BlockSpec((nb, CHUNK, N_QKV), fw),
                pl.BlockSpec((nb, CHUNK, N_QKV), bw),
                pl.BlockSpec((nb, CHUNK, 4 * H_A), fw),
                pl.BlockSpec((nb, CHUNK, 4 * H_A), bw),
                pl.BlockSpec((nb, 1, 4 * H_A, CHUNK), lambda b, n: (b, n, 0, 0)),
                pl.BlockSpec((nb, 1, 4 * H_A, CHUNK), lambda b, n: (b, nc - 1 - n, 0, 0)),
                pl.BlockSpec((1, 2 * H_A), small),
                pl.BlockSpec((1, 2 * H_A), small),
                pl.BlockSpec((2 * H_A, 1), small),
                pl.BlockSpec((2 * H_A, 1), small)]
    args = [qkv, qkv, ba, ba, bat, bat, alog, dtb, alog_t, dtb_t]
    state_spec = pl.BlockSpec((nb, 2, H_A, DK, DV), lambda b, n: (b, 0, 0, 0, 0))
    if not zero_init:
        in_specs.append(state_spec)
        args.append(s0)
    out_specs = [pl.BlockSpec((nb, CHUNK, W_A), fw), pl.BlockSpec((nb, CHUNK, W_A), bw)]
    out_shape = [jax.ShapeDtypeStruct((batch, t, W_A), F32),
                 jax.ShapeDtypeStruct((batch, t, W_A), F32)]
    if emit_state:
        out_specs.append(state_spec)
        out_shape.append(jax.ShapeDtypeStruct((batch, 2, H_A, DK, DV), F32))
    return pl.pallas_call(
        functools.partial(_delta_kernel, zero_init=zero_init, emit_state=emit_state),
        grid=(batch // nb, nc),
        in_specs=in_specs,
        out_specs=out_specs,
        out_shape=out_shape,
        scratch_shapes=[pltpu.VMEM((nb, 2, H_A, 2 * DK, 2 * DV), F32)],
        compiler_params=pltpu.CompilerParams(dimension_semantics=("arbitrary", "arbitrary"),
                                             vmem_limit_bytes=VMEM_LIMIT),
        name="delta_scan",
    )(*args)


def _mix_kernel(x_ref, mod_ref, gain_ref, of_ref, ob_ref, wz_ref, wsc_ref, wg_ref, conv_ref,
                onorm_ref, hsum_ref, wa_ref, wb_ref, wo_ref, out_ref, *, row_len):
    x = x_ref[...]
    h = _rms(x, gain_ref[0:1, :]) * (1.0 + mod_ref[0, 1:2, :]) + mod_ref[0, 0:1, :]
    hb = h.astype(BF16)
    o = of_ref[...] + ob_ref[...]
    ms = _dot(o * o, hsum_ref[...], HIGHEST) * (1.0 / DV)
    z = _dot(hb, wz_ref[...])
    oa = o * lax.rsqrt(ms + EPS) * onorm_ref[...] * _silu(z)
    y_a = _dot(oa.astype(BF16), wa_ref[...])
    sc = _dot(hb, wsc_ref[...])
    bg, cg, xs = sc[:, 0:W_B], sc[:, W_B:2 * W_B], sc[:, 2 * W_B:3 * W_B]
    yb_in = bg * _conv3_rows(cg * xs, conv_ref, row_len)
    y_b = _dot(yb_in.astype(BF16), wb_ref[...])
    gates = _dot(hb, wg_ref[...])
    m = jax.nn.sigmoid(gates[:, 0:D_MODEL]) * y_a + jax.nn.sigmoid(gates[:, D_MODEL:]) * y_b
    y = _dot(m.astype(BF16), wo_ref[...])
    out_ref[...] = x + mod_ref[0, 2:3, :] * _rms(y, gain_ref[1:2, :])


def _mix(x, mod3, gains, o_f, o_b, w_z, w_sc, w_g, conv_sc, onorm_t, head_sum, w_a, w_b, w_o,
         *, mod_row_fn, row_len):
    n_tok = x.shape[0]
    tm = TOKEN_TILE
    const = lambda i: (0, 0)
    tok = lambda i: (i, 0)
    return pl.pallas_call(
        functools.partial(_mix_kernel, row_len=row_len),
        grid=(n_tok // tm,),
        in_specs=[pl.BlockSpec((tm, D_MODEL), tok),
                  pl.BlockSpec((1, 6, D_MODEL), lambda i: (mod_row_fn(i), 0, 0)),
                  pl.BlockSpec((4, D_MODEL), const),
                  pl.BlockSpec((tm, W_A), tok),
                  pl.BlockSpec((tm, W_A), tok),
                  pl.BlockSpec((D_MODEL, W_A), const),
                  pl.BlockSpec((D_MODEL, 3 * W_B), const),
                  pl.BlockSpec((D_MODEL, 2 * D_MODEL), const),
                  pl.BlockSpec((3, W_B), const),
                  pl.BlockSpec((1, W_A), const),
                  pl.BlockSpec((W_A, W_A), const),
                  pl.BlockSpec((W_A, D_MODEL), const),
                  pl.BlockSpec((W_B, D_MODEL), const),
                  pl.BlockSpec((D_MODEL, D_MODEL), const)],
        out_specs=pl.BlockSpec((tm, D_MODEL), tok),
        out_shape=jax.ShapeDtypeStruct((n_tok, D_MODEL), F32),
        compiler_params=pltpu.CompilerParams(dimension_semantics=("arbitrary",),
                                             vmem_limit_bytes=VMEM_LIMIT),
        name="mix",
    )(x, mod3, gains, o_f, o_b, w_z, w_sc, w_g, conv_sc, onorm_t, head_sum, w_a, w_b, w_o)


def _ffn_kernel(x_ref, mod_ref, gain_ref, win_ref, wout_ref, out_ref):
    x = x_ref[...]
    h = _rms(x, gain_ref[2:3, :]) * (1.0 + mod_ref[0, 4:5, :]) + mod_ref[0, 3:4, :]
    hb = h.astype(BF16)
    gate = _dot(hb, win_ref[:, 0:D_FF])
    up = _dot(hb, win_ref[:, D_FF:])
    y = _dot((_silu(gate) * up).astype(BF16), wout_ref[...])
    out_ref[...] = x + mod_ref[0, 5:6, :] * _rms(y, gain_ref[3:4, :])


def _ffn(x, mod3, gains, w_in, w_out, *, mod_row_fn):
    n_tok = x.shape[0]
    tm = TOKEN_TILE
    const = lambda i: (0, 0)
    tok = lambda i: (i, 0)
    return pl.pallas_call(
        _ffn_kernel,
        grid=(n_tok // tm,),
        in_specs=[pl.BlockSpec((tm, D_MODEL), tok),
                  pl.BlockSpec((1, 6, D_MODEL), lambda i: (mod_row_fn(i), 0, 0)),
                  pl.BlockSpec((4, D_MODEL), const),
                  pl.BlockSpec((D_MODEL, 2 * D_FF), const),
                  pl.BlockSpec((D_FF, D_MODEL), const)],
        out_specs=pl.BlockSpec((tm, D_MODEL), tok),
        out_shape=jax.ShapeDtypeStruct((n_tok, D_MODEL), F32),
        compiler_params=pltpu.CompilerParams(dimension_semantics=("arbitrary",),
                                             vmem_limit_bytes=VMEM_LIMIT),
        name="ffn",
    )(x, mod3, gains, w_in, w_out)


def _qkv_slab_order():
    cols = []
    for j in range(H_A // 2):
        cols += list(range((2 * j + 1) * DK, (2 * j + 2) * DK))
        cols += list(range(2 * j * DK, (2 * j + 1) * DK))
    for h in range(H_A):
        k_cols = list(range(W_A + h * DK, W_A + (h + 1) * DK))
        v_cols = list(range(2 * W_A + h * DV, 2 * W_A + (h + 1) * DV))
        cols += (k_cols + v_cols) if h % 2 else (v_cols + k_cols)
    return jnp.asarray(cols, dtype=jnp.int32)


def kernel(x_prompt, x_sample, state_delta, c, c_ctx, w_mod, b_mod, norm_gains, w_in, conv_qkv,
           a_log, dt_bias, onorm, conv_sc, w_branch_a, w_branch_b, w_out, w_ffn_in, w_ffn_out):
    depth = w_in.shape[0]
    assert depth == 1
    b_ctx, t_ctx, _ = x_prompt.shape
    b_lat, t_lat, _ = x_sample.shape
    assert t_ctx % TOKEN_TILE == 0 and t_lat % TOKEN_TILE == 0 and TOKEN_TILE % GRID_W == 0
    assert t_ctx & (t_ctx - 1) == 0 and TOKEN_TILE % t_ctx == 0
    assert b_lat + 1 <= MOD_ROWS
    layer = 0

    cvec = jnp.zeros((MOD_ROWS, D_MODEL), F32).at[0].set(c_ctx).at[1:1 + b_lat].set(c)
    mod3 = _modulation(cvec, w_mod[layer], b_mod[layer][None, :]).reshape(MOD_ROWS, 6, D_MODEL)

    o_qkv, o_z = N_QKV, N_QKV + W_A
    o_beta, o_sc = o_z + 4 * H_A, o_z + 4 * H_A + 3 * W_B
    wi = w_in[layer]
    slab_order = _qkv_slab_order()
    w_qkv = wi[:, :o_qkv][:, slab_order].astype(BF16)
    conv_perm = conv_qkv[layer][:, slab_order]
    w_z = wi[:, o_qkv:o_z].astype(BF16)
    w_ba = wi[:, o_z:o_beta].astype(BF16)
    w_bat = w_ba.T
    w_sc = wi[:, o_beta:o_sc].astype(BF16)
    w_g = wi[:, o_sc:].astype(BF16)
    w_a = w_branch_a[layer].astype(BF16)
    w_b = w_branch_b[layer].astype(BF16)
    w_o = w_out[layer].astype(BF16)
    w_f1 = w_ffn_in[layer].astype(BF16)
    w_f2 = w_ffn_out[layer].astype(BF16)
    gains = norm_gains[layer]
    onorm_t = jnp.tile(onorm[layer], H_A)[None, :]
    lane_head = jnp.arange(W_A) // DV
    head_sum = (lane_head[:, None] == lane_head[None, :]).astype(F32)

    def run_group(x3, s0, mod_row_fn, row_len, emit_state):
        batch, t, _ = x3.shape
        x = x3.reshape(batch * t, D_MODEL)
        qkv, ba, bat = _inproj(x, mod3, gains, w_qkv, w_ba, w_bat, conv_perm,
                               mod_row_fn=mod_row_fn, row_len=row_len)
        res = _delta_scan(qkv, ba, bat, a_log[layer], dt_bias[layer], s0,
                          batch=batch, emit_state=emit_state)
        o_f = res[0].reshape(batch * t, W_A)
        o_b = res[1].reshape(batch * t, W_A)
        x1 = _mix(x, mod3, gains, o_f, o_b, w_z, w_sc, w_g, conv_sc[layer], onorm_t, head_sum,
                  w_a, w_b, w_o, mod_row_fn=mod_row_fn, row_len=row_len)
        x2 = _ffn(x1, mod3, gains, w_f1, w_f2, mod_row_fn=mod_row_fn)
        return x2.reshape(batch, t, D_MODEL), (res[2] if emit_state else None)

    tiles_per_lat_seq = t_lat // TOKEN_TILE
    y_p, s_fin = run_group(x_prompt, None, lambda i: 0, t_ctx, True)
    y_s, _ = run_group(x_sample, state_delta[:, layer], lambda i: 1 + i // tiles_per_lat_seq,
                       GRID_W, False)
    new_state = s_fin[:, None].astype(x_prompt.dtype)
    return (y_p, y_s, new_state)
```

```python
import functools

import jax
import jax.numpy as jnp
from jax import lax
from jax.experimental import pallas as pl
from jax.experimental.pallas import tpu as pltpu

D_MODEL = 1024
H_A = 8
DK = 64
DV = 64
W_A = H_A * DV
W_B = 512
CHUNK = 64
GRID_W = 64
EPS = 1e-6
N_QKV = 3 * H_A * DK
D_FF = 2816
MOD_ROWS = 8
TOKEN_TILE = 256
WIDE_TILE = 512
DELTA_BATCH = 2
VMEM_LIMIT = 56 * 1024 * 1024

F32 = jnp.float32
BF16 = jnp.bfloat16
HIGHEST = lax.Precision.HIGHEST


def _dot(a, b, precision=None):
    return jnp.dot(a, b, preferred_element_type=F32, precision=precision)


def _dot_nt(a, b, precision=None):
    return lax.dot_general(a, b, (((1,), (1,)), ((), ())), preferred_element_type=F32,
                           precision=precision)


def _dot_tn(a, b, precision=None):
    return lax.dot_general(a, b, (((0,), (0,)), ((), ())), preferred_element_type=F32,
                           precision=precision)


def _resident(shape):
    return pl.BlockSpec(shape, lambda i: (0,) * len(shape), pipeline_mode=pl.Buffered(1))


def _silu(x):
    return x * jax.nn.sigmoid(x)


def _rms(x, gain):
    return x * lax.rsqrt(jnp.mean(x * x, axis=-1, keepdims=True) + EPS) * gain


def _conv3_rows(x, w_ref, row_len):
    n = x.shape[0]
    pos = lax.broadcasted_iota(jnp.int32, (n, 1), 0) & (row_len - 1)
    prev = jnp.where(pos == 0, 0.0, pltpu.roll(x, 1, 0))
    nxt = jnp.where(pos == row_len - 1, 0.0, pltpu.roll(x, n - 1, 0))
    return prev * w_ref[0:1, :] + x * w_ref[1:2, :] + nxt * w_ref[2:3, :]


def _mod_kernel(c_ref, w_ref, b_ref, o_ref):
    o_ref[...] = _dot(_silu(c_ref[...]), w_ref[...], HIGHEST) + b_ref[...]


def _modulation(cvec, w_mod, b_mod):
    n_out = w_mod.shape[1]
    blk = D_MODEL
    return pl.pallas_call(
        _mod_kernel,
        grid=(n_out // blk,),
        in_specs=[pl.BlockSpec((MOD_ROWS, D_MODEL), lambda j: (0, 0)),
                  pl.BlockSpec((D_MODEL, blk), lambda j: (0, j)),
                  pl.BlockSpec((1, blk), lambda j: (0, j))],
        out_specs=pl.BlockSpec((MOD_ROWS, blk), lambda j: (0, j)),
        out_shape=jax.ShapeDtypeStruct((MOD_ROWS, n_out), F32),
        name="modulation",
    )(cvec, w_mod, b_mod)


def _inproj_kernel(x_ref, mod_ref, gain_ref, wqkv_ref, wba_ref, wbat_ref, conv_ref,
                   qkv_ref, ba_ref, bat_ref, *, row_len):
    x = x_ref[...]
    h = _rms(x, gain_ref[0:1, :]) * (1.0 + mod_ref[0, 1:2, :]) + mod_ref[0, 0:1, :]
    hb = h.astype(BF16)
    qkv = _conv3_rows(_dot(hb, wqkv_ref[...]), conv_ref, row_len)
    qkv_ref[...] = _silu(qkv)
    ba_ref[...] = _dot(hb, wba_ref[...])
    for ci in range(x.shape[0] // CHUNK):
        bat_ref[ci] = _dot_nt(wbat_ref[...], hb[ci * CHUNK:(ci + 1) * CHUNK, :])


def _inproj(x, mod3, gains, w_qkv, w_ba, w_bat, conv_qkv, *, mod_row_fn, row_len):
    n_tok = x.shape[0]
    tm = TOKEN_TILE
    const = lambda i: (0, 0)
    return pl.pallas_call(
        functools.partial(_inproj_kernel, row_len=row_len),
        grid=(n_tok // tm,),
        in_specs=[pl.BlockSpec((tm, D_MODEL), lambda i: (i, 0)),
                  pl.BlockSpec((1, 6, D_MODEL), lambda i: (mod_row_fn(i, tm), 0, 0)),
                  pl.BlockSpec((4, D_MODEL), const),
                  pl.BlockSpec((D_MODEL, N_QKV), const),
                  pl.BlockSpec((D_MODEL, 4 * H_A), const),
                  pl.BlockSpec((4 * H_A, D_MODEL), const),
                  pl.BlockSpec((3, N_QKV), const)],
        out_specs=[pl.BlockSpec((tm, N_QKV), lambda i: (i, 0)),
                   pl.BlockSpec((tm, 4 * H_A), lambda i: (i, 0)),
                   pl.BlockSpec((tm // CHUNK, 4 * H_A, CHUNK), lambda i: (i, 0, 0))],
        out_shape=[jax.ShapeDtypeStruct((n_tok, N_QKV), F32),
                   jax.ShapeDtypeStruct((n_tok, 4 * H_A), F32),
                   jax.ShapeDtypeStruct((n_tok // CHUNK, 4 * H_A, CHUNK), F32)],
        compiler_params=pltpu.CompilerParams(dimension_semantics=("arbitrary",),
                                             vmem_limit_bytes=VMEM_LIMIT),
        name="inproj",
    )(x, mod3, gains, w_qkv, w_ba, w_bat, conv_qkv)


def _unit_tri_inverse_all(lows):
    c = CHUNK
    ri = lax.broadcasted_iota(jnp.int32, (c, c), 0)
    ci = lax.broadcasted_iota(jnp.int32, (c, c), 1)
    eye = (ri == ci).astype(F32)
    rng = range(len(lows))
    joins = lambda b: ((ri // (2 * b)) == (ci // (2 * b))) & ((ri // b) != (ci // b))
    t = [eye - jnp.where(joins(1), lows[i], 0.0) for i in rng]
    b = 2
    while b < c:
        mask = joins(b)
        t16 = [t[i].astype(BF16) for i in rng]
        x = [_dot(t16[i], jnp.where(mask, lows[i], 0.0).astype(BF16)) for i in rng]
        t = [t[i] - _dot(x[i].astype(BF16), t16[i]) for i in rng]
        b *= 2
    return t


def _delta_kernel(*refs, zero_init, emit_state):
    (qkv_f_ref, qkv_b_ref, ba_f_ref, ba_b_ref, bat_f_ref, bat_b_ref,
     alog_ref, dtb_ref, alog_t_ref, dtb_t_ref) = refs[:10]
    rest = list(refs[10:])
    s0_ref = None if zero_init else rest.pop(0)
    o_f_ref, o_b_ref = rest.pop(0), rest.pop(0)
    sfin_ref = rest.pop(0) if emit_state else None
    s_ref = rest.pop(0)

    n = pl.program_id(1)
    n_chunks = pl.num_programs(1)
    nb = s_ref.shape[0]
    state_at = lambda h: (0, DV) if h % 2 else (DK, 0)

    @pl.when(n == 0)
    def _():
        s_ref[...] = jnp.zeros(s_ref.shape, F32)
        if not zero_init:
            for bb in range(nb):
                for d in range(2):
                    for h in range(H_A):
                        r0, l0 = state_at(h)
                        s_ref[bb, d, h, r0:r0 + DK, l0:l0 + DV] = s0_ref[bb, d, h]

    c = CHUNK
    row = lax.broadcasted_iota(jnp.int32, (c, c), 0)
    col = lax.broadcasted_iota(jnp.int32, (c, c), 1)
    tril = (row >= col).astype(F32)
    triu = (row <= col).astype(F32)
    lane = lax.broadcasted_iota(jnp.int32, (c, 2 * DK), 1)

    inst, kq = [], []
    for bb in range(nb):
        for d, (qkv_ref, ba_ref, bat_ref, o_ref) in enumerate(
                ((qkv_f_ref, ba_f_ref, bat_f_ref, o_f_ref),
                 (qkv_b_ref, ba_b_ref, bat_b_ref, o_b_ref))):
            fwd = d == 0
            incl = (row >= col) if fwd else (row <= col)
            strict = (row > col) if fwd else (row < col)
            csum, csum_t = (tril, triu) if fwd else (triu, tril)
            last = c - 1 if fwd else 0

            ba = ba_ref[bb]
            bat = bat_ref[bb, 0]
            beta_all = jax.nn.sigmoid(ba[:, 0:2 * H_A])
            g_all = -jnp.exp(alog_ref[...]) * jax.nn.softplus(ba[:, 2 * H_A:] + dtb_ref[...])
            g_all_t = -jnp.exp(alog_t_ref[...]) * jax.nn.softplus(
                bat[2 * H_A:, :] + dtb_t_ref[...])
            gc_all = _dot(csum, g_all, HIGHEST)
            gc_all_t = _dot(g_all_t, csum_t, HIGHEST)
            glast_all = gc_all[last:last + 1, :]
            eg_all = jnp.exp(gc_all)
            tail_all = jnp.exp(glast_all - gc_all)
            eglast_all = jnp.exp(glast_all)

            for h in range(H_A):
                j = d * H_A + h
                kmask = (lane < DK) if h % 2 else (lane >= DK)
                kv = qkv_ref[bb, :, W_A + h * 2 * DK:W_A + (h + 1) * 2 * DK]
                qs = qkv_ref[bb, :, (h // 2) * 2 * DK:(h // 2 + 1) * 2 * DK]
                rk = lax.rsqrt(jnp.sum(jnp.where(kmask, kv * kv, 0.0), axis=-1, keepdims=True)
                               + EPS)
                rq = lax.rsqrt(jnp.sum(jnp.where(kmask, qs * qs, 0.0), axis=-1, keepdims=True)
                               + EPS) * (DK ** -0.5)
                beta = beta_all[:, j:j + 1]
                eg = eg_all[:, j:j + 1]
                kvn = kv * jnp.where(kmask, rk, 1.0)
                kvb = kvn * beta
                kn = jnp.where(kmask, kvn, 0.0)
                qm = jnp.where(kmask, qs * rq, 0.0)
                diff = gc_all[:, j:j + 1] - gc_all_t[j:j + 1, :]
                inst.append(dict(
                    bb=bb, d=d, h=h, o_ref=o_ref, strict=strict, vmask=~kmask,
                    decay=jnp.where(incl, jnp.exp(jnp.where(incl, diff, 0.0)), 0.0),
                    uw_rhs16=(kvb * jnp.where(kmask, eg, 1.0)).astype(BF16),
                    qh=qm * eg,
                    kt16=(kn * tail_all[:, j:j + 1]).astype(BF16),
                    eglast=eglast_all[:, j:j + 1]))
                kbq16 = jnp.concatenate([jnp.where(kmask, kvb, 0.0), qm], axis=0).astype(BF16)
                kq.append(_dot_nt(kbq16, kn.astype(BF16)))
    rng = range(len(inst))

    lows = [jnp.where(inst[i]["strict"], kq[i][:c] * inst[i]["decay"], 0.0) for i in rng]
    attn16 = [(kq[i][c:] * inst[i]["decay"]).astype(BF16) for i in rng]
    tmat16 = [t.astype(BF16) for t in _unit_tri_inverse_all(lows)]
    uw = [_dot(tmat16[i], inst[i]["uw_rhs16"]) for i in rng]

    s_old = [s_ref[inst[i]["bb"], inst[i]["d"], inst[i]["h"]] for i in rng]
    wq = [_dot(jnp.concatenate([uw[i], inst[i]["qh"]], axis=0).astype(BF16),
               s_old[i].astype(BF16)) for i in rng]
    v_new16 = [jnp.where(inst[i]["vmask"], uw[i] - wq[i][:c], 0.0).astype(BF16) for i in rng]
    o = [wq[i][c:] + _dot(attn16[i], v_new16[i]) for i in rng]
    s_new = [s_old[i] * inst[i]["eglast"] + _dot_tn(inst[i]["kt16"], v_new16[i]) for i in rng]
    for i in rng:
        it = inst[i]
        s_ref[it["bb"], it["d"], it["h"]] = s_new[i]
    for i in range(0, len(inst), 2):
        it = inst[i]
        it["o_ref"][it["bb"], :, it["h"] * DV:(it["h"] + 2) * DV] = o[i] + o[i + 1]

    if emit_state:
        @pl.when(n == n_chunks - 1)
        def _():
            for bb in range(nb):
                for d in range(2):
                    for h in range(H_A):
                        r0, l0 = state_at(h)
                        sfin_ref[bb, d, h] = s_ref[bb, d, h, r0:r0 + DK, l0:l0 + DV]


def _delta_scan(qkv, ba, bat, a_log, dt_bias, s0, *, batch, emit_state):
    n_tok = qkv.shape[0]
    t = n_tok // batch
    nc = t // CHUNK
    nb = DELTA_BATCH
    assert batch % nb == 0
    qkv = qkv.reshape(batch, t, N_QKV)
    ba = ba.reshape(batch, t, 4 * H_A)
    bat = bat.reshape(batch, nc, 4 * H_A, CHUNK)
    zero_init = s0 is None
    alog = a_log.reshape(1, 2 * H_A)
    dtb = dt_bias.reshape(1, 2 * H_A)
    alog_t = a_log.reshape(2 * H_A, 1)
    dtb_t = dt_bias.reshape(2 * H_A, 1)

    fw = lambda b, n: (b, n, 0)
    bw = lambda b, n: (b, nc - 1 - n, 0)
    small = lambda b, n: (0, 0)
    in_specs = [pl.BlockSpec((nb, CHUNK, N_QKV), fw),
                pl.BlockSpec((nb, CHUNK, N_QKV), bw),
                pl.BlockSpec((nb, CHUNK, 4 * H_A), fw),
                pl.BlockSpec((nb, CHUNK, 4 * H_A), bw),
                pl.BlockSpec((nb, 1, 4 * H_A, CHUNK), lambda b, n: (b, n, 0, 0)),
                pl.BlockSpec((nb, 1, 4 * H_A, CHUNK), lambda b, n: (b, nc - 1 - n, 0, 0)),
                pl.BlockSpec((1, 2 * H_A), small),
                pl.BlockSpec((1, 2 * H_A), small),
                pl.BlockSpec((2 * H_A, 1), small),
                pl.BlockSpec((2 * H_A, 1), small)]
    args = [qkv, qkv, ba, ba, bat, bat, alog, dtb, alog_t, dtb_t]
    state_spec = pl.BlockSpec((nb, 2, H_A, DK, DV), lambda b, n: (b, 0, 0, 0, 0))
    if not zero_init:
        in_specs.append(state_spec)
        args.append(s0)
    out_specs = [pl.BlockSpec((nb, CHUNK, W_A), fw), pl.BlockSpec((nb, CHUNK, W_A), bw)]
    out_shape = [jax.ShapeDtypeStruct((batch, t, W_A), F32),
                 jax.ShapeDtypeStruct((batch, t, W_A), F32)]
    if emit_state:
        out_specs.append(state_spec)
        out_shape.append(jax.ShapeDtypeStruct((batch, 2, H_A, DK, DV), F32))
    return pl.pallas_call(
        functools.partial(_delta_kernel, zero_init=zero_init, emit_state=emit_state),
        grid=(batch // nb, nc),
        in_specs=in_specs,
        out_specs=out_specs,
        out_shape=out_shape,
        scratch_shapes=[pltpu.VMEM((nb, 2, H_A, 2 * DK, 2 * DV), F32)],
        compiler_params=pltpu.CompilerParams(dimension_semantics=("arbitrary", "arbitrary"),
                                             vmem_limit_bytes=VMEM_LIMIT),
        name="delta_scan",
    )(*args)


def _mix_kernel(x_ref, mod_ref, gain_ref, of_ref, ob_ref, wz_ref, wsc_ref, wg_ref, conv_ref,
                onorm_ref, hsum_ref, wa_ref, wb_ref, wo_ref, out_ref, *, row_len):
    x = x_ref[...]
    h = _rms(x, gain_ref[0:1, :]) * (1.0 + mod_ref[0, 1:2, :]) + mod_ref[0, 0:1, :]
    hb = h.astype(BF16)
    o = of_ref[...] + ob_ref[...]
    rest = o * o
    ms = jnp.zeros_like(rest)
    for _ in range(3):
        piece = rest.astype(BF16)
        rest = rest - piece.astype(F32)
        ms = ms + _dot(piece, hsum_ref[...])
    ms = ms * (1.0 / DV)
    z = _dot(hb, wz_ref[...])
    oa = o * lax.rsqrt(ms + EPS) * onorm_ref[...] * _silu(z)
    y_a = _dot(oa.astype(BF16), wa_ref[...])
    sc = _dot(hb, wsc_ref[...])
    bg, cg, xs = sc[:, 0:W_B], sc[:, W_B:2 * W_B], sc[:, 2 * W_B:3 * W_B]
    yb_in = bg * _conv3_rows(cg * xs, conv_ref, row_len)
    y_b = _dot(yb_in.astype(BF16), wb_ref[...])
    gates = _dot(hb, wg_ref[...])
    m = jax.nn.sigmoid(gates[:, 0:D_MODEL]) * y_a + jax.nn.sigmoid(gates[:, D_MODEL:]) * y_b
    y = _dot(m.astype(BF16), wo_ref[...])
    out_ref[...] = x + mod_ref[0, 2:3, :] * _rms(y, gain_ref[1:2, :])


def _mix(x, mod3, gains, o_f, o_b, w_z, w_sc, w_g, conv_sc, onorm_t, head_sum, w_a, w_b, w_o,
         *, mod_row_fn, row_len):
    n_tok = x.shape[0]
    tm = WIDE_TILE
    const = lambda i: (0, 0)
    tok = lambda i: (i, 0)
    return pl.pallas_call(
        functools.partial(_mix_kernel, row_len=row_len),
        grid=(n_tok // tm,),
        in_specs=[pl.BlockSpec((tm, D_MODEL), tok),
                  pl.BlockSpec((1, 6, D_MODEL), lambda i: (mod_row_fn(i, tm), 0, 0)),
                  pl.BlockSpec((4, D_MODEL), const),
                  pl.BlockSpec((tm, W_A), tok),
                  pl.BlockSpec((tm, W_A), tok),
                  _resident((D_MODEL, W_A)),
                  _resident((D_MODEL, 3 * W_B)),
                  _resident((D_MODEL, 2 * D_MODEL)),
                  pl.BlockSpec((3, W_B), const),
                  pl.BlockSpec((1, W_A), const),
                  _resident((W_A, W_A)),
                  _resident((W_A, D_MODEL)),
                  _resident((W_B, D_MODEL)),
                  _resident((D_MODEL, D_MODEL))],
        out_specs=pl.BlockSpec((tm, D_MODEL), tok),
        out_shape=jax.ShapeDtypeStruct((n_tok, D_MODEL), F32),
        compiler_params=pltpu.CompilerParams(dimension_semantics=("arbitrary",),
                                             vmem_limit_bytes=VMEM_LIMIT),
        name="mix",
    )(x, mod3, gains, o_f, o_b, w_z, w_sc, w_g, conv_sc, onorm_t, head_sum, w_a, w_b, w_o)


def _ffn_kernel(x_ref, mod_ref, gain_ref, win_ref, wout_ref, out_ref):
    x = x_ref[...]
    h = _rms(x, gain_ref[2:3, :]) * (1.0 + mod_ref[0, 4:5, :]) + mod_ref[0, 3:4, :]
    hb = h.astype(BF16)
    gate = _dot(hb, win_ref[:, 0:D_FF])
    up = _dot(hb, win_ref[:, D_FF:])
    y = _dot((_silu(gate) * up).astype(BF16), wout_ref[...])
    out_ref[...] = x + mod_ref[0, 5:6, :] * _rms(y, gain_ref[3:4, :])


def _ffn(x, mod3, gains, w_in, w_out, *, mod_row_fn):
    n_tok = x.shape[0]
    tm = WIDE_TILE
    const = lambda i: (0, 0)
    tok = lambda i: (i, 0)
    return pl.pallas_call(
        _ffn_kernel,
        grid=(n_tok // tm,),
        in_specs=[pl.BlockSpec((tm, D_MODEL), tok),
                  pl.BlockSpec((1, 6, D_MODEL), lambda i: (mod_row_fn(i, tm), 0, 0)),
                  pl.BlockSpec((4, D_MODEL), const),
                  _resident((D_MODEL, 2 * D_FF)),
                  _resident((D_FF, D_MODEL))],
        out_specs=pl.BlockSpec((tm, D_MODEL), tok),
        out_shape=jax.ShapeDtypeStruct((n_tok, D_MODEL), F32),
        compiler_params=pltpu.CompilerParams(dimension_semantics=("arbitrary",),
                                             vmem_limit_bytes=VMEM_LIMIT),
        name="ffn",
    )(x, mod3, gains, w_in, w_out)


def _to_slab_order(w):
    rows = w.shape[0]
    pairs = lambda a: a.reshape(rows, H_A // 2, 2, DK)
    q, k, v = (pairs(w[:, i * W_A:(i + 1) * W_A]) for i in range(3))
    q_slabs = q[:, :, ::-1, :].reshape(rows, W_A)
    even = jnp.concatenate([v[:, :, 0], k[:, :, 0]], axis=-1)
    odd = jnp.concatenate([k[:, :, 1], v[:, :, 1]], axis=-1)
    kv_slabs = jnp.stack([even, odd], axis=2).reshape(rows, 2 * W_A)
    return jnp.concatenate([q_slabs, kv_slabs], axis=-1)


def kernel(x_prompt, x_sample, state_delta, c, c_ctx, w_mod, b_mod, norm_gains, w_in, conv_qkv,
           a_log, dt_bias, onorm, conv_sc, w_branch_a, w_branch_b, w_out, w_ffn_in, w_ffn_out):
    depth = w_in.shape[0]
    assert depth == 1
    b_ctx, t_ctx, _ = x_prompt.shape
    b_lat, t_lat, _ = x_sample.shape
    for tile in (TOKEN_TILE, WIDE_TILE):
        assert tile % t_ctx == 0 and t_lat % tile == 0 and tile % GRID_W == 0
    assert t_ctx & (t_ctx - 1) == 0
    assert b_lat + 1 <= MOD_ROWS
    layer = 0

    cvec = jnp.zeros((MOD_ROWS, D_MODEL), F32).at[0].set(c_ctx).at[1:1 + b_lat].set(c)
    mod3 = _modulation(cvec, w_mod[layer], b_mod[layer][None, :]).reshape(MOD_ROWS, 6, D_MODEL)

    o_qkv, o_z = N_QKV, N_QKV + W_A
    o_beta, o_sc = o_z + 4 * H_A, o_z + 4 * H_A + 3 * W_B
    wi = w_in[layer]
    w_qkv = _to_slab_order(wi[:, :o_qkv]).astype(BF16)
    conv_perm = _to_slab_order(conv_qkv[layer])
    w_z = wi[:, o_qkv:o_z].astype(BF16)
    w_ba = wi[:, o_z:o_beta].astype(BF16)
    w_bat = w_ba.T
    w_sc = wi[:, o_beta:o_sc].astype(BF16)
    w_g = wi[:, o_sc:].astype(BF16)
    w_a = w_branch_a[layer].astype(BF16)
    w_b = w_branch_b[layer].astype(BF16)
    w_o = w_out[layer].astype(BF16)
    w_f1 = w_ffn_in[layer].astype(BF16)
    w_f2 = w_ffn_out[layer].astype(BF16)
    gains = norm_gains[layer]
    onorm_t = jnp.tile(onorm[layer], H_A)[None, :]
    lane_head = jnp.arange(W_A) // DV
    head_sum = (lane_head[:, None] == lane_head[None, :]).astype(BF16)

    def run_group(x3, s0, mod_row_fn, row_len, emit_state):
        batch, t, _ = x3.shape
        x = x3.reshape(batch * t, D_MODEL)
        qkv, ba, bat = _inproj(x, mod3, gains, w_qkv, w_ba, w_bat, conv_perm,
                               mod_row_fn=mod_row_fn, row_len=row_len)
        res = _delta_scan(qkv, ba, bat, a_log[layer], dt_bias[layer], s0,
                          batch=batch, emit_state=emit_state)
        o_f = res[0].reshape(batch * t, W_A)
        o_b = res[1].reshape(batch * t, W_A)
        x1 = _mix(x, mod3, gains, o_f, o_b, w_z, w_sc, w_g, conv_sc[layer], onorm_t, head_sum,
                  w_a, w_b, w_o, mod_row_fn=mod_row_fn, row_len=row_len)
        x2 = _ffn(x1, mod3, gains, w_f1, w_f2, mod_row_fn=mod_row_fn)
        return x2.reshape(batch, t, D_MODEL), (res[2] if emit_state else None)

    y_p, s_fin = run_group(x_prompt, None, lambda i, tm: 0, t_ctx, True)
    y_s, _ = run_group(x_sample, state_delta[:, layer], lambda i, tm: 1 + (i * tm) // t_lat,
                       GRID_W, False)
    new_state = s_fin[:, None].astype(x_prompt.dtype)
    return (y_p, y_s, new_state)
```

```python
import functools

import jax
import jax.numpy as jnp
from jax import lax
from jax.experimental import pallas as pl
from jax.experimental.pallas import tpu as pltpu

D_MODEL = 1024
H_A = 8
DK = 64
DV = 64
W_A = H_A * DV
W_B = 512
CHUNK = 64
GRID_W = 64
EPS = 1e-6
N_QKV = 3 * H_A * DK
D_FF = 2816
MOD_ROWS = 8
TOKEN_TILE = 256
WIDE_TILE = 512
DELTA_BATCH = 2
VMEM_LIMIT = 56 * 1024 * 1024

F32 = jnp.float32
BF16 = jnp.bfloat16
HIGHEST = lax.Precision.HIGHEST


def _dot(a, b, precision=None):
    return jnp.dot(a, b, preferred_element_type=F32, precision=precision)


def _dot_nt(a, b, precision=None):
    return lax.dot_general(a, b, (((1,), (1,)), ((), ())), preferred_element_type=F32,
                           precision=precision)


def _dot_tn(a, b, precision=None):
    return lax.dot_general(a, b, (((0,), (0,)), ((), ())), preferred_element_type=F32,
                           precision=precision)


def _resident(shape):
    return pl.BlockSpec(shape, lambda i: (0,) * len(shape), pipeline_mode=pl.Buffered(1))


def _silu(x):
    return x * jax.nn.sigmoid(x)


def _rms(x, gain):
    return x * lax.rsqrt(jnp.mean(x * x, axis=-1, keepdims=True) + EPS) * gain


def _conv3_rows(x, w_ref, row_len):
    n = x.shape[0]
    pos = lax.broadcasted_iota(jnp.int32, (n, 1), 0) & (row_len - 1)
    prev = jnp.where(pos == 0, 0.0, pltpu.roll(x, 1, 0))
    nxt = jnp.where(pos == row_len - 1, 0.0, pltpu.roll(x, n - 1, 0))
    return prev * w_ref[0:1, :] + x * w_ref[1:2, :] + nxt * w_ref[2:3, :]


def _mod_kernel(c_ref, w_ref, b_ref, o_ref):
    o_ref[...] = _dot(_silu(c_ref[...]), w_ref[...], HIGHEST) + b_ref[...]


def _modulation(cvec, w_mod, b_mod):
    n_out = w_mod.shape[1]
    blk = D_MODEL
    return pl.pallas_call(
        _mod_kernel,
        grid=(n_out // blk,),
        in_specs=[pl.BlockSpec((MOD_ROWS, D_MODEL), lambda j: (0, 0)),
                  pl.BlockSpec((D_MODEL, blk), lambda j: (0, j)),
                  pl.BlockSpec((1, blk), lambda j: (0, j))],
        out_specs=pl.BlockSpec((MOD_ROWS, blk), lambda j: (0, j)),
        out_shape=jax.ShapeDtypeStruct((MOD_ROWS, n_out), F32),
        name="modulation",
    )(cvec, w_mod, b_mod)


def _inproj_kernel(x_ref, mod_ref, gain_ref, wqkv_ref, wba_ref, wbat_ref, conv_ref,
                   qkv_ref, ba_ref, bat_ref, *, row_len):
    x = x_ref[...]
    h = _rms(x, gain_ref[0:1, :]) * (1.0 + mod_ref[0, 1:2, :]) + mod_ref[0, 0:1, :]
    hb = h.astype(BF16)
    slab = 2 * DK
    lo = lax.broadcasted_iota(jnp.int32, (x.shape[0], slab), 1) < DK
    ssq = lambda v, m: jnp.sum(jnp.where(m, v * v, 0.0), axis=-1, keepdims=True)
    for pair in range(N_QKV // (2 * slab)):
        cols2 = slice(pair * 2 * slab, (pair + 1) * 2 * slab)
        v2 = _silu(_conv3_rows(_dot(hb, wqkv_ref[:, cols2]), conv_ref.at[:, cols2], row_len))
        for half in range(2):
            s = 2 * pair + half
            cols = slice(s * slab, (s + 1) * slab)
            v = v2[:, half * slab:(half + 1) * slab]
            if s < W_A // slab:
                r = jnp.where(lo, lax.rsqrt(ssq(v, lo) + EPS), lax.rsqrt(ssq(v, ~lo) + EPS))
                qkv_ref[:, cols] = v * (r * (DK ** -0.5))
            else:
                kmask = lo if (s - W_A // slab) % 2 else ~lo
                qkv_ref[:, cols] = v * jnp.where(kmask, lax.rsqrt(ssq(v, kmask) + EPS), 1.0)
    ba_ref[...] = _dot(hb, wba_ref[...])
    for ci in range(x.shape[0] // CHUNK):
        bat_ref[ci] = _dot_nt(wbat_ref[...], hb[ci * CHUNK:(ci + 1) * CHUNK, :])


def _inproj(x, mod3, gains, w_qkv, w_ba, w_bat, conv_qkv, *, mod_row_fn, row_len):
    n_tok = x.shape[0]
    tm = TOKEN_TILE
    const = lambda i: (0, 0)
    return pl.pallas_call(
        functools.partial(_inproj_kernel, row_len=row_len),
        grid=(n_tok // tm,),
        in_specs=[pl.BlockSpec((tm, D_MODEL), lambda i: (i, 0)),
                  pl.BlockSpec((1, 6, D_MODEL), lambda i: (mod_row_fn(i, tm), 0, 0)),
                  pl.BlockSpec((4, D_MODEL), const),
                  pl.BlockSpec((D_MODEL, N_QKV), const),
                  pl.BlockSpec((D_MODEL, 4 * H_A), const),
                  pl.BlockSpec((4 * H_A, D_MODEL), const),
                  pl.BlockSpec((3, N_QKV), const)],
        out_specs=[pl.BlockSpec((tm, N_QKV), lambda i: (i, 0)),
                   pl.BlockSpec((tm, 4 * H_A), lambda i: (i, 0)),
                   pl.BlockSpec((tm // CHUNK, 4 * H_A, CHUNK), lambda i: (i, 0, 0))],
        out_shape=[jax.ShapeDtypeStruct((n_tok, N_QKV), F32),
                   jax.ShapeDtypeStruct((n_tok, 4 * H_A), F32),
                   jax.ShapeDtypeStruct((n_tok // CHUNK, 4 * H_A, CHUNK), F32)],
        compiler_params=pltpu.CompilerParams(dimension_semantics=("arbitrary",),
                                             vmem_limit_bytes=VMEM_LIMIT),
        name="inproj",
    )(x, mod3, gains, w_qkv, w_ba, w_bat, conv_qkv)


def _unit_tri_inverse_stages(lows):
    c = CHUNK
    ri = lax.broadcasted_iota(jnp.int32, (c, c), 0)
    ci = lax.broadcasted_iota(jnp.int32, (c, c), 1)
    eye = (ri == ci).astype(F32)
    rng = range(len(lows))
    joins = lambda b: ((ri // (2 * b)) == (ci // (2 * b))) & ((ri // b) != (ci // b))
    t = [eye - jnp.where(joins(1), lows[i], 0.0) for i in rng]
    b = 2
    while b < c:
        mask = joins(b)
        t16 = [t[i].astype(BF16) for i in rng]
        x = [_dot(t16[i], jnp.where(mask, lows[i], 0.0).astype(BF16)) for i in rng]
        yield
        t = [t[i] - _dot(x[i].astype(BF16), t16[i]) for i in rng]
        yield
        b *= 2
    return t


def _split3(x):
    pieces = []
    for _ in range(3):
        p = x.astype(BF16)
        x = x - p.astype(F32)
        pieces.append(p)
    return pieces


def _group_stages(qkv_ref, o_ref, s_ref, bb, d, gates):
    c = CHUNK
    fwd = d == 0
    row = lax.broadcasted_iota(jnp.int32, (c, c), 0)
    col = lax.broadcasted_iota(jnp.int32, (c, c), 1)
    incl = (row >= col) if fwd else (row <= col)
    strict = (row > col) if fwd else (row < col)
    lane = lax.broadcasted_iota(jnp.int32, (c, 2 * DK), 1)
    beta_all, gc_all, gc_all_t, eg_all, tail_all, eglast_all = gates
    heads = range(H_A)

    kmask, decay, uw_rhs16, qh, kt16, kq = [], [], [], [], [], []
    for h in heads:
        j = d * H_A + h
        km = (lane < DK) if h % 2 else (lane >= DK)
        kv = qkv_ref[bb, :, W_A + h * 2 * DK:W_A + (h + 1) * 2 * DK]
        qs = qkv_ref[bb, :, (h // 2) * 2 * DK:(h // 2 + 1) * 2 * DK]
        beta = jnp.broadcast_to(beta_all[:, j:j + 1], kv.shape)
        eg = jnp.broadcast_to(eg_all[:, j:j + 1], kv.shape)
        kvb = kv * beta
        kn = jnp.where(km, kv, 0.0)
        qm = jnp.where(km, qs, 0.0)
        diff = gc_all[:, j:j + 1] - gc_all_t[j:j + 1, :]
        kmask.append(km)
        decay.append(jnp.where(incl, jnp.exp(jnp.where(incl, diff, 0.0)), 0.0))
        uw_rhs16.append(jnp.where(km, kvb * eg, kvb).astype(BF16))
        qh.append(qm * eg)
        kt16.append((kn * tail_all[:, j:j + 1]).astype(BF16))
        kbq16 = jnp.concatenate([jnp.where(km, kvb, 0.0), qm], axis=0).astype(BF16)
        kq.append(_dot_nt(kbq16, kn.astype(BF16)))
    yield

    lows = [jnp.where(strict, kq[h][:c] * decay[h], 0.0) for h in heads]
    attn16 = [(kq[h][c:] * decay[h]).astype(BF16) for h in heads]
    tmat = yield from _unit_tri_inverse_stages(lows)
    uw = [_dot(tmat[h].astype(BF16), uw_rhs16[h]) for h in heads]
    yield

    s_old = [s_ref[bb, d, h] for h in heads]
    wq = [_dot(jnp.concatenate([uw[h], qh[h]], axis=0).astype(BF16), s_old[h].astype(BF16))
          for h in heads]
    yield

    v_new16 = [jnp.where(kmask[h], 0.0, uw[h] - wq[h][:c]).astype(BF16) for h in heads]
    o = [wq[h][c:] + _dot(attn16[h], v_new16[h]) for h in heads]
    for h in heads:
        j = d * H_A + h
        s_ref[bb, d, h] = s_old[h] * eglast_all[:, j:j + 1] + _dot_tn(kt16[h], v_new16[h])
    for h in range(0, H_A, 2):
        o_ref[bb, :, h * DV:(h + 2) * DV] = o[h] + o[h + 1]


def _delta_kernel(*refs, zero_init, emit_state):
    (qkv_f_ref, qkv_b_ref, ba_f_ref, ba_b_ref, bat_f_ref, bat_b_ref,
     alog_ref, dtb_ref, alog_t_ref, dtb_t_ref) = refs[:10]
    rest = list(refs[10:])
    s0_ref = None if zero_init else rest.pop(0)
    o_f_ref, o_b_ref = rest.pop(0), rest.pop(0)
    sfin_ref = rest.pop(0) if emit_state else None
    s_ref = rest.pop(0)

    n = pl.program_id(1)
    n_chunks = pl.num_programs(1)
    nb = s_ref.shape[0]
    state_at = lambda h: (0, DV) if h % 2 else (DK, 0)

    @pl.when(n == 0)
    def _():
        s_ref[...] = jnp.zeros(s_ref.shape, F32)
        if not zero_init:
            for bb in range(nb):
                for d in range(2):
                    for h in range(H_A):
                        r0, l0 = state_at(h)
                        s_ref[bb, d, h, r0:r0 + DK, l0:l0 + DV] = s0_ref[bb, d, h]

    c = CHUNK
    row = lax.broadcasted_iota(jnp.int32, (c, c), 0)
    col = lax.broadcasted_iota(jnp.int32, (c, c), 1)
    tril16 = (row >= col).astype(BF16)
    triu16 = (row <= col).astype(BF16)

    programs = []
    for bb in range(nb):
        for d, (qkv_ref, ba_ref, bat_ref, o_ref) in enumerate(
                ((qkv_f_ref, ba_f_ref, bat_f_ref, o_f_ref),
                 (qkv_b_ref, ba_b_ref, bat_b_ref, o_b_ref))):
            fwd = d == 0
            csum16, csum_t16 = (tril16, triu16) if fwd else (triu16, tril16)
            last = c - 1 if fwd else 0
            ba = ba_ref[bb]
            bat = bat_ref[bb, 0]
            beta_all = jax.nn.sigmoid(ba[:, 0:2 * H_A])
            g_all = -jnp.exp(alog_ref[...]) * jax.nn.softplus(ba[:, 2 * H_A:] + dtb_ref[...])
            g_all_t = -jnp.exp(alog_t_ref[...]) * jax.nn.softplus(
                bat[2 * H_A:, :] + dtb_t_ref[...])
            gc_all = sum(_dot(csum16, p) for p in _split3(g_all))
            gc_all_t = sum(_dot(p, csum_t16) for p in _split3(g_all_t))
            glast_all = gc_all[last:last + 1, :]
            gates = (beta_all, gc_all, gc_all_t, jnp.exp(gc_all), jnp.exp(glast_all - gc_all),
                     jnp.exp(glast_all))
            programs.append(_group_stages(qkv_ref, o_ref, s_ref, bb, d, gates))

    while programs:
        for prog in list(programs):
            try:
                next(prog)
            except StopIteration:
                programs.remove(prog)

    if emit_state:
        @pl.when(n == n_chunks - 1)
        def _():
            for bb in range(nb):
                for d in range(2):
                    for h in range(H_A):
                        r0, l0 = state_at(h)
                        sfin_ref[bb, d, h] = s_ref[bb, d, h, r0:r0 + DK, l0:l0 + DV]


def _delta_scan(qkv, ba, bat, a_log, dt_bias, s0, *, batch, emit_state):
    n_tok = qkv.shape[0]
    t = n_tok // batch
    nc = t // CHUNK
    nb = DELTA_BATCH
    assert batch % nb == 0
    qkv = qkv.reshape(batch, t, N_QKV)
    ba = ba.reshape(batch, t, 4 * H_A)
    bat = bat.reshape(batch, nc, 4 * H_A, CHUNK)
    zero_init = s0 is None
    alog = a_log.reshape(1, 2 * H_A)
    dtb = dt_bias.reshape(1, 2 * H_A)
    alog_t = a_log.reshape(2 * H_A, 1)
    dtb_t = dt_bias.reshape(2 * H_A, 1)

    fw = lambda b, n: (b, n, 0)
    bw = lambda b, n: (b, nc - 1 - n, 0)
    small = lambda b, n: (0, 0)
    in_specs = [pl.BlockSpec((nb, CHUNK, N_QKV), fw),
                pl.BlockSpec((nb, CHUNK, N_QKV), bw),
                pl.BlockSpec((nb, CHUNK, 4 * H_A), fw),
                pl.BlockSpec((nb, CHUNK, 4 * H_A), bw),
                pl.BlockSpec((nb, 1, 4 * H_A, CHUNK), lambda b, n: (b, n, 0, 0)),
                pl.BlockSpec((nb, 1, 4 * H_A, CHUNK), lambda b, n: (b, nc - 1 - n, 0, 0)),
                pl.BlockSpec((1, 2 * H_A), small),
                pl.BlockSpec((1, 2 * H_A), small),
                pl.BlockSpec((2 * H_A, 1), small),
                pl.BlockSpec((2 * H_A, 1), small)]
    args = [qkv, qkv, ba, ba, bat, bat, alog, dtb, alog_t, dtb_t]
    state_spec = pl.BlockSpec((nb, 2, H_A, DK, DV), lambda b, n: (b, 0, 0, 0, 0))
    if not zero_init:
        in_specs.append(state_spec)
        args.append(s0)
    out_specs = [pl.BlockSpec((nb, CHUNK, W_A), fw), pl.BlockSpec((nb, CHUNK, W_A), bw)]
    out_shape = [jax.ShapeDtypeStruct((batch, t, W_A), F32),
                 jax.ShapeDtypeStruct((batch, t, W_A), F32)]
    if emit_state:
        out_specs.append(state_spec)
        out_shape.append(jax.ShapeDtypeStruct((batch, 2, H_A, DK, DV), F32))
    return pl.pallas_call(
        functools.partial(_delta_kernel, zero_init=zero_init, emit_state=emit_state),
        grid=(batch // nb, nc),
        in_specs=in_specs,
        out_specs=out_specs,
        out_shape=out_shape,
        scratch_shapes=[pltpu.VMEM((nb, 2, H_A, 2 * DK, 2 * DV), F32)],
        compiler_params=pltpu.CompilerParams(dimension_semantics=("arbitrary", "arbitrary"),
                                             vmem_limit_bytes=VMEM_LIMIT),
        name="delta_scan",
    )(*args)


def _mix_kernel(x_ref, mod_ref, gain_ref, of_ref, ob_ref, wz_ref, wsc_ref, wg_ref, conv_ref,
                onorm_ref, hsum_ref, wa_ref, wb_ref, wo_ref, out_ref, *, row_len):
    x = x_ref[...]
    h = _rms(x, gain_ref[0:1, :]) * (1.0 + mod_ref[0, 1:2, :]) + mod_ref[0, 0:1, :]
    hb = h.astype(BF16)
    o = of_ref[...] + ob_ref[...]
    rest = o * o
    ms = jnp.zeros_like(rest)
    for _ in range(3):
        piece = rest.astype(BF16)
        rest = rest - piece.astype(F32)
        ms = ms + _dot(piece, hsum_ref[...])
    ms = ms * (1.0 / DV)
    z = _dot(hb, wz_ref[...])
    oa = o * lax.rsqrt(ms + EPS) * onorm_ref[...] * _silu(z)
    y_a = _dot(oa.astype(BF16), wa_ref[...])
    sc = _dot(hb, wsc_ref[...])
    bg, cg, xs = sc[:, 0:W_B], sc[:, W_B:2 * W_B], sc[:, 2 * W_B:3 * W_B]
    yb_in = bg * _conv3_rows(cg * xs, conv_ref, row_len)
    y_b = _dot(yb_in.astype(BF16), wb_ref[...])
    gates = _dot(hb, wg_ref[...])
    m = jax.nn.sigmoid(gates[:, 0:D_MODEL]) * y_a + jax.nn.sigmoid(gates[:, D_MODEL:]) * y_b
    y = _dot(m.astype(BF16), wo_ref[...])
    out_ref[...] = x + mod_ref[0, 2:3, :] * _rms(y, gain_ref[1:2, :])


def _mix(x, mod3, gains, o_f, o_b, w_z, w_sc, w_g, conv_sc, onorm_t, head_sum, w_a, w_b, w_o,
         *, mod_row_fn, row_len):
    n_tok = x.shape[0]
    tm = WIDE_TILE
    const = lambda i: (0, 0)
    tok = lambda i: (i, 0)
    return pl.pallas_call(
        functools.partial(_mix_kernel, row_len=row_len),
        grid=(n_tok // tm,),
        in_specs=[pl.BlockSpec((tm, D_MODEL), tok),
                  pl.BlockSpec((1, 6, D_MODEL), lambda i: (mod_row_fn(i, tm), 0, 0)),
                  pl.BlockSpec((4, D_MODEL), const),
                  pl.BlockSpec((tm, W_A), tok),
                  pl.BlockSpec((tm, W_A), tok),
                  _resident((D_MODEL, W_A)),
                  _resident((D_MODEL, 3 * W_B)),
                  _resident((D_MODEL, 2 * D_MODEL)),
                  pl.BlockSpec((3, W_B), const),
                  pl.BlockSpec((1, W_A), const),
                  _resident((W_A, W_A)),
                  _resident((W_A, D_MODEL)),
                  _resident((W_B, D_MODEL)),
                  _resident((D_MODEL, D_MODEL))],
        out_specs=pl.BlockSpec((tm, D_MODEL), tok),
        out_shape=jax.ShapeDtypeStruct((n_tok, D_MODEL), F32),
        compiler_params=pltpu.CompilerParams(dimension_semantics=("arbitrary",),
                                             vmem_limit_bytes=VMEM_LIMIT),
        name="mix",
    )(x, mod3, gains, o_f, o_b, w_z, w_sc, w_g, conv_sc, onorm_t, head_sum, w_a, w_b, w_o)


def _ffn_kernel(x_ref, mod_ref, gain_ref, win_ref, wout_ref, out_ref):
    x = x_ref[...]
    h = _rms(x, gain_ref[2:3, :]) * (1.0 + mod_ref[0, 4:5, :]) + mod_ref[0, 3:4, :]
    hb = h.astype(BF16)
    gate = _dot(hb, win_ref[:, 0:D_FF])
    up = _dot(hb, win_ref[:, D_FF:])
    y = _dot((_silu(gate) * up).astype(BF16), wout_ref[...])
    out_ref[...] = x + mod_ref[0, 5:6, :] * _rms(y, gain_ref[3:4, :])


def _ffn(x, mod3, gains, w_in, w_out, *, mod_row_fn):
    n_tok = x.shape[0]
    tm = WIDE_TILE
    const = lambda i: (0, 0)
    tok = lambda i: (i, 0)
    return pl.pallas_call(
        _ffn_kernel,
        grid=(n_tok // tm,),
        in_specs=[pl.BlockSpec((tm, D_MODEL), tok),
                  pl.BlockSpec((1, 6, D_MODEL), lambda i: (mod_row_fn(i, tm), 0, 0)),
                  pl.BlockSpec((4, D_MODEL), const),
                  _resident((D_MODEL, 2 * D_FF)),
                  _resident((D_FF, D_MODEL))],
        out_specs=pl.BlockSpec((tm, D_MODEL), tok),
        out_shape=jax.ShapeDtypeStruct((n_tok, D_MODEL), F32),
        compiler_params=pltpu.CompilerParams(dimension_semantics=("arbitrary",),
                                             vmem_limit_bytes=VMEM_LIMIT),
        name="ffn",
    )(x, mod3, gains, w_in, w_out)


def _to_slab_order(w):
    rows = w.shape[0]
    pairs = lambda a: a.reshape(rows, H_A // 2, 2, DK)
    q, k, v = (pairs(w[:, i * W_A:(i + 1) * W_A]) for i in range(3))
    q_slabs = q[:, :, ::-1, :].reshape(rows, W_A)
    even = jnp.concatenate([v[:, :, 0], k[:, :, 0]], axis=-1)
    odd = jnp.concatenate([k[:, :, 1], v[:, :, 1]], axis=-1)
    kv_slabs = jnp.stack([even, odd], axis=2).reshape(rows, 2 * W_A)
    return jnp.concatenate([q_slabs, kv_slabs], axis=-1)


def kernel(x_prompt, x_sample, state_delta, c, c_ctx, w_mod, b_mod, norm_gains, w_in, conv_qkv,
           a_log, dt_bias, onorm, conv_sc, w_branch_a, w_branch_b, w_out, w_ffn_in, w_ffn_out):
    depth = w_in.shape[0]
    assert depth == 1
    b_ctx, t_ctx, _ = x_prompt.shape
    b_lat, t_lat, _ = x_sample.shape
    for tile in (TOKEN_TILE, WIDE_TILE):
        assert tile % t_ctx == 0 and t_lat % tile == 0 and tile % GRID_W == 0
    assert t_ctx & (t_ctx - 1) == 0
    assert b_lat + 1 <= MOD_ROWS
    layer = 0

    cvec = jnp.zeros((MOD_ROWS, D_MODEL), F32).at[0].set(c_ctx).at[1:1 + b_lat].set(c)
    mod3 = _modulation(cvec, w_mod[layer], b_mod[layer][None, :]).reshape(MOD_ROWS, 6, D_MODEL)

    o_qkv, o_z = N_QKV, N_QKV + W_A
    o_beta, o_sc = o_z + 4 * H_A, o_z + 4 * H_A + 3 * W_B
    wi = w_in[layer]
    w_qkv = _to_slab_order(wi[:, :o_qkv]).astype(BF16)
    conv_perm = _to_slab_order(conv_qkv[layer])
    w_z = wi[:, o_qkv:o_z].astype(BF16)
    w_ba = wi[:, o_z:o_beta].astype(BF16)
    w_bat = w_ba.T
    w_sc = wi[:, o_beta:o_sc].astype(BF16)
    w_g = wi[:, o_sc:].astype(BF16)
    w_a = w_branch_a[layer].astype(BF16)
    w_b = w_branch_b[layer].astype(BF16)
    w_o = w_out[layer].astype(BF16)
    w_f1 = w_ffn_in[layer].astype(BF16)
    w_f2 = w_ffn_out[layer].astype(BF16)
    gains = norm_gains[layer]
    onorm_t = jnp.tile(onorm[layer], H_A)[None, :]
    lane_head = jnp.arange(W_A) // DV
    head_sum = (lane_head[:, None] == lane_head[None, :]).astype(BF16)

    def run_group(x3, s0, mod_row_fn, row_len, emit_state):
        batch, t, _ = x3.shape
        x = x3.reshape(batch * t, D_MODEL)
        qkv, ba, bat = _inproj(x, mod3, gains, w_qkv, w_ba, w_bat, conv_perm,
                               mod_row_fn=mod_row_fn, row_len=row_len)
        res = _delta_scan(qkv, ba, bat, a_log[layer], dt_bias[layer], s0,
                          batch=batch, emit_state=emit_state)
        o_f = res[0].reshape(batch * t, W_A)
        o_b = res[1].reshape(batch * t, W_A)
        x1 = _mix(x, mod3, gains, o_f, o_b, w_z, w_sc, w_g, conv_sc[layer], onorm_t, head_sum,
                  w_a, w_b, w_o, mod_row_fn=mod_row_fn, row_len=row_len)
        x2 = _ffn(x1, mod3, gains, w_f1, w_f2, mod_row_fn=mod_row_fn)
        return x2.reshape(batch, t, D_MODEL), (res[2] if emit_state else None)

    y_p, s_fin = run_group(x_prompt, None, lambda i, tm: 0, t_ctx, True)
    y_s, _ = run_group(x_sample, state_delta[:, layer], lambda i, tm: 1 + (i * tm) // t_lat,
                       GRID_W, False)
    new_state = s_fin[:, None].astype(x_prompt.dtype)
    return (y_p, y_s, new_state)
```

```python
import functools

import jax
import jax.numpy as jnp
from jax import lax
from jax.experimental import pallas as pl
from jax.experimental.pallas import tpu as pltpu

D_MODEL = 1024
H_A = 8
DK = 64
DV = 64
W_A = H_A * DV
W_B = 512
CHUNK = 64
GRID_W = 64
EPS = 1e-6
N_QKV = 3 * H_A * DK
D_FF = 2816
SUBLANES = 8
MOD_ROWS = SUBLANES
TOKEN_TILE = 256
WIDE_TILE = 512
DELTA_BATCH = 2
VMEM_LIMIT = 56 * 1024 * 1024

F32 = jnp.float32
BF16 = jnp.bfloat16
HIGHEST = lax.Precision.HIGHEST


def _dot(a, b, precision=None):
    return jnp.dot(a, b, preferred_element_type=F32, precision=precision)


def _dot_nt(a, b, precision=None):
    return lax.dot_general(a, b, (((1,), (1,)), ((), ())), preferred_element_type=F32,
                           precision=precision)


def _dot_tn(a, b, precision=None):
    return lax.dot_general(a, b, (((0,), (0,)), ((), ())), preferred_element_type=F32,
                           precision=precision)


def _resident(shape):
    return pl.BlockSpec(shape, lambda i: (0,) * len(shape), pipeline_mode=pl.Buffered(1))


def _silu(x):
    return x * jax.nn.sigmoid(x)


def _rms(x, gain):
    return x * lax.rsqrt(jnp.mean(x * x, axis=-1, keepdims=True) + EPS) * gain


def _conv3_rows(x, w_ref, row_len):
    n = x.shape[0]
    pos = lax.broadcasted_iota(jnp.int32, (n, 1), 0) & (row_len - 1)
    prev = jnp.where(pos == 0, 0.0, pltpu.roll(x, 1, 0))
    nxt = jnp.where(pos == row_len - 1, 0.0, pltpu.roll(x, n - 1, 0))
    return prev * w_ref[0:1, :] + x * w_ref[1:2, :] + nxt * w_ref[2:3, :]


def _mod_kernel(c_ref, w_ref, b_ref, o_ref):
    rows = c_ref.shape[0]
    s3 = jnp.concatenate([p.astype(F32) for p in _split3(_silu(c_ref[...]))], axis=0).astype(BF16)
    w = w_ref[...]
    w_hi = w.astype(BF16)
    w_lo = (w - w_hi.astype(F32)).astype(BF16)
    acc = _dot(s3, w_hi) + _dot(s3, w_lo)
    o_ref[...] = acc[0:rows] + acc[rows:2 * rows] + acc[2 * rows:] + b_ref[...]


def _modulation(cvec, w_mod, b_mod):
    n_out = w_mod.shape[1]
    blk = D_MODEL
    return pl.pallas_call(
        _mod_kernel,
        grid=(n_out // blk,),
        in_specs=[pl.BlockSpec((MOD_ROWS, D_MODEL), lambda j: (0, 0)),
                  pl.BlockSpec((D_MODEL, blk), lambda j: (0, j)),
                  pl.BlockSpec((1, blk), lambda j: (0, j))],
        out_specs=pl.BlockSpec((MOD_ROWS, blk), lambda j: (0, j)),
        out_shape=jax.ShapeDtypeStruct((MOD_ROWS, n_out), F32),
        name="modulation",
    )(cvec, w_mod, b_mod)


def _inproj_kernel(x_ref, mod_ref, gain_ref, wqkv_ref, wba_ref, wbat_ref, conv_ref,
                   qkv_ref, ba_ref, bat_ref, *, row_len):
    x = x_ref[...]
    h = _rms(x, gain_ref[0:1, :]) * (1.0 + mod_ref[0, 1:2, :]) + mod_ref[0, 0:1, :]
    hb = h.astype(BF16)
    slab = 2 * DK
    lo = lax.broadcasted_iota(jnp.int32, (x.shape[0], slab), 1) < DK
    ssq = lambda v, m: jnp.sum(jnp.where(m, v * v, 0.0), axis=-1, keepdims=True)
    for pair in range(N_QKV // (2 * slab)):
        cols2 = slice(pair * 2 * slab, (pair + 1) * 2 * slab)
        v2 = _silu(_conv3_rows(_dot(hb, wqkv_ref[:, cols2]), conv_ref.at[:, cols2], row_len))
        for half in range(2):
            s = 2 * pair + half
            cols = slice(s * slab, (s + 1) * slab)
            v = v2[:, half * slab:(half + 1) * slab]
            if s < W_A // slab:
                r = jnp.where(lo, lax.rsqrt(ssq(v, lo) + EPS), lax.rsqrt(ssq(v, ~lo) + EPS))
                qkv_ref[:, cols] = v * (r * (DK ** -0.5))
            else:
                kmask = lo if (s - W_A // slab) % 2 else ~lo
                qkv_ref[:, cols] = v * jnp.where(kmask, lax.rsqrt(ssq(v, kmask) + EPS), 1.0)
    ba_ref[...] = _dot(hb, wba_ref[...])
    for ci in range(x.shape[0] // CHUNK):
        bat_ref[ci] = _dot_nt(wbat_ref[...], hb[ci * CHUNK:(ci + 1) * CHUNK, :])


def _inproj(x, mod3, gains, w_qkv, w_ba, w_bat, conv_qkv, *, mod_row_fn, row_len):
    n_tok = x.shape[0]
    tm = TOKEN_TILE
    const = lambda i: (0, 0)
    return pl.pallas_call(
        functools.partial(_inproj_kernel, row_len=row_len),
        grid=(n_tok // tm,),
        in_specs=[pl.BlockSpec((tm, D_MODEL), lambda i: (i, 0)),
                  pl.BlockSpec((1, 6, D_MODEL), lambda i: (mod_row_fn(i, tm), 0, 0)),
                  pl.BlockSpec((4, D_MODEL), const),
                  pl.BlockSpec((D_MODEL, N_QKV), const),
                  pl.BlockSpec((D_MODEL, 4 * H_A), const),
                  pl.BlockSpec((4 * H_A, D_MODEL), const),
                  pl.BlockSpec((3, N_QKV), const)],
        out_specs=[pl.BlockSpec((tm, N_QKV), lambda i: (i, 0)),
                   pl.BlockSpec((tm, 4 * H_A), lambda i: (i, 0)),
                   pl.BlockSpec((tm // CHUNK, 4 * H_A, CHUNK), lambda i: (i, 0, 0))],
        out_shape=[jax.ShapeDtypeStruct((n_tok, N_QKV), F32),
                   jax.ShapeDtypeStruct((n_tok, 4 * H_A), F32),
                   jax.ShapeDtypeStruct((n_tok // CHUNK, 4 * H_A, CHUNK), F32)],
        compiler_params=pltpu.CompilerParams(dimension_semantics=("arbitrary",),
                                             vmem_limit_bytes=VMEM_LIMIT),
        name="inproj",
    )(x, mod3, gains, w_qkv, w_ba, w_bat, conv_qkv)


def _unit_tri_inverse_stages(lows, lower):
    c = CHUNK
    ri = lax.broadcasted_iota(jnp.int32, (c, c), 0)
    ci = lax.broadcasted_iota(jnp.int32, (c, c), 1)
    eye = (ri == ci).astype(F32)
    rng = range(len(lows))
    joins = lambda b: ((ri // (2 * b)) == (ci // (2 * b))) & ((ri // b) != (ci // b))
    t = [eye - jnp.where(joins(1), lows[i], 0.0) for i in rng]
    b = 2
    while b < c:
        mask = joins(b)
        t16 = [t[i].astype(BF16) for i in rng]
        e16 = [jnp.where(mask, lows[i], 0.0).astype(BF16) for i in rng]
        if b % SUBLANES:
            x = [_dot(t16[i], e16[i]) for i in rng]
            yield
            t = [t[i] - _dot(x[i].astype(BF16), t16[i]) for i in rng]
            yield
        else:
            first = [slice(s, s + b) for s in range(0, c, 2 * b)]
            second = [slice(s + b, s + 2 * b) for s in range(0, c, 2 * b)]
            keep, upd = (first, second) if lower else (second, first)
            t_upd = [jnp.concatenate([t[i][r] for r in upd], axis=0) for i in rng]
            x = [_dot(t_upd[i].astype(BF16), e16[i]) for i in rng]
            yield
            t_upd = [t_upd[i] - _dot(x[i].astype(BF16), t16[i]) for i in rng]
            yield
            pair = lambda kept, new: (kept, new) if lower else (new, kept)
            t = [jnp.concatenate(
                [piece for k, r in enumerate(keep)
                 for piece in pair(t[i][r], t_upd[i][k * b:(k + 1) * b])], axis=0) for i in rng]
        b *= 2
    return t


def _split3(x):
    pieces = []
    for _ in range(3):
        p = x.astype(BF16)
        x = x - p.astype(F32)
        pieces.append(p)
    return pieces


def _group_stages(qkv_ref, o_ref, s_ref, bb, d, gates):
    c = CHUNK
    fwd = d == 0
    row = lax.broadcasted_iota(jnp.int32, (c, c), 0)
    col = lax.broadcasted_iota(jnp.int32, (c, c), 1)
    incl = (row >= col) if fwd else (row <= col)
    strict = (row > col) if fwd else (row < col)
    lane = lax.broadcasted_iota(jnp.int32, (c, 2 * DK), 1)
    beta_all, gc_all, gc_all_t, eg_all, tail_all, eglast_all = gates
    heads = range(H_A)

    kmask, decay, uw_rhs16, qh, kt16, kq = [], [], [], [], [], []
    for h in heads:
        j = d * H_A + h
        km = (lane < DK) if h % 2 else (lane >= DK)
        kv = qkv_ref[bb, :, W_A + h * 2 * DK:W_A + (h + 1) * 2 * DK]
        qs = qkv_ref[bb, :, (h // 2) * 2 * DK:(h // 2 + 1) * 2 * DK]
        beta = jnp.broadcast_to(beta_all[:, j:j + 1], kv.shape)
        eg = jnp.broadcast_to(eg_all[:, j:j + 1], kv.shape)
        kvb = kv * beta
        kn = jnp.where(km, kv, 0.0)
        qm = jnp.where(km, qs, 0.0)
        diff = gc_all[:, j:j + 1] - gc_all_t[j:j + 1, :]
        kmask.append(km)
        decay.append(jnp.where(incl, jnp.exp(jnp.where(incl, diff, 0.0)), 0.0))
        uw_rhs16.append(jnp.where(km, kvb * eg, kvb).astype(BF16))
        qh.append(qm * eg)
        kt16.append((kn * tail_all[:, j:j + 1]).astype(BF16))
        kbq16 = jnp.concatenate([jnp.where(km, kvb, 0.0), qm], axis=0).astype(BF16)
        kq.append(_dot_nt(kbq16, kn.astype(BF16)))
    yield

    lows = [jnp.where(strict, kq[h][:c] * decay[h], 0.0) for h in heads]
    attn16 = [(kq[h][c:] * decay[h]).astype(BF16) for h in heads]
    tmat = yield from _unit_tri_inverse_stages(lows, lower=fwd)
    uw = [_dot(tmat[h].astype(BF16), uw_rhs16[h]) for h in heads]
    yield

    s_old = [s_ref[bb, d, h] for h in heads]
    wq = [_dot(jnp.concatenate([uw[h], qh[h]], axis=0).astype(BF16), s_old[h].astype(BF16))
          for h in heads]
    yield

    v_new16 = [jnp.where(kmask[h], 0.0, uw[h] - wq[h][:c]).astype(BF16) for h in heads]
    o = [wq[h][c:] + _dot(attn16[h], v_new16[h]) for h in heads]
    for h in heads:
        j = d * H_A + h
        k0 = 0 if h % 2 else DK
        s_ref[bb, d, h, k0:k0 + DK, :] = (s_old[h][k0:k0 + DK] * eglast_all[:, j:j + 1]
                                          + _dot_tn(kt16[h][:, k0:k0 + DK], v_new16[h]))
    for h in range(0, H_A, 2):
        o_ref[bb, :, h * DV:(h + 2) * DV] = o[h] + o[h + 1]


def _delta_kernel(*refs, zero_init, emit_state):
    (qkv_f_ref, qkv_b_ref, ba_f_ref, ba_b_ref, bat_f_ref, bat_b_ref,
     alog_ref, dtb_ref, alog_t_ref, dtb_t_ref) = refs[:10]
    rest = list(refs[10:])
    s0_ref = None if zero_init else rest.pop(0)
    o_f_ref, o_b_ref = rest.pop(0), rest.pop(0)
    sfin_ref = rest.pop(0) if emit_state else None
    s_ref = rest.pop(0)

    n = pl.program_id(1)
    n_chunks = pl.num_programs(1)
    nb = s_ref.shape[0]
    state_at = lambda h: (0, DV) if h % 2 else (DK, 0)

    @pl.when(n == 0)
    def _():
        s_ref[...] = jnp.zeros(s_ref.shape, F32)
        if not zero_init:
            for bb in range(nb):
                for d in range(2):
                    for h in range(H_A):
                        r0, l0 = state_at(h)
                        s_ref[bb, d, h, r0:r0 + DK, l0:l0 + DV] = s0_ref[bb, d, h]

    c = CHUNK
    row = lax.broadcasted_iota(jnp.int32, (c, c), 0)
    col = lax.broadcasted_iota(jnp.int32, (c, c), 1)
    tril16 = (row >= col).astype(BF16)
    triu16 = (row <= col).astype(BF16)

    programs = []
    for bb in range(nb):
        for d, (qkv_ref, ba_ref, bat_ref, o_ref) in enumerate(
                ((qkv_f_ref, ba_f_ref, bat_f_ref, o_f_ref),
                 (qkv_b_ref, ba_b_ref, bat_b_ref, o_b_ref))):
            fwd = d == 0
            csum16, csum_t16 = (tril16, triu16) if fwd else (triu16, tril16)
            last = c - 1 if fwd else 0
            ba = ba_ref[bb]
            bat = bat_ref[bb, 0]
            beta_all = jax.nn.sigmoid(ba[:, 0:2 * H_A])
            g_all = -jnp.exp(alog_ref[...]) * jax.nn.softplus(ba[:, 2 * H_A:] + dtb_ref[...])
            g_all_t = -jnp.exp(alog_t_ref[...]) * jax.nn.softplus(
                bat[2 * H_A:, :] + dtb_t_ref[...])
            gc_all = sum(_dot(csum16, p) for p in _split3(g_all))
            gc_all_t = sum(_dot(p, csum_t16) for p in _split3(g_all_t))
            glast_all = gc_all[last:last + 1, :]
            gates = (beta_all, gc_all, gc_all_t, jnp.exp(gc_all), jnp.exp(glast_all - gc_all),
                     jnp.exp(glast_all))
            programs.append(_group_stages(qkv_ref, o_ref, s_ref, bb, d, gates))

    while programs:
        for prog in list(programs):
            try:
                next(prog)
            except StopIteration:
                programs.remove(prog)

    if emit_state:
        @pl.when(n == n_chunks - 1)
        def _():
            for bb in range(nb):
                for d in range(2):
                    for h in range(H_A):
                        r0, l0 = state_at(h)
                        sfin_ref[bb, d, h] = s_ref[bb, d, h, r0:r0 + DK, l0:l0 + DV]


def _delta_scan(qkv, ba, bat, a_log, dt_bias, s0, *, batch, emit_state):
    n_tok = qkv.shape[0]
    t = n_tok // batch
    nc = t // CHUNK
    nb = DELTA_BATCH
    assert batch % nb == 0
    qkv = qkv.reshape(batch, t, N_QKV)
    ba = ba.reshape(batch, t, 4 * H_A)
    bat = bat.reshape(batch, nc, 4 * H_A, CHUNK)
    zero_init = s0 is None
    alog = a_log.reshape(1, 2 * H_A)
    dtb = dt_bias.reshape(1, 2 * H_A)
    alog_t = a_log.reshape(2 * H_A, 1)
    dtb_t = dt_bias.reshape(2 * H_A, 1)

    fw = lambda b, n: (b, n, 0)
    bw = lambda b, n: (b, nc - 1 - n, 0)
    small = lambda b, n: (0, 0)
    in_specs = [pl.BlockSpec((nb, CHUNK, N_QKV), fw),
                pl.BlockSpec((nb, CHUNK, N_QKV), bw),
                pl.BlockSpec((nb, CHUNK, 4 * H_A), fw),
                pl.BlockSpec((nb, CHUNK, 4 * H_A), bw),
                pl.BlockSpec((nb, 1, 4 * H_A, CHUNK), lambda b, n: (b, n, 0, 0)),
                pl.BlockSpec((nb, 1, 4 * H_A, CHUNK), lambda b, n: (b, nc - 1 - n, 0, 0)),
                pl.BlockSpec((1, 2 * H_A), small),
                pl.BlockSpec((1, 2 * H_A), small),
                pl.BlockSpec((2 * H_A, 1), small),
                pl.BlockSpec((2 * H_A, 1), small)]
    args = [qkv, qkv, ba, ba, bat, bat, alog, dtb, alog_t, dtb_t]
    state_spec = pl.BlockSpec((nb, 2, H_A, DK, DV), lambda b, n: (b, 0, 0, 0, 0))
    if not zero_init:
        in_specs.append(state_spec)
        args.append(s0)
    out_specs = [pl.BlockSpec((nb, CHUNK, W_A), fw), pl.BlockSpec((nb, CHUNK, W_A), bw)]
    out_shape = [jax.ShapeDtypeStruct((batch, t, W_A), F32),
                 jax.ShapeDtypeStruct((batch, t, W_A), F32)]
    if emit_state:
        out_specs.append(state_spec)
        out_shape.append(jax.ShapeDtypeStruct((batch, 2, H_A, DK, DV), F32))
    return pl.pallas_call(
        functools.partial(_delta_kernel, zero_init=zero_init, emit_state=emit_state),
        grid=(batch // nb, nc),
        in_specs=in_specs,
        out_specs=out_specs,
        out_shape=out_shape,
        scratch_shapes=[pltpu.VMEM((nb, 2, H_A, 2 * DK, 2 * DV), F32)],
        compiler_params=pltpu.CompilerParams(dimension_semantics=("arbitrary", "arbitrary"),
                                             vmem_limit_bytes=VMEM_LIMIT),
        name="delta_scan",
    )(*args)


def _mix_kernel(x_ref, mod_ref, gain_ref, of_ref, ob_ref, wz_ref, wsc_ref, wg_ref, conv_ref,
                onorm_ref, hsum_ref, wa_ref, wb_ref, wo_ref, out_ref, *, row_len):
    x = x_ref[...]
    h = _rms(x, gain_ref[0:1, :]) * (1.0 + mod_ref[0, 1:2, :]) + mod_ref[0, 0:1, :]
    hb = h.astype(BF16)
    o = of_ref[...] + ob_ref[...]
    rest = o * o
    ms = jnp.zeros_like(rest)
    for _ in range(3):
        piece = rest.astype(BF16)
        rest = rest - piece.astype(F32)
        ms = ms + _dot(piece, hsum_ref[...])
    ms = ms * (1.0 / DV)
    z = _dot(hb, wz_ref[...])
    oa = o * lax.rsqrt(ms + EPS) * onorm_ref[...] * _silu(z)
    y_a = _dot(oa.astype(BF16), wa_ref[...])
    sc = _dot(hb, wsc_ref[...])
    bg, cg, xs = sc[:, 0:W_B], sc[:, W_B:2 * W_B], sc[:, 2 * W_B:3 * W_B]
    yb_in = bg * _conv3_rows(cg * xs, conv_ref, row_len)
    y_b = _dot(yb_in.astype(BF16), wb_ref[...])
    gates = _dot(hb, wg_ref[...])
    m = jax.nn.sigmoid(gates[:, 0:D_MODEL]) * y_a + jax.nn.sigmoid(gates[:, D_MODEL:]) * y_b
    y = _dot(m.astype(BF16), wo_ref[...])
    out_ref[...] = x + mod_ref[0, 2:3, :] * _rms(y, gain_ref[1:2, :])


def _mix(x, mod3, gains, o_f, o_b, w_z, w_sc, w_g, conv_sc, onorm_t, head_sum, w_a, w_b, w_o,
         *, mod_row_fn, row_len):
    n_tok = x.shape[0]
    tm = WIDE_TILE
    const = lambda i: (0, 0)
    tok = lambda i: (i, 0)
    return pl.pallas_call(
        functools.partial(_mix_kernel, row_len=row_len),
        grid=(n_tok // tm,),
        in_specs=[pl.BlockSpec((tm, D_MODEL), tok),
                  pl.BlockSpec((1, 6, D_MODEL), lambda i: (mod_row_fn(i, tm), 0, 0)),
                  pl.BlockSpec((4, D_MODEL), const),
                  pl.BlockSpec((tm, W_A), tok),
                  pl.BlockSpec((tm, W_A), tok),
                  _resident((D_MODEL, W_A)),
                  _resident((D_MODEL, 3 * W_B)),
                  _resident((D_MODEL, 2 * D_MODEL)),
                  pl.BlockSpec((3, W_B), const),
                  pl.BlockSpec((1, W_A), const),
                  _resident((W_A, W_A)),
                  _resident((W_A, D_MODEL)),
                  _resident((W_B, D_MODEL)),
                  _resident((D_MODEL, D_MODEL))],
        out_specs=pl.BlockSpec((tm, D_MODEL), tok),
        out_shape=jax.ShapeDtypeStruct((n_tok, D_MODEL), F32),
        compiler_params=pltpu.CompilerParams(dimension_semantics=("arbitrary",),
                                             vmem_limit_bytes=VMEM_LIMIT),
        name="mix",
    )(x, mod3, gains, o_f, o_b, w_z, w_sc, w_g, conv_sc, onorm_t, head_sum, w_a, w_b, w_o)


def _ffn_kernel(x_ref, mod_ref, gain_ref, win_ref, wout_ref, out_ref):
    x = x_ref[...]
    h = _rms(x, gain_ref[2:3, :]) * (1.0 + mod_ref[0, 4:5, :]) + mod_ref[0, 3:4, :]
    hb = h.astype(BF16)
    gate = _dot(hb, win_ref[:, 0:D_FF])
    up = _dot(hb, win_ref[:, D_FF:])
    y = _dot((_silu(gate) * up).astype(BF16), wout_ref[...])
    out_ref[...] = x + mod_ref[0, 5:6, :] * _rms(y, gain_ref[3:4, :])


def _ffn(x, mod3, gains, w_in, w_out, *, mod_row_fn):
    n_tok = x.shape[0]
    tm = WIDE_TILE
    const = lambda i: (0, 0)
    tok = lambda i: (i, 0)
    return pl.pallas_call(
        _ffn_kernel,
        grid=(n_tok // tm,),
        in_specs=[pl.BlockSpec((tm, D_MODEL), tok),
                  pl.BlockSpec((1, 6, D_MODEL), lambda i: (mod_row_fn(i, tm), 0, 0)),
                  pl.BlockSpec((4, D_MODEL), const),
                  _resident((D_MODEL, 2 * D_FF)),
                  _resident((D_FF, D_MODEL))],
        out_specs=pl.BlockSpec((tm, D_MODEL), tok),
        out_shape=jax.ShapeDtypeStruct((n_tok, D_MODEL), F32),
        compiler_params=pltpu.CompilerParams(dimension_semantics=("arbitrary",),
                                             vmem_limit_bytes=VMEM_LIMIT),
        name="ffn",
    )(x, mod3, gains, w_in, w_out)


def _to_slab_order(w):
    rows = w.shape[0]
    pairs = lambda a: a.reshape(rows, H_A // 2, 2, DK)
    q, k, v = (pairs(w[:, i * W_A:(i + 1) * W_A]) for i in range(3))
    q_slabs = q[:, :, ::-1, :].reshape(rows, W_A)
    even = jnp.concatenate([v[:, :, 0], k[:, :, 0]], axis=-1)
    odd = jnp.concatenate([k[:, :, 1], v[:, :, 1]], axis=-1)
    kv_slabs = jnp.stack([even, odd], axis=2).reshape(rows, 2 * W_A)
    return jnp.concatenate([q_slabs, kv_slabs], axis=-1)


def kernel(x_prompt, x_sample, state_delta, c, c_ctx, w_mod, b_mod, norm_gains, w_in, conv_qkv,
           a_log, dt_bias, onorm, conv_sc, w_branch_a, w_branch_b, w_out, w_ffn_in, w_ffn_out):
    depth = w_in.shape[0]
    assert depth == 1
    b_ctx, t_ctx, _ = x_prompt.shape
    b_lat, t_lat, _ = x_sample.shape
    for tile in (TOKEN_TILE, WIDE_TILE):
        assert tile % t_ctx == 0 and t_lat % tile == 0 and tile % GRID_W == 0
    assert t_ctx & (t_ctx - 1) == 0
    assert b_lat + 1 <= MOD_ROWS
    layer = 0

    cvec = jnp.zeros((MOD_ROWS, D_MODEL), F32).at[0].set(c_ctx).at[1:1 + b_lat].set(c)
    mod3 = _modulation(cvec, w_mod[layer], b_mod[layer][None, :]).reshape(MOD_ROWS, 6, D_MODEL)

    o_qkv, o_z = N_QKV, N_QKV + W_A
    o_beta, o_sc = o_z + 4 * H_A, o_z + 4 * H_A + 3 * W_B
    wi = w_in[layer]
    w_qkv = _to_slab_order(wi[:, :o_qkv]).astype(BF16)
    conv_perm = _to_slab_order(conv_qkv[layer])
    w_z = wi[:, o_qkv:o_z].astype(BF16)
    w_ba = wi[:, o_z:o_beta].astype(BF16)
    w_bat = w_ba.T
    w_sc = wi[:, o_beta:o_sc].astype(BF16)
    w_g = wi[:, o_sc:].astype(BF16)
    w_a = w_branch_a[layer].astype(BF16)
    w_b = w_branch_b[layer].astype(BF16)
    w_o = w_out[layer].astype(BF16)
    w_f1 = w_ffn_in[layer].astype(BF16)
    w_f2 = w_ffn_out[layer].astype(BF16)
    gains = norm_gains[layer]
    onorm_t = jnp.tile(onorm[layer], H_A)[None, :]
    lane_head = jnp.arange(W_A) // DV
    head_sum = (lane_head[:, None] == lane_head[None, :]).astype(BF16)

    def run_group(x3, s0, mod_row_fn, row_len, emit_state):
        batch, t, _ = x3.shape
        x = x3.reshape(batch * t, D_MODEL)
        qkv, ba, bat = _inproj(x, mod3, gains, w_qkv, w_ba, w_bat, conv_perm,
                               mod_row_fn=mod_row_fn, row_len=row_len)
        res = _delta_scan(qkv, ba, bat, a_log[layer], dt_bias[layer], s0,
                          batch=batch, emit_state=emit_state)
        o_f = res[0].reshape(batch * t, W_A)
        o_b = res[1].reshape(batch * t, W_A)
        x1 = _mix(x, mod3, gains, o_f, o_b, w_z, w_sc, w_g, conv_sc[layer], onorm_t, head_sum,
                  w_a, w_b, w_o, mod_row_fn=mod_row_fn, row_len=row_len)
        x2 = _ffn(x1, mod3, gains, w_f1, w_f2, mod_row_fn=mod_row_fn)
        return x2.reshape(batch, t, D_MODEL), (res[2] if emit_state else None)

    y_p, s_fin = run_group(x_prompt, None, lambda i, tm: 0, t_ctx, True)
    y_s, _ = run_group(x_sample, state_delta[:, layer], lambda i, tm: 1 + (i * tm) // t_lat,
                       GRID_W, False)
    new_state = s_fin[:, None].astype(x_prompt.dtype)
    return (y_p, y_s, new_state)
```

```python
import functools

import jax
import jax.numpy as jnp
from jax import lax
from jax.experimental import pallas as pl
from jax.experimental.pallas import tpu as pltpu

D_MODEL = 1024
H_A = 8
DK = 64
DV = 64
W_A = H_A * DV
W_B = 512
CHUNK = 64
GRID_W = 64
EPS = 1e-6
N_QKV = 3 * H_A * DK
D_FF = 2816
SUBLANES = 8
MOD_ROWS = SUBLANES
TOKEN_TILE = 256
WIDE_TILE = 512
SUB_TILE = 256
DELTA_BATCH = 2
VMEM_LIMIT = 56 * 1024 * 1024

F32 = jnp.float32
BF16 = jnp.bfloat16
HIGHEST = lax.Precision.HIGHEST


def _dot(a, b, precision=None):
    return jnp.dot(a, b, preferred_element_type=F32, precision=precision)


def _dot_nt(a, b, precision=None):
    return lax.dot_general(a, b, (((1,), (1,)), ((), ())), preferred_element_type=F32,
                           precision=precision)


def _dot_tn(a, b, precision=None):
    return lax.dot_general(a, b, (((0,), (0,)), ((), ())), preferred_element_type=F32,
                           precision=precision)


def _resident(shape):
    return pl.BlockSpec(shape, lambda i: (0,) * len(shape), pipeline_mode=pl.Buffered(1))


def _lock_step(programs):
    programs = list(programs)
    while programs:
        for prog in list(programs):
            try:
                next(prog)
            except StopIteration:
                programs.remove(prog)


def _silu(x):
    return x * jax.nn.sigmoid(x)


def _rms(x, gain):
    return x * lax.rsqrt(jnp.mean(x * x, axis=-1, keepdims=True) + EPS) * gain


def _conv3_rows(x, w_ref, row_len):
    n = x.shape[0]
    pos = lax.broadcasted_iota(jnp.int32, (n, 1), 0) & (row_len - 1)
    prev = jnp.where(pos == 0, 0.0, pltpu.roll(x, 1, 0))
    nxt = jnp.where(pos == row_len - 1, 0.0, pltpu.roll(x, n - 1, 0))
    return prev * w_ref[0:1, :] + x * w_ref[1:2, :] + nxt * w_ref[2:3, :]


def _mod_kernel(c_ref, w_ref, b_ref, o_ref):
    rows = c_ref.shape[0]
    s3 = jnp.concatenate([p.astype(F32) for p in _split3(_silu(c_ref[...]))], axis=0).astype(BF16)
    w = w_ref[...]
    w_hi = w.astype(BF16)
    w_lo = (w - w_hi.astype(F32)).astype(BF16)
    acc = _dot(s3, w_hi) + _dot(s3, w_lo)
    o_ref[...] = acc[0:rows] + acc[rows:2 * rows] + acc[2 * rows:] + b_ref[...]


def _modulation(cvec, w_mod, b_mod):
    n_out = w_mod.shape[1]
    blk = D_MODEL
    return pl.pallas_call(
        _mod_kernel,
        grid=(n_out // blk,),
        in_specs=[pl.BlockSpec((MOD_ROWS, D_MODEL), lambda j: (0, 0)),
                  pl.BlockSpec((D_MODEL, blk), lambda j: (0, j)),
                  pl.BlockSpec((1, blk), lambda j: (0, j))],
        out_specs=pl.BlockSpec((MOD_ROWS, blk), lambda j: (0, j)),
        out_shape=jax.ShapeDtypeStruct((MOD_ROWS, n_out), F32),
        name="modulation",
    )(cvec, w_mod, b_mod)


def _inproj_kernel(x_ref, mod_ref, gain_ref, wqkv_ref, wba_ref, wbat_ref, conv_ref,
                   qkv_ref, ba_ref, bat_ref, *, row_len):
    x = x_ref[...]
    h = _rms(x, gain_ref[0:1, :]) * (1.0 + mod_ref[0, 1:2, :]) + mod_ref[0, 0:1, :]
    hb = h.astype(BF16)
    slab = 2 * DK
    lo = lax.broadcasted_iota(jnp.int32, (x.shape[0], slab), 1) < DK
    ssq = lambda v, m: jnp.sum(jnp.where(m, v * v, 0.0), axis=-1, keepdims=True)
    for pair in range(N_QKV // (2 * slab)):
        cols2 = slice(pair * 2 * slab, (pair + 1) * 2 * slab)
        v2 = _silu(_conv3_rows(_dot(hb, wqkv_ref[:, cols2]), conv_ref.at[:, cols2], row_len))
        for half in range(2):
            s = 2 * pair + half
            cols = slice(s * slab, (s + 1) * slab)
            v = v2[:, half * slab:(half + 1) * slab]
            if s < W_A // slab:
                r = jnp.where(lo, lax.rsqrt(ssq(v, lo) + EPS), lax.rsqrt(ssq(v, ~lo) + EPS))
                qkv_ref[:, cols] = v * (r * (DK ** -0.5))
            else:
                kmask = lo if (s - W_A // slab) % 2 else ~lo
                qkv_ref[:, cols] = v * jnp.where(kmask, lax.rsqrt(ssq(v, kmask) + EPS), 1.0)
    ba_ref[...] = _dot(hb, wba_ref[...])
    for ci in range(x.shape[0] // CHUNK):
        bat_ref[ci] = _dot_nt(wbat_ref[...], hb[ci * CHUNK:(ci + 1) * CHUNK, :])


def _inproj(x, mod3, gains, w_qkv, w_ba, w_bat, conv_qkv, *, mod_row_fn, row_len):
    n_tok = x.shape[0]
    tm = TOKEN_TILE
    const = lambda i: (0, 0)
    return pl.pallas_call(
        functools.partial(_inproj_kernel, row_len=row_len),
        grid=(n_tok // tm,),
        in_specs=[pl.BlockSpec((tm, D_MODEL), lambda i: (i, 0)),
                  pl.BlockSpec((1, 6, D_MODEL), lambda i: (mod_row_fn(i, tm), 0, 0)),
                  pl.BlockSpec((4, D_MODEL), const),
                  pl.BlockSpec((D_MODEL, N_QKV), const),
                  pl.BlockSpec((D_MODEL, 4 * H_A), const),
                  pl.BlockSpec((4 * H_A, D_MODEL), const),
                  pl.BlockSpec((3, N_QKV), const)],
        out_specs=[pl.BlockSpec((tm, N_QKV), lambda i: (i, 0)),
                   pl.BlockSpec((tm, 4 * H_A), lambda i: (i, 0)),
                   pl.BlockSpec((tm // CHUNK, 4 * H_A, CHUNK), lambda i: (i, 0, 0))],
        out_shape=[jax.ShapeDtypeStruct((n_tok, N_QKV), F32),
                   jax.ShapeDtypeStruct((n_tok, 4 * H_A), F32),
                   jax.ShapeDtypeStruct((n_tok // CHUNK, 4 * H_A, CHUNK), F32)],
        compiler_params=pltpu.CompilerParams(dimension_semantics=("arbitrary",),
                                             vmem_limit_bytes=VMEM_LIMIT),
        name="inproj",
    )(x, mod3, gains, w_qkv, w_ba, w_bat, conv_qkv)


def _unit_tri_inverse_stages(lows, lower):
    c = CHUNK
    ri = lax.broadcasted_iota(jnp.int32, (c, c), 0)
    ci = lax.broadcasted_iota(jnp.int32, (c, c), 1)
    eye = (ri == ci).astype(F32)
    rng = range(len(lows))
    joins = lambda b: ((ri // (2 * b)) == (ci // (2 * b))) & ((ri // b) != (ci // b))
    t = [eye - jnp.where(joins(1), lows[i], 0.0) for i in rng]
    b = 2
    while b < c:
        mask = joins(b)
        t16 = [t[i].astype(BF16) for i in rng]
        e16 = [jnp.where(mask, lows[i], 0.0).astype(BF16) for i in rng]
        if b % SUBLANES:
            x = [_dot(t16[i], e16[i]) for i in rng]
            yield
            t = [t[i] - _dot(x[i].astype(BF16), t16[i]) for i in rng]
            yield
        else:
            first = [slice(s, s + b) for s in range(0, c, 2 * b)]
            second = [slice(s + b, s + 2 * b) for s in range(0, c, 2 * b)]
            keep, upd = (first, second) if lower else (second, first)
            t_upd = [jnp.concatenate([t[i][r] for r in upd], axis=0) for i in rng]
            x = [_dot(t_upd[i].astype(BF16), e16[i]) for i in rng]
            yield
            t_upd = [t_upd[i] - _dot(x[i].astype(BF16), t16[i]) for i in rng]
            yield
            pair = lambda kept, new: (kept, new) if lower else (new, kept)
            t = [jnp.concatenate(
                [piece for k, r in enumerate(keep)
                 for piece in pair(t[i][r], t_upd[i][k * b:(k + 1) * b])], axis=0) for i in rng]
        b *= 2
    return t


def _split3(x):
    pieces = []
    for _ in range(3):
        p = x.astype(BF16)
        x = x - p.astype(F32)
        pieces.append(p)
    return pieces


def _group_stages(qkv_ref, o_ref, s_ref, bb, d, gates):
    c = CHUNK
    fwd = d == 0
    row = lax.broadcasted_iota(jnp.int32, (c, c), 0)
    col = lax.broadcasted_iota(jnp.int32, (c, c), 1)
    incl = (row >= col) if fwd else (row <= col)
    strict = (row > col) if fwd else (row < col)
    lane = lax.broadcasted_iota(jnp.int32, (c, 2 * DK), 1)
    beta_all, beta_all_t, gc_all, gc_all_t, eg_all, tail_all, eglast_all = gates
    heads = range(H_A)
    kmask = [(lane < DK) if h % 2 else (lane >= DK) for h in heads]
    kv_slab = lambda h: qkv_ref[bb, :, W_A + h * 2 * DK:W_A + (h + 1) * 2 * DK]
    q_slab = lambda h: qkv_ref[bb, :, (h // 2) * 2 * DK:(h // 2 + 1) * 2 * DK]

    decay, kq = [], []
    for h in heads:
        j = d * H_A + h
        kv = kv_slab(h)
        lhs16 = jnp.concatenate([kv * beta_all[:, j:j + 1], q_slab(h)], axis=0).astype(BF16)
        kq.append(_dot_nt(lhs16, jnp.where(kmask[h], kv, 0.0).astype(BF16)))
        diff = gc_all[:, j:j + 1] - gc_all_t[j:j + 1, :]
        decay.append(jnp.where(incl, jnp.exp(jnp.where(incl, diff, 0.0)), 0.0))
    yield

    lows = [jnp.where(strict, kq[h][:c] * decay[h], 0.0) for h in heads]
    attn16 = [(kq[h][c:] * decay[h]).astype(BF16) for h in heads]
    tmat = yield from _unit_tri_inverse_stages(lows, lower=fwd)

    uw, qh = [], []
    for h in heads:
        j = d * H_A + h
        eg = jnp.broadcast_to(eg_all[:, j:j + 1], (c, 2 * DK))
        kv = kv_slab(h)
        t_beta16 = (tmat[h] * beta_all_t[j:j + 1, :]).astype(BF16)
        uw.append(_dot(t_beta16, jnp.where(kmask[h], kv * eg, kv).astype(BF16)))
        qh.append(q_slab(h) * eg)
    yield

    s_old = [s_ref[bb, d, h] for h in heads]
    wq = [_dot(jnp.concatenate([uw[h], qh[h]], axis=0).astype(BF16), s_old[h].astype(BF16))
          for h in heads]
    yield

    v_new16 = [jnp.where(kmask[h], 0.0, uw[h] - wq[h][:c]).astype(BF16) for h in heads]
    o = [wq[h][c:] + _dot(attn16[h], v_new16[h]) for h in heads]
    for h in heads:
        j = d * H_A + h
        k0 = 0 if h % 2 else DK
        kt16 = (kv_slab(h)[:, k0:k0 + DK] * tail_all[:, j:j + 1]).astype(BF16)
        s_ref[bb, d, h, k0:k0 + DK, :] = (s_old[h][k0:k0 + DK] * eglast_all[:, j:j + 1]
                                          + _dot_tn(kt16, v_new16[h]))
    for h in range(0, H_A, 2):
        o_ref[bb, :, h * DV:(h + 2) * DV] = o[h] + o[h + 1]


def _delta_kernel(*refs, zero_init, emit_state):
    (qkv_f_ref, qkv_b_ref, ba_f_ref, ba_b_ref, bat_f_ref, bat_b_ref,
     alog_ref, dtb_ref, alog_t_ref, dtb_t_ref) = refs[:10]
    rest = list(refs[10:])
    s0_ref = None if zero_init else rest.pop(0)
    o_f_ref, o_b_ref = rest.pop(0), rest.pop(0)
    sfin_ref = rest.pop(0) if emit_state else None
    s_ref = rest.pop(0)

    n = pl.program_id(1)
    n_chunks = pl.num_programs(1)
    nb = s_ref.shape[0]
    state_at = lambda h: (0, DV) if h % 2 else (DK, 0)

    @pl.when(n == 0)
    def _():
        s_ref[...] = jnp.zeros(s_ref.shape, F32)
        if not zero_init:
            for bb in range(nb):
                for d in range(2):
                    for h in range(H_A):
                        r0, l0 = state_at(h)
                        s_ref[bb, d, h, r0:r0 + DK, l0:l0 + DV] = s0_ref[bb, d, h]

    c = CHUNK
    row = lax.broadcasted_iota(jnp.int32, (c, c), 0)
    col = lax.broadcasted_iota(jnp.int32, (c, c), 1)
    tril16 = (row >= col).astype(BF16)
    triu16 = (row <= col).astype(BF16)

    programs = []
    for bb in range(nb):
        for d, (qkv_ref, ba_ref, bat_ref, o_ref) in enumerate(
                ((qkv_f_ref, ba_f_ref, bat_f_ref, o_f_ref),
                 (qkv_b_ref, ba_b_ref, bat_b_ref, o_b_ref))):
            fwd = d == 0
            csum16, csum_t16 = (tril16, triu16) if fwd else (triu16, tril16)
            last = c - 1 if fwd else 0
            ba = ba_ref[bb]
            bat = bat_ref[bb, 0]
            beta_all = jax.nn.sigmoid(ba[:, 0:2 * H_A])
            g_all = -jnp.exp(alog_ref[...]) * jax.nn.softplus(ba[:, 2 * H_A:] + dtb_ref[...])
            g_all_t = -jnp.exp(alog_t_ref[...]) * jax.nn.softplus(
                bat[2 * H_A:, :] + dtb_t_ref[...])
            gc_all = sum(_dot(csum16, p) for p in _split3(g_all))
            gc_all_t = sum(_dot(p, csum_t16) for p in _split3(g_all_t))
            glast_all = gc_all[last:last + 1, :]
            beta_all_t = jax.nn.sigmoid(bat[0:2 * H_A, :])
            gates = (beta_all, beta_all_t, gc_all, gc_all_t, jnp.exp(gc_all),
                     jnp.exp(glast_all - gc_all), jnp.exp(glast_all))
            programs.append(_group_stages(qkv_ref, o_ref, s_ref, bb, d, gates))

    _lock_step(programs)

    if emit_state:
        @pl.when(n == n_chunks - 1)
        def _():
            for bb in range(nb):
                for d in range(2):
                    for h in range(H_A):
                        r0, l0 = state_at(h)
                        sfin_ref[bb, d, h] = s_ref[bb, d, h, r0:r0 + DK, l0:l0 + DV]


def _delta_scan(qkv, ba, bat, a_log, dt_bias, s0, *, batch, emit_state):
    n_tok = qkv.shape[0]
    t = n_tok // batch
    nc = t // CHUNK
    nb = DELTA_BATCH
    assert batch % nb == 0
    qkv = qkv.reshape(batch, t, N_QKV)
    ba = ba.reshape(batch, t, 4 * H_A)
    bat = bat.reshape(batch, nc, 4 * H_A, CHUNK)
    zero_init = s0 is None
    alog = a_log.reshape(1, 2 * H_A)
    dtb = dt_bias.reshape(1, 2 * H_A)
    alog_t = a_log.reshape(2 * H_A, 1)
    dtb_t = dt_bias.reshape(2 * H_A, 1)

    fw = lambda b, n: (b, n, 0)
    bw = lambda b, n: (b, nc - 1 - n, 0)
    small = lambda b, n: (0, 0)
    in_specs = [pl.BlockSpec((nb, CHUNK, N_QKV), fw),
                pl.BlockSpec((nb, CHUNK, N_QKV), bw),
                pl.BlockSpec((nb, CHUNK, 4 * H_A), fw),
                pl.BlockSpec((nb, CHUNK, 4 * H_A), bw),
                pl.BlockSpec((nb, 1, 4 * H_A, CHUNK), lambda b, n: (b, n, 0, 0)),
                pl.BlockSpec((nb, 1, 4 * H_A, CHUNK), lambda b, n: (b, nc - 1 - n, 0, 0)),
                pl.BlockSpec((1, 2 * H_A), small),
                pl.BlockSpec((1, 2 * H_A), small),
                pl.BlockSpec((2 * H_A, 1), small),
                pl.BlockSpec((2 * H_A, 1), small)]
    args = [qkv, qkv, ba, ba, bat, bat, alog, dtb, alog_t, dtb_t]
    state_spec = pl.BlockSpec((nb, 2, H_A, DK, DV), lambda b, n: (b, 0, 0, 0, 0))
    if not zero_init:
        in_specs.append(state_spec)
        args.append(s0)
    out_specs = [pl.BlockSpec((nb, CHUNK, W_A), fw), pl.BlockSpec((nb, CHUNK, W_A), bw)]
    out_shape = [jax.ShapeDtypeStruct((batch, t, W_A), F32),
                 jax.ShapeDtypeStruct((batch, t, W_A), F32)]
    if emit_state:
        out_specs.append(state_spec)
        out_shape.append(jax.ShapeDtypeStruct((batch, 2, H_A, DK, DV), F32))
    return pl.pallas_call(
        functools.partial(_delta_kernel, zero_init=zero_init, emit_state=emit_state),
        grid=(batch // nb, nc),
        in_specs=in_specs,
        out_specs=out_specs,
        out_shape=out_shape,
        scratch_shapes=[pltpu.VMEM((nb, 2, H_A, 2 * DK, 2 * DV), F32)],
        compiler_params=pltpu.CompilerParams(dimension_semantics=("arbitrary", "arbitrary"),
                                             vmem_limit_bytes=VMEM_LIMIT),
        name="delta_scan",
    )(*args)


def _mix_kernel(x_ref, mod_ref, gain_ref, of_ref, ob_ref, wz_ref, wsc_ref, wg_ref, conv_ref,
                onorm_ref, wa_ref, wb_ref, wo_ref, out_ref, *, row_len):
    def sub_tile(rows):
        x = x_ref[rows, :]
        h = _rms(x, gain_ref[0:1, :]) * (1.0 + mod_ref[0, 1:2, :]) + mod_ref[0, 0:1, :]
        hb = h.astype(BF16)
        o = of_ref[rows, :] + ob_ref[rows, :]
        slab = 2 * DV
        lo = lax.broadcasted_iota(jnp.int32, (SUB_TILE, slab), 1) < DV
        inv_rms = []
        for s in range(W_A // slab):
            o2 = o[:, s * slab:(s + 1) * slab]
            o2 = o2 * o2
            ms_lo = jnp.sum(jnp.where(lo, o2, 0.0), axis=-1, keepdims=True) * (1.0 / DV)
            ms_hi = jnp.sum(jnp.where(lo, 0.0, o2), axis=-1, keepdims=True) * (1.0 / DV)
            inv_rms.append(jnp.where(lo, lax.rsqrt(ms_lo + EPS), lax.rsqrt(ms_hi + EPS)))
        yield
        z = _dot(hb, wz_ref[...])
        yield
        oa = o * jnp.concatenate(inv_rms, axis=-1) * onorm_ref[...] * _silu(z)
        y_a = _dot(oa.astype(BF16), wa_ref[...])
        sc = _dot(hb, wsc_ref[...])
        yield
        bg, cg, xs = sc[:, 0:W_B], sc[:, W_B:2 * W_B], sc[:, 2 * W_B:3 * W_B]
        yb_in = bg * _conv3_rows(cg * xs, conv_ref, row_len)
        y_b = _dot(yb_in.astype(BF16), wb_ref[...])
        gates = _dot(hb, wg_ref[...])
        yield
        m = (jax.nn.sigmoid(gates[:, 0:D_MODEL]) * y_a
             + jax.nn.sigmoid(gates[:, D_MODEL:]) * y_b)
        y = _dot(m.astype(BF16), wo_ref[...])
        yield
        out_ref[rows, :] = x + mod_ref[0, 2:3, :] * _rms(y, gain_ref[1:2, :])

    _lock_step(sub_tile(slice(r0, r0 + SUB_TILE)) for r0 in range(0, x_ref.shape[0], SUB_TILE))


def _mix(x, mod3, gains, o_f, o_b, w_z, w_sc, w_g, conv_sc, onorm_t, w_a, w_b, w_o,
         *, mod_row_fn, row_len):
    n_tok = x.shape[0]
    tm = WIDE_TILE
    const = lambda i: (0, 0)
    tok = lambda i: (i, 0)
    return pl.pallas_call(
        functools.partial(_mix_kernel, row_len=row_len),
        grid=(n_tok // tm,),
        in_specs=[pl.BlockSpec((tm, D_MODEL), tok),
                  pl.BlockSpec((1, 6, D_MODEL), lambda i: (mod_row_fn(i, tm), 0, 0)),
                  pl.BlockSpec((4, D_MODEL), const),
                  pl.BlockSpec((tm, W_A), tok),
                  pl.BlockSpec((tm, W_A), tok),
                  _resident((D_MODEL, W_A)),
                  _resident((D_MODEL, 3 * W_B)),
                  _resident((D_MODEL, 2 * D_MODEL)),
                  pl.BlockSpec((3, W_B), const),
                  pl.BlockSpec((1, W_A), const),
                  _resident((W_A, D_MODEL)),
                  _resident((W_B, D_MODEL)),
                  _resident((D_MODEL, D_MODEL))],
        out_specs=pl.BlockSpec((tm, D_MODEL), tok),
        out_shape=jax.ShapeDtypeStruct((n_tok, D_MODEL), F32),
        compiler_params=pltpu.CompilerParams(dimension_semantics=("arbitrary",),
                                             vmem_limit_bytes=VMEM_LIMIT),
        name="mix",
    )(x, mod3, gains, o_f, o_b, w_z, w_sc, w_g, conv_sc, onorm_t, w_a, w_b, w_o)


def _ffn_kernel(x_ref, mod_ref, gain_ref, win_ref, wout_ref, out_ref):
    def sub_tile(rows):
        x = x_ref[rows, :]
        h = _rms(x, gain_ref[2:3, :]) * (1.0 + mod_ref[0, 4:5, :]) + mod_ref[0, 3:4, :]
        hb = h.astype(BF16)
        yield
        gate = _dot(hb, win_ref[:, 0:D_FF])
        yield
        up = _dot(hb, win_ref[:, D_FF:])
        yield
        y = _dot((_silu(gate) * up).astype(BF16), wout_ref[...])
        yield
        out_ref[rows, :] = x + mod_ref[0, 5:6, :] * _rms(y, gain_ref[3:4, :])

    _lock_step(sub_tile(slice(r0, r0 + SUB_TILE)) for r0 in range(0, x_ref.shape[0], SUB_TILE))


def _ffn(x, mod3, gains, w_in, w_out, *, mod_row_fn):
    n_tok = x.shape[0]
    tm = WIDE_TILE
    const = lambda i: (0, 0)
    tok = lambda i: (i, 0)
    return pl.pallas_call(
        _ffn_kernel,
        grid=(n_tok // tm,),
        in_specs=[pl.BlockSpec((tm, D_MODEL), tok),
                  pl.BlockSpec((1, 6, D_MODEL), lambda i: (mod_row_fn(i, tm), 0, 0)),
                  pl.BlockSpec((4, D_MODEL), const),
                  _resident((D_MODEL, 2 * D_FF)),
                  _resident((D_FF, D_MODEL))],
        out_specs=pl.BlockSpec((tm, D_MODEL), tok),
        out_shape=jax.ShapeDtypeStruct((n_tok, D_MODEL), F32),
        compiler_params=pltpu.CompilerParams(dimension_semantics=("arbitrary",),
                                             vmem_limit_bytes=VMEM_LIMIT),
        name="ffn",
    )(x, mod3, gains, w_in, w_out)


def _to_slab_order(w):
    rows = w.shape[0]
    pairs = lambda a: a.reshape(rows, H_A // 2, 2, DK)
    q, k, v = (pairs(w[:, i * W_A:(i + 1) * W_A]) for i in range(3))
    q_slabs = q[:, :, ::-1, :].reshape(rows, W_A)
    even = jnp.concatenate([v[:, :, 0], k[:, :, 0]], axis=-1)
    odd = jnp.concatenate([k[:, :, 1], v[:, :, 1]], axis=-1)
    kv_slabs = jnp.stack([even, odd], axis=2).reshape(rows, 2 * W_A)
    return jnp.concatenate([q_slabs, kv_slabs], axis=-1)


def kernel(x_prompt, x_sample, state_delta, c, c_ctx, w_mod, b_mod, norm_gains, w_in, conv_qkv,
           a_log, dt_bias, onorm, conv_sc, w_branch_a, w_branch_b, w_out, w_ffn_in, w_ffn_out):
    depth = w_in.shape[0]
    assert depth == 1
    b_ctx, t_ctx, _ = x_prompt.shape
    b_lat, t_lat, _ = x_sample.shape
    for tile in (TOKEN_TILE, WIDE_TILE):
        assert tile % t_ctx == 0 and t_lat % tile == 0 and tile % GRID_W == 0
    assert t_ctx & (t_ctx - 1) == 0
    assert b_lat + 1 <= MOD_ROWS
    layer = 0

    cvec = jnp.zeros((MOD_ROWS, D_MODEL), F32).at[0].set(c_ctx).at[1:1 + b_lat].set(c)
    mod3 = _modulation(cvec, w_mod[layer], b_mod[layer][None, :]).reshape(MOD_ROWS, 6, D_MODEL)

    o_qkv, o_z = N_QKV, N_QKV + W_A
    o_beta, o_sc = o_z + 4 * H_A, o_z + 4 * H_A + 3 * W_B
    wi = w_in[layer]
    w_qkv = _to_slab_order(wi[:, :o_qkv]).astype(BF16)
    conv_perm = _to_slab_order(conv_qkv[layer])
    w_z = wi[:, o_qkv:o_z].astype(BF16)
    w_ba = wi[:, o_z:o_beta].astype(BF16)
    w_bat = w_ba.T
    w_sc = wi[:, o_beta:o_sc].astype(BF16)
    w_g = wi[:, o_sc:].astype(BF16)
    w_a = w_branch_a[layer].astype(BF16)
    w_b = w_branch_b[layer].astype(BF16)
    w_o = w_out[layer].astype(BF16)
    w_f1 = w_ffn_in[layer].astype(BF16)
    w_f2 = w_ffn_out[layer].astype(BF16)
    gains = norm_gains[layer]
    onorm_t = jnp.tile(onorm[layer], H_A)[None, :]

    def run_group(x3, s0, mod_row_fn, row_len, emit_state):
        batch, t, _ = x3.shape
        x = x3.reshape(batch * t, D_MODEL)
        qkv, ba, bat = _inproj(x, mod3, gains, w_qkv, w_ba, w_bat, conv_perm,
                               mod_row_fn=mod_row_fn, row_len=row_len)
        res = _delta_scan(qkv, ba, bat, a_log[layer], dt_bias[layer], s0,
                          batch=batch, emit_state=emit_state)
        o_f = res[0].reshape(batch * t, W_A)
        o_b = res[1].reshape(batch * t, W_A)
        x1 = _mix(x, mod3, gains, o_f, o_b, w_z, w_sc, w_g, conv_sc[layer], onorm_t,
                  w_a, w_b, w_o, mod_row_fn=mod_row_fn, row_len=row_len)
        x2 = _ffn(x1, mod3, gains, w_f1, w_f2, mod_row_fn=mod_row_fn)
        return x2.reshape(batch, t, D_MODEL), (res[2] if emit_state else None)

    y_p, s_fin = run_group(x_prompt, None, lambda i, tm: 0, t_ctx, True)
    y_s, _ = run_group(x_sample, state_delta[:, layer], lambda i, tm: 1 + (i * tm) // t_lat,
                       GRID_W, False)
    new_state = s_fin[:, None].astype(x_prompt.dtype)
    return (y_p, y_s, new_state)
```

```python
import functools

import jax
import jax.numpy as jnp
from jax import lax
from jax.experimental import pallas as pl
from jax.experimental.pallas import tpu as pltpu

D_MODEL = 1024
H_A = 8
DK = 64
DV = 64
W_A = H_A * DV
W_B = 512
CHUNK = 64
GRID_W = 64
EPS = 1e-6
N_QKV = 3 * H_A * DK
D_FF = 2816
SUBLANES = 8
MOD_ROWS = SUBLANES
WIDE_TILE = 512
SUB_TILE = 256
DELTA_BATCH = 4
VMEM_LIMIT = 56 * 1024 * 1024

F32 = jnp.float32
BF16 = jnp.bfloat16
HIGHEST = lax.Precision.HIGHEST


def _dot(a, b, precision=None):
    return jnp.dot(a, b, preferred_element_type=F32, precision=precision)


def _dot_nt(a, b, precision=None):
    return lax.dot_general(a, b, (((1,), (1,)), ((), ())), preferred_element_type=F32,
                           precision=precision)


def _dot_tn(a, b, precision=None):
    return lax.dot_general(a, b, (((0,), (0,)), ((), ())), preferred_element_type=F32,
                           precision=precision)


def _resident(shape):
    return pl.BlockSpec(shape, lambda i: (0,) * len(shape), pipeline_mode=pl.Buffered(1))


def _lock_step(programs):
    programs = list(programs)
    while programs:
        for prog in list(programs):
            try:
                next(prog)
            except StopIteration:
                programs.remove(prog)


def _silu(x):
    return x * jax.nn.sigmoid(x)


def _rms(x, gain):
    return x * lax.rsqrt(jnp.mean(x * x, axis=-1, keepdims=True) + EPS) * gain


def _conv3_rows(x, w_ref, row_len):
    n = x.shape[0]
    pos = lax.broadcasted_iota(jnp.int32, (n, 1), 0) & (row_len - 1)
    prev = jnp.where(pos == 0, 0.0, pltpu.roll(x, 1, 0))
    nxt = jnp.where(pos == row_len - 1, 0.0, pltpu.roll(x, n - 1, 0))
    return prev * w_ref[0:1, :] + x * w_ref[1:2, :] + nxt * w_ref[2:3, :]


def _mod_kernel(c_ref, w_ref, b_ref, o_ref):
    rows = c_ref.shape[0]
    s3 = jnp.concatenate([p.astype(F32) for p in _split3(_silu(c_ref[...]))], axis=0).astype(BF16)
    w = w_ref[...]
    w_hi = w.astype(BF16)
    w_lo = (w - w_hi.astype(F32)).astype(BF16)
    acc = _dot(s3, w_hi) + _dot(s3, w_lo)
    o_ref[...] = acc[0:rows] + acc[rows:2 * rows] + acc[2 * rows:] + b_ref[...]


def _modulation(cvec, w_mod, b_mod):
    n_out = w_mod.shape[1]
    blk = D_MODEL
    return pl.pallas_call(
        _mod_kernel,
        grid=(n_out // blk,),
        in_specs=[pl.BlockSpec((MOD_ROWS, D_MODEL), lambda j: (0, 0)),
                  pl.BlockSpec((D_MODEL, blk), lambda j: (0, j)),
                  pl.BlockSpec((1, blk), lambda j: (0, j))],
        out_specs=pl.BlockSpec((MOD_ROWS, blk), lambda j: (0, j)),
        out_shape=jax.ShapeDtypeStruct((MOD_ROWS, n_out), F32),
        name="modulation",
    )(cvec, w_mod, b_mod)


def _inproj_kernel(x_ref, mod_ref, gain_ref, wqkv_ref, wba_ref, wbat_ref, conv_ref,
                   qkv_ref, ba_ref, bat_ref, *, row_len):
    slab = 2 * DK
    lo = lax.broadcasted_iota(jnp.int32, (SUB_TILE, slab), 1) < DK
    ssq = lambda v, m: jnp.sum(jnp.where(m, v * v, 0.0), axis=-1, keepdims=True)

    def sub_tile(r0):
        rows = slice(r0, r0 + SUB_TILE)
        x = x_ref[rows, :]
        h = _rms(x, gain_ref[0:1, :]) * (1.0 + mod_ref[0, 1:2, :]) + mod_ref[0, 0:1, :]
        hb = h.astype(BF16)
        yield
        for pair in range(N_QKV // (2 * slab)):
            cols2 = slice(pair * 2 * slab, (pair + 1) * 2 * slab)
            v2 = _silu(_conv3_rows(_dot(hb, wqkv_ref[:, cols2]), conv_ref.at[:, cols2], row_len))
            for half in range(2):
                s = 2 * pair + half
                cols = slice(s * slab, (s + 1) * slab)
                v = v2[:, half * slab:(half + 1) * slab]
                if s < W_A // slab:
                    r = jnp.where(lo, lax.rsqrt(ssq(v, lo) + EPS) * (DK ** -0.5),
                                  lax.rsqrt(ssq(v, ~lo) + EPS) * (DK ** -0.5))
                    qkv_ref[rows, cols] = v * r
                else:
                    kmask = lo if (s - W_A // slab) % 2 else ~lo
                    qkv_ref[rows, cols] = v * jnp.where(kmask, lax.rsqrt(ssq(v, kmask) + EPS), 1.0)
            yield
        ba_ref[rows, :] = _dot(hb, wba_ref[...])
        for ci in range(SUB_TILE // CHUNK):
            bat_ref[r0 // CHUNK + ci] = _dot_nt(wbat_ref[...], hb[ci * CHUNK:(ci + 1) * CHUNK, :])

    _lock_step(sub_tile(r0) for r0 in range(0, x_ref.shape[0], SUB_TILE))


def _inproj(x, mod3, gains, w_qkv, w_ba, w_bat, conv_qkv, *, mod_row_fn, row_len):
    n_tok = x.shape[0]
    tm = WIDE_TILE
    const = lambda i: (0, 0)
    return pl.pallas_call(
        functools.partial(_inproj_kernel, row_len=row_len),
        grid=(n_tok // tm,),
        in_specs=[pl.BlockSpec((tm, D_MODEL), lambda i: (i, 0)),
                  pl.BlockSpec((1, 6, D_MODEL), lambda i: (mod_row_fn(i, tm), 0, 0)),
                  pl.BlockSpec((4, D_MODEL), const),
                  pl.BlockSpec((D_MODEL, N_QKV), const),
                  pl.BlockSpec((D_MODEL, 4 * H_A), const),
                  pl.BlockSpec((4 * H_A, D_MODEL), const),
                  pl.BlockSpec((3, N_QKV), const)],
        out_specs=[pl.BlockSpec((tm, N_QKV), lambda i: (i, 0)),
                   pl.BlockSpec((tm, 4 * H_A), lambda i: (i, 0)),
                   pl.BlockSpec((tm // CHUNK, 4 * H_A, CHUNK), lambda i: (i, 0, 0))],
        out_shape=[jax.ShapeDtypeStruct((n_tok, N_QKV), F32),
                   jax.ShapeDtypeStruct((n_tok, 4 * H_A), F32),
                   jax.ShapeDtypeStruct((n_tok // CHUNK, 4 * H_A, CHUNK), F32)],
        compiler_params=pltpu.CompilerParams(dimension_semantics=("arbitrary",),
                                             vmem_limit_bytes=VMEM_LIMIT),
        name="inproj",
    )(x, mod3, gains, w_qkv, w_ba, w_bat, conv_qkv)


def _unit_tri_inverse_stages(lows, lower):
    c = CHUNK
    ri = lax.broadcasted_iota(jnp.int32, (c, c), 0)
    ci = lax.broadcasted_iota(jnp.int32, (c, c), 1)
    eye = (ri == ci).astype(F32)
    rng = range(len(lows))
    joins = lambda b: ((ri // (2 * b)) == (ci // (2 * b))) & ((ri // b) != (ci // b))
    t = [eye - jnp.where(joins(1), lows[i], 0.0) for i in rng]
    b = 2
    while b < c:
        mask = joins(b)
        t16 = [t[i].astype(BF16) for i in rng]
        e16 = [jnp.where(mask, lows[i], 0.0).astype(BF16) for i in rng]
        if b % SUBLANES:
            x = [_dot(t16[i], e16[i]) for i in rng]
            yield
            t = [t[i] - _dot(x[i].astype(BF16), t16[i]) for i in rng]
            yield
        else:
            first = [slice(s, s + b) for s in range(0, c, 2 * b)]
            second = [slice(s + b, s + 2 * b) for s in range(0, c, 2 * b)]
            keep, upd = (first, second) if lower else (second, first)
            t_upd = [jnp.concatenate([t[i][r] for r in upd], axis=0) for i in rng]
            x = [_dot(t_upd[i].astype(BF16), e16[i]) for i in rng]
            yield
            t_upd = [t_upd[i] - _dot(x[i].astype(BF16), t16[i]) for i in rng]
            yield
            pair = lambda kept, new: (kept, new) if lower else (new, kept)
            t = [jnp.concatenate(
                [piece for k, r in enumerate(keep)
                 for piece in pair(t[i][r], t_upd[i][k * b:(k + 1) * b])], axis=0) for i in rng]
        b *= 2
    return t


def _split3(x):
    pieces = []
    for _ in range(3):
        p = x.astype(BF16)
        x = x - p.astype(F32)
        pieces.append(p)
    return pieces


def _group_stages(qkv_ref, o_ref, s_ref, bb, d, gates):
    c = CHUNK
    fwd = d == 0
    row = lax.broadcasted_iota(jnp.int32, (c, c), 0)
    col = lax.broadcasted_iota(jnp.int32, (c, c), 1)
    incl = (row >= col) if fwd else (row <= col)
    strict = (row > col) if fwd else (row < col)
    lane = lax.broadcasted_iota(jnp.int32, (c, 2 * DK), 1)
    beta_all, beta_all_t, gc_all, gc_all_t, eg_all, tail_all, eglast_all = gates
    heads = range(H_A)
    kmask = [(lane < DK) if h % 2 else (lane >= DK) for h in heads]
    kv_slab = lambda h: qkv_ref[bb, :, W_A + h * 2 * DK:W_A + (h + 1) * 2 * DK]
    q_slab = lambda h: qkv_ref[bb, :, (h // 2) * 2 * DK:(h // 2 + 1) * 2 * DK]

    decay, kq = [], []
    for h in heads:
        j = d * H_A + h
        kv = kv_slab(h)
        lhs16 = jnp.concatenate([kv * beta_all[:, j:j + 1], q_slab(h)], axis=0).astype(BF16)
        kq.append(_dot_nt(lhs16, jnp.where(kmask[h], kv, 0.0).astype(BF16)))
        diff = gc_all[:, j:j + 1] - gc_all_t[j:j + 1, :]
        decay.append(jnp.where(incl, jnp.exp(jnp.where(incl, diff, 0.0)), 0.0))
    yield

    lows = [jnp.where(strict, kq[h][:c] * decay[h], 0.0) for h in heads]
    attn16 = [(kq[h][c:] * decay[h]).astype(BF16) for h in heads]
    tmat = yield from _unit_tri_inverse_stages(lows, lower=fwd)

    uw, qh = [], []
    for h in heads:
        j = d * H_A + h
        eg = jnp.broadcast_to(eg_all[:, j:j + 1], (c, 2 * DK))
        kv = kv_slab(h)
        t_beta16 = (tmat[h] * beta_all_t[j:j + 1, :]).astype(BF16)
        uw.append(_dot(t_beta16, jnp.where(kmask[h], kv * eg, kv).astype(BF16)))
        qh.append(q_slab(h) * eg)
    yield

    s_old = [s_ref[bb, d, h] for h in heads]
    wq = [_dot(jnp.concatenate([uw[h], qh[h]], axis=0).astype(BF16), s_old[h].astype(BF16))
          for h in heads]
    yield

    v_new16 = [jnp.where(kmask[h], 0.0, uw[h] - wq[h][:c]).astype(BF16) for h in heads]
    o = [wq[h][c:] + _dot(attn16[h], v_new16[h]) for h in heads]
    for h in heads:
        j = d * H_A + h
        k0 = 0 if h % 2 else DK
        kt16 = (kv_slab(h)[:, k0:k0 + DK] * tail_all[:, j:j + 1]).astype(BF16)
        s_ref[bb, d, h, k0:k0 + DK, :] = (s_old[h][k0:k0 + DK] * eglast_all[:, j:j + 1]
                                          + _dot_tn(kt16, v_new16[h]))
    for h in range(0, H_A, 2):
        o_ref[bb, :, h * DV:(h + 2) * DV] = o[h] + o[h + 1]


def _delta_kernel(*refs, zero_init, emit_state):
    (qkv_f_ref, qkv_b_ref, ba_f_ref, ba_b_ref, bat_f_ref, bat_b_ref,
     alog_ref, dtb_ref, alog_t_ref, dtb_t_ref) = refs[:10]
    rest = list(refs[10:])
    s0_ref = None if zero_init else rest.pop(0)
    o_f_ref, o_b_ref = rest.pop(0), rest.pop(0)
    sfin_ref = rest.pop(0) if emit_state else None
    s_ref = rest.pop(0)

    n = pl.program_id(1)
    n_chunks = pl.num_programs(1)
    nb = s_ref.shape[0]
    state_at = lambda h: (0, DV) if h % 2 else (DK, 0)

    @pl.when(n == 0)
    def _():
        s_ref[...] = jnp.zeros(s_ref.shape, F32)
        if not zero_init:
            for bb in range(nb):
                for d in range(2):
                    for h in range(H_A):
                        r0, l0 = state_at(h)
                        s_ref[bb, d, h, r0:r0 + DK, l0:l0 + DV] = s0_ref[bb, d, h]

    c = CHUNK
    row = lax.broadcasted_iota(jnp.int32, (c, c), 0)
    col = lax.broadcasted_iota(jnp.int32, (c, c), 1)
    tril16 = (row >= col).astype(BF16)
    triu16 = (row <= col).astype(BF16)

    programs = []
    for bb in range(nb):
        for d, (qkv_ref, ba_ref, bat_ref, o_ref) in enumerate(
                ((qkv_f_ref, ba_f_ref, bat_f_ref, o_f_ref),
                 (qkv_b_ref, ba_b_ref, bat_b_ref, o_b_ref))):
            fwd = d == 0
            csum16, csum_t16 = (tril16, triu16) if fwd else (triu16, tril16)
            last = c - 1 if fwd else 0
            ba = ba_ref[bb]
            bat = bat_ref[bb, 0]
            beta_all = jax.nn.sigmoid(ba[:, 0:2 * H_A])
            g_all = -jnp.exp(alog_ref[...]) * jax.nn.softplus(ba[:, 2 * H_A:] + dtb_ref[...])
            g_all_t = -jnp.exp(alog_t_ref[...]) * jax.nn.softplus(
                bat[2 * H_A:, :] + dtb_t_ref[...])
            gc_all = sum(_dot(csum16, p) for p in _split3(g_all))
            gc_all_t = sum(_dot(p, csum_t16) for p in _split3(g_all_t))
            glast_all = gc_all[last:last + 1, :]
            beta_all_t = jax.nn.sigmoid(bat[0:2 * H_A, :])
            gates = (beta_all, beta_all_t, gc_all, gc_all_t, jnp.exp(gc_all),
                     jnp.exp(glast_all - gc_all), jnp.exp(glast_all))
            programs.append(_group_stages(qkv_ref, o_ref, s_ref, bb, d, gates))

    _lock_step(programs)

    if emit_state:
        @pl.when(n == n_chunks - 1)
        def _():
            for bb in range(nb):
                for d in range(2):
                    for h in range(H_A):
                        r0, l0 = state_at(h)
                        sfin_ref[bb, d, h] = s_ref[bb, d, h, r0:r0 + DK, l0:l0 + DV]


def _delta_scan(qkv, ba, bat, a_log, dt_bias, s0, *, batch, emit_state):
    n_tok = qkv.shape[0]
    t = n_tok // batch
    nc = t // CHUNK
    nb = DELTA_BATCH
    assert batch % nb == 0
    qkv = qkv.reshape(batch, t, N_QKV)
    ba = ba.reshape(batch, t, 4 * H_A)
    bat = bat.reshape(batch, nc, 4 * H_A, CHUNK)
    zero_init = s0 is None
    alog = a_log.reshape(1, 2 * H_A)
    dtb = dt_bias.reshape(1, 2 * H_A)
    alog_t = a_log.reshape(2 * H_A, 1)
    dtb_t = dt_bias.reshape(2 * H_A, 1)

    fw = lambda b, n: (b, n, 0)
    bw = lambda b, n: (b, nc - 1 - n, 0)
    small = lambda b, n: (0, 0)
    in_specs = [pl.BlockSpec((nb, CHUNK, N_QKV), fw),
                pl.BlockSpec((nb, CHUNK, N_QKV), bw),
                pl.BlockSpec((nb, CHUNK, 4 * H_A), fw),
                pl.BlockSpec((nb, CHUNK, 4 * H_A), bw),
                pl.BlockSpec((nb, 1, 4 * H_A, CHUNK), lambda b, n: (b, n, 0, 0)),
                pl.BlockSpec((nb, 1, 4 * H_A, CHUNK), lambda b, n: (b, nc - 1 - n, 0, 0)),
                pl.BlockSpec((1, 2 * H_A), small),
                pl.BlockSpec((1, 2 * H_A), small),
                pl.BlockSpec((2 * H_A, 1), small),
                pl.BlockSpec((2 * H_A, 1), small)]
    args = [qkv, qkv, ba, ba, bat, bat, alog, dtb, alog_t, dtb_t]
    state_spec = pl.BlockSpec((nb, 2, H_A, DK, DV), lambda b, n: (b, 0, 0, 0, 0))
    if not zero_init:
        in_specs.append(state_spec)
        args.append(s0)
    out_specs = [pl.BlockSpec((nb, CHUNK, W_A), fw), pl.BlockSpec((nb, CHUNK, W_A), bw)]
    out_shape = [jax.ShapeDtypeStruct((batch, t, W_A), F32),
                 jax.ShapeDtypeStruct((batch, t, W_A), F32)]
    if emit_state:
        out_specs.append(state_spec)
        out_shape.append(jax.ShapeDtypeStruct((batch, 2, H_A, DK, DV), F32))
    return pl.pallas_call(
        functools.partial(_delta_kernel, zero_init=zero_init, emit_state=emit_state),
        grid=(batch // nb, nc),
        in_specs=in_specs,
        out_specs=out_specs,
        out_shape=out_shape,
        scratch_shapes=[pltpu.VMEM((nb, 2, H_A, 2 * DK, 2 * DV), F32)],
        compiler_params=pltpu.CompilerParams(dimension_semantics=("arbitrary", "arbitrary"),
                                             vmem_limit_bytes=VMEM_LIMIT),
        name="delta_scan",
    )(*args)


def _mix_kernel(x_ref, mod_ref, gain_ref, of_ref, ob_ref, wz_ref, wsc_ref, wg_ref, conv_ref,
                onorm_ref, wa_ref, wb_ref, wo_ref, out_ref, *, row_len):
    def sub_tile(rows):
        x = x_ref[rows, :]
        h = _rms(x, gain_ref[0:1, :]) * (1.0 + mod_ref[0, 1:2, :]) + mod_ref[0, 0:1, :]
        hb = h.astype(BF16)
        o = of_ref[rows, :] + ob_ref[rows, :]
        slab = 2 * DV
        lo = lax.broadcasted_iota(jnp.int32, (SUB_TILE, slab), 1) < DV
        inv_rms = []
        for s in range(W_A // slab):
            o2 = o[:, s * slab:(s + 1) * slab]
            o2 = o2 * o2
            ms_lo = jnp.sum(jnp.where(lo, o2, 0.0), axis=-1, keepdims=True) * (1.0 / DV)
            ms_hi = jnp.sum(jnp.where(lo, 0.0, o2), axis=-1, keepdims=True) * (1.0 / DV)
            inv_rms.append(jnp.where(lo, lax.rsqrt(ms_lo + EPS), lax.rsqrt(ms_hi + EPS)))
        yield
        z = _dot(hb, wz_ref[...])
        yield
        oa = o * jnp.concatenate(inv_rms, axis=-1) * onorm_ref[...] * _silu(z)
        y_a = _dot(oa.astype(BF16), wa_ref[...])
        sc = _dot(hb, wsc_ref[...])
        yield
        bg, cg, xs = sc[:, 0:W_B], sc[:, W_B:2 * W_B], sc[:, 2 * W_B:3 * W_B]
        yb_in = bg * _conv3_rows(cg * xs, conv_ref, row_len)
        y_b = _dot(yb_in.astype(BF16), wb_ref[...])
        gates = _dot(hb, wg_ref[...])
        yield
        m = (jax.nn.sigmoid(gates[:, 0:D_MODEL]) * y_a
             + jax.nn.sigmoid(gates[:, D_MODEL:]) * y_b)
        y = _dot(m.astype(BF16), wo_ref[...])
        yield
        out_ref[rows, :] = x + mod_ref[0, 2:3, :] * _rms(y, gain_ref[1:2, :])

    _lock_step(sub_tile(slice(r0, r0 + SUB_TILE)) for r0 in range(0, x_ref.shape[0], SUB_TILE))


def _mix(x, mod3, gains, o_f, o_b, w_z, w_sc, w_g, conv_sc, onorm_t, w_a, w_b, w_o,
         *, mod_row_fn, row_len):
    n_tok = x.shape[0]
    tm = WIDE_TILE
    const = lambda i: (0, 0)
    tok = lambda i: (i, 0)
    return pl.pallas_call(
        functools.partial(_mix_kernel, row_len=row_len),
        grid=(n_tok // tm,),
        in_specs=[pl.BlockSpec((tm, D_MODEL), tok),
                  pl.BlockSpec((1, 6, D_MODEL), lambda i: (mod_row_fn(i, tm), 0, 0)),
                  pl.BlockSpec((4, D_MODEL), const),
                  pl.BlockSpec((tm, W_A), tok),
                  pl.BlockSpec((tm, W_A), tok),
                  _resident((D_MODEL, W_A)),
                  _resident((D_MODEL, 3 * W_B)),
                  _resident((D_MODEL, 2 * D_MODEL)),
                  pl.BlockSpec((3, W_B), const),
                  pl.BlockSpec((1, W_A), const),
                  _resident((W_A, D_MODEL)),
                  _resident((W_B, D_MODEL)),
                  _resident((D_MODEL, D_MODEL))],
        out_specs=pl.BlockSpec((tm, D_MODEL), tok),
        out_shape=jax.ShapeDtypeStruct((n_tok, D_MODEL), F32),
        compiler_params=pltpu.CompilerParams(dimension_semantics=("arbitrary",),
                                             vmem_limit_bytes=VMEM_LIMIT),
        name="mix",
    )(x, mod3, gains, o_f, o_b, w_z, w_sc, w_g, conv_sc, onorm_t, w_a, w_b, w_o)


def _ffn_kernel(x_ref, mod_ref, gain_ref, win_ref, wout_ref, out_ref):
    def sub_tile(rows):
        x = x_ref[rows, :]
        h = _rms(x, gain_ref[2:3, :]) * (1.0 + mod_ref[0, 4:5, :]) + mod_ref[0, 3:4, :]
        hb = h.astype(BF16)
        yield
        gate = _dot(hb, win_ref[:, 0:D_FF])
        yield
        up = _dot(hb, win_ref[:, D_FF:])
        yield
        y = _dot((_silu(gate) * up).astype(BF16), wout_ref[...])
        yield
        out_ref[rows, :] = x + mod_ref[0, 5:6, :] * _rms(y, gain_ref[3:4, :])

    _lock_step(sub_tile(slice(r0, r0 + SUB_TILE)) for r0 in range(0, x_ref.shape[0], SUB_TILE))


def _ffn(x, mod3, gains, w_in, w_out, *, mod_row_fn):
    n_tok = x.shape[0]
    tm = WIDE_TILE
    const = lambda i: (0, 0)
    tok = lambda i: (i, 0)
    return pl.pallas_call(
        _ffn_kernel,
        grid=(n_tok // tm,),
        in_specs=[pl.BlockSpec((tm, D_MODEL), tok),
                  pl.BlockSpec((1, 6, D_MODEL), lambda i: (mod_row_fn(i, tm), 0, 0)),
                  pl.BlockSpec((4, D_MODEL), const),
                  _resident((D_MODEL, 2 * D_FF)),
                  _resident((D_FF, D_MODEL))],
        out_specs=pl.BlockSpec((tm, D_MODEL), tok),
        out_shape=jax.ShapeDtypeStruct((n_tok, D_MODEL), F32),
        compiler_params=pltpu.CompilerParams(dimension_semantics=("arbitrary",),
                                             vmem_limit_bytes=VMEM_LIMIT),
        name="ffn",
    )(x, mod3, gains, w_in, w_out)


def _to_slab_order(w):
    rows = w.shape[0]
    pairs = lambda a: a.reshape(rows, H_A // 2, 2, DK)
    q, k, v = (pairs(w[:, i * W_A:(i + 1) * W_A]) for i in range(3))
    q_slabs = q[:, :, ::-1, :].reshape(rows, W_A)
    even = jnp.concatenate([v[:, :, 0], k[:, :, 0]], axis=-1)
    odd = jnp.concatenate([k[:, :, 1], v[:, :, 1]], axis=-1)
    kv_slabs = jnp.stack([even, odd], axis=2).reshape(rows, 2 * W_A)
    return jnp.concatenate([q_slabs, kv_slabs], axis=-1)


def kernel(x_prompt, x_sample, state_delta, c, c_ctx, w_mod, b_mod, norm_gains, w_in, conv_qkv,
           a_log, dt_bias, onorm, conv_sc, w_branch_a, w_branch_b, w_out, w_ffn_in, w_ffn_out):
    depth = w_in.shape[0]
    assert depth == 1
    b_ctx, t_ctx, _ = x_prompt.shape
    b_lat, t_lat, _ = x_sample.shape
    assert SUB_TILE % t_ctx == 0 and SUB_TILE % GRID_W == 0 and t_ctx & (t_ctx - 1) == 0
    assert WIDE_TILE % SUB_TILE == 0 and t_lat % WIDE_TILE == 0
    assert (b_ctx * t_ctx) % WIDE_TILE == 0
    assert b_lat + 1 <= MOD_ROWS
    layer = 0

    cvec = jnp.zeros((MOD_ROWS, D_MODEL), F32).at[0].set(c_ctx).at[1:1 + b_lat].set(c)
    mod3 = _modulation(cvec, w_mod[layer], b_mod[layer][None, :]).reshape(MOD_ROWS, 6, D_MODEL)

    o_qkv, o_z = N_QKV, N_QKV + W_A
    o_beta, o_sc = o_z + 4 * H_A, o_z + 4 * H_A + 3 * W_B
    wi = w_in[layer]
    w_qkv = _to_slab_order(wi[:, :o_qkv]).astype(BF16)
    conv_perm = _to_slab_order(conv_qkv[layer])
    w_z = wi[:, o_qkv:o_z].astype(BF16)
    w_ba = wi[:, o_z:o_beta].astype(BF16)
    w_bat = w_ba.T
    w_sc = wi[:, o_beta:o_sc].astype(BF16)
    w_g = wi[:, o_sc:].astype(BF16)
    w_a = w_branch_a[layer].astype(BF16)
    w_b = w_branch_b[layer].astype(BF16)
    w_o = w_out[layer].astype(BF16)
    w_f1 = w_ffn_in[layer].astype(BF16)
    w_f2 = w_ffn_out[layer].astype(BF16)
    gains = norm_gains[layer]
    onorm_t = jnp.tile(onorm[layer], H_A)[None, :]

    def run_group(x3, s0, mod_row_fn, row_len, emit_state):
        batch, t, _ = x3.shape
        x = x3.reshape(batch * t, D_MODEL)
        qkv, ba, bat = _inproj(x, mod3, gains, w_qkv, w_ba, w_bat, conv_perm,
                               mod_row_fn=mod_row_fn, row_len=row_len)
        res = _delta_scan(qkv, ba, bat, a_log[layer], dt_bias[layer], s0,
                          batch=batch, emit_state=emit_state)
        o_f = res[0].reshape(batch * t, W_A)
        o_b = res[1].reshape(batch * t, W_A)
        x1 = _mix(x, mod3, gains, o_f, o_b, w_z, w_sc, w_g, conv_sc[layer], onorm_t,
                  w_a, w_b, w_o, mod_row_fn=mod_row_fn, row_len=row_len)
        x2 = _ffn(x1, mod3, gains, w_f1, w_f2, mod_row_fn=mod_row_fn)
        return x2.reshape(batch, t, D_MODEL), (res[2] if emit_state else None)

    y_p, s_fin = run_group(x_prompt, None, lambda i, tm: 0, t_ctx, True)
    y_s, _ = run_group(x_sample, state_delta[:, layer], lambda i, tm: 1 + (i * tm) // t_lat,
                       GRID_W, False)
    new_state = s_fin[:, None].astype(x_prompt.dtype)
    return (y_p, y_s, new_state)
```

```python
import functools

import jax
import jax.numpy as jnp
from jax import lax
from jax.experimental import pallas as pl
from jax.experimental.pallas import tpu as pltpu

D_MODEL = 1024
H_A = 8
DK = 64
DV = 64
W_A = H_A * DV
W_B = 512
CHUNK = 64
GRID_W = 64
EPS = 1e-6
N_QKV = 3 * H_A * DK
D_FF = 2816
SUBLANES = 8
MOD_ROWS = SUBLANES
WIDE_TILE = 512
SUB_TILE = 256
DELTA_BATCH = 4
VMEM_LIMIT = 56 * 1024 * 1024

F32 = jnp.float32
BF16 = jnp.bfloat16
HIGHEST = lax.Precision.HIGHEST


def _dot(a, b, precision=None):
    return jnp.dot(a, b, preferred_element_type=F32, precision=precision)


def _dot_nt(a, b, precision=None):
    return lax.dot_general(a, b, (((1,), (1,)), ((), ())), preferred_element_type=F32,
                           precision=precision)


def _dot_tn(a, b, precision=None):
    return lax.dot_general(a, b, (((0,), (0,)), ((), ())), preferred_element_type=F32,
                           precision=precision)


def _resident(shape):
    return pl.BlockSpec(shape, lambda i: (0,) * len(shape), pipeline_mode=pl.Buffered(1))


def _lock_step(programs):
    programs = list(programs)
    while programs:
        for prog in list(programs):
            try:
                next(prog)
            except StopIteration:
                programs.remove(prog)


def _silu(x):
    return x * jax.nn.sigmoid(x)


def _rms(x, gain):
    return x * lax.rsqrt(jnp.mean(x * x, axis=-1, keepdims=True) + EPS) * gain


def _conv3_rows(x, w_ref, row_len):
    n = x.shape[0]
    pos = lax.broadcasted_iota(jnp.int32, (n, 1), 0) & (row_len - 1)
    prev = jnp.where(pos == 0, 0.0, pltpu.roll(x, 1, 0))
    nxt = jnp.where(pos == row_len - 1, 0.0, pltpu.roll(x, n - 1, 0))
    return prev * w_ref[0:1, :] + x * w_ref[1:2, :] + nxt * w_ref[2:3, :]


def _mod_kernel(c_ref, w_ref, b_ref, o_ref):
    rows = c_ref.shape[0]
    s3 = jnp.concatenate([p.astype(F32) for p in _split3(_silu(c_ref[...]))], axis=0).astype(BF16)
    w = w_ref[...]
    w_hi = w.astype(BF16)
    w_lo = (w - w_hi.astype(F32)).astype(BF16)
    acc = _dot(s3, w_hi) + _dot(s3, w_lo)
    o_ref[...] = acc[0:rows] + acc[rows:2 * rows] + acc[2 * rows:] + b_ref[...]


def _modulation(cvec, w_mod, b_mod):
    n_out = w_mod.shape[1]
    blk = D_MODEL
    return pl.pallas_call(
        _mod_kernel,
        grid=(n_out // blk,),
        in_specs=[pl.BlockSpec((MOD_ROWS, D_MODEL), lambda j: (0, 0)),
                  pl.BlockSpec((D_MODEL, blk), lambda j: (0, j)),
                  pl.BlockSpec((1, blk), lambda j: (0, j))],
        out_specs=pl.BlockSpec((MOD_ROWS, blk), lambda j: (0, j)),
        out_shape=jax.ShapeDtypeStruct((MOD_ROWS, n_out), F32),
        name="modulation",
    )(cvec, w_mod, b_mod)


def _inproj_kernel(x_ref, mod_ref, gain_ref, wqkv_ref, wba_ref, wbat_ref, conv_ref,
                   qkv_ref, ba_ref, bat_ref, *, row_len):
    slab = 2 * DK
    lo = lax.broadcasted_iota(jnp.int32, (SUB_TILE, slab), 1) < DK
    ssq = lambda v, m: jnp.sum(jnp.where(m, v * v, 0.0), axis=-1, keepdims=True)

    def sub_tile(r0):
        rows = slice(r0, r0 + SUB_TILE)
        x = x_ref[rows, :]
        h = _rms(x, gain_ref[0:1, :]) * (1.0 + mod_ref[0, 1:2, :]) + mod_ref[0, 0:1, :]
        hb = h.astype(BF16)
        yield
        for pair in range(N_QKV // (2 * slab)):
            cols2 = slice(pair * 2 * slab, (pair + 1) * 2 * slab)
            v2 = _silu(_conv3_rows(_dot(hb, wqkv_ref[:, cols2]), conv_ref.at[:, cols2], row_len))
            for half in range(2):
                s = 2 * pair + half
                cols = slice(s * slab, (s + 1) * slab)
                v = v2[:, half * slab:(half + 1) * slab]
                if s < W_A // slab:
                    r = jnp.where(lo, lax.rsqrt(ssq(v, lo) + EPS) * (DK ** -0.5),
                                  lax.rsqrt(ssq(v, ~lo) + EPS) * (DK ** -0.5))
                    qkv_ref[rows, cols] = v * r
                else:
                    kmask = lo if (s - W_A // slab) % 2 else ~lo
                    qkv_ref[rows, cols] = v * jnp.where(kmask, lax.rsqrt(ssq(v, kmask) + EPS), 1.0)
            yield
        ba_ref[rows, :] = _dot(hb, wba_ref[...])
        for ci in range(SUB_TILE // CHUNK):
            bat_ref[r0 // CHUNK + ci] = _dot_nt(wbat_ref[...], hb[ci * CHUNK:(ci + 1) * CHUNK, :])

    _lock_step(sub_tile(r0) for r0 in range(0, x_ref.shape[0], SUB_TILE))


def _inproj(x, mod3, gains, w_qkv, w_ba, w_bat, conv_qkv, *, mod_row_fn, row_len):
    n_tok = x.shape[0]
    tm = WIDE_TILE
    const = lambda i: (0, 0)
    return pl.pallas_call(
        functools.partial(_inproj_kernel, row_len=row_len),
        grid=(n_tok // tm,),
        in_specs=[pl.BlockSpec((tm, D_MODEL), lambda i: (i, 0)),
                  pl.BlockSpec((1, 6, D_MODEL), lambda i: (mod_row_fn(i, tm), 0, 0)),
                  pl.BlockSpec((4, D_MODEL), const),
                  pl.BlockSpec((D_MODEL, N_QKV), const),
                  pl.BlockSpec((D_MODEL, 4 * H_A), const),
                  pl.BlockSpec((4 * H_A, D_MODEL), const),
                  pl.BlockSpec((3, N_QKV), const)],
        out_specs=[pl.BlockSpec((tm, N_QKV), lambda i: (i, 0)),
                   pl.BlockSpec((tm, 4 * H_A), lambda i: (i, 0)),
                   pl.BlockSpec((tm // CHUNK, 4 * H_A, CHUNK), lambda i: (i, 0, 0))],
        out_shape=[jax.ShapeDtypeStruct((n_tok, N_QKV), F32),
                   jax.ShapeDtypeStruct((n_tok, 4 * H_A), F32),
                   jax.ShapeDtypeStruct((n_tok // CHUNK, 4 * H_A, CHUNK), F32)],
        compiler_params=pltpu.CompilerParams(dimension_semantics=("arbitrary",),
                                             vmem_limit_bytes=VMEM_LIMIT),
        name="inproj",
    )(x, mod3, gains, w_qkv, w_ba, w_bat, conv_qkv)


def _unit_tri_inverse_stages(lows, lower):
    c = CHUNK
    ri = lax.broadcasted_iota(jnp.int32, (c, c), 0)
    ci = lax.broadcasted_iota(jnp.int32, (c, c), 1)
    eye = (ri == ci).astype(F32)
    rng = range(len(lows))
    joins = lambda b: ((ri // (2 * b)) == (ci // (2 * b))) & ((ri // b) != (ci // b))
    t = [eye - jnp.where(joins(1), lows[i], 0.0) for i in rng]
    b = 2
    while b < c:
        mask = joins(b)
        t16 = [t[i].astype(BF16) for i in rng]
        e16 = [jnp.where(mask, lows[i], 0.0).astype(BF16) for i in rng]
        if b % SUBLANES:
            x = [_dot(t16[i], e16[i]) for i in rng]
            yield
            t = [t[i] - _dot(x[i].astype(BF16), t16[i]) for i in rng]
            yield
        else:
            first = [slice(s, s + b) for s in range(0, c, 2 * b)]
            second = [slice(s + b, s + 2 * b) for s in range(0, c, 2 * b)]
            keep, upd = (first, second) if lower else (second, first)
            t_upd = [jnp.concatenate([t[i][r] for r in upd], axis=0) for i in rng]
            x = [_dot(t_upd[i].astype(BF16), e16[i]) for i in rng]
            yield
            t_upd = [t_upd[i] - _dot(x[i].astype(BF16), t16[i]) for i in rng]
            yield
            pair = lambda kept, new: (kept, new) if lower else (new, kept)
            t = [jnp.concatenate(
                [piece for k, r in enumerate(keep)
                 for piece in pair(t[i][r], t_upd[i][k * b:(k + 1) * b])], axis=0) for i in rng]
        b *= 2
    return t


def _split3(x):
    pieces = []
    for _ in range(3):
        p = x.astype(BF16)
        x = x - p.astype(F32)
        pieces.append(p)
    return pieces


def _group_stages(qkv_ref, o_ref, s_ref, bb, d, gates):
    c = CHUNK
    fwd = d == 0
    row = lax.broadcasted_iota(jnp.int32, (c, c), 0)
    col = lax.broadcasted_iota(jnp.int32, (c, c), 1)
    incl = (row >= col) if fwd else (row <= col)
    strict = (row > col) if fwd else (row < col)
    lane = lax.broadcasted_iota(jnp.int32, (c, 2 * DK), 1)
    beta_all, beta_all_t, gc_all, gc_all_t, eg_all, tail_all, eglast_all = gates
    heads = range(H_A)
    kmask = [(lane < DK) if h % 2 else (lane >= DK) for h in heads]
    kv_slab = lambda h: qkv_ref[bb, :, W_A + h * 2 * DK:W_A + (h + 1) * 2 * DK]
    q_slab = lambda h: qkv_ref[bb, :, (h // 2) * 2 * DK:(h // 2 + 1) * 2 * DK]

    decay, kq = [], []
    for h in heads:
        j = d * H_A + h
        kv = kv_slab(h)
        lhs16 = jnp.concatenate([kv * beta_all[:, j:j + 1], q_slab(h)], axis=0).astype(BF16)
        kq.append(_dot_nt(lhs16, jnp.where(kmask[h], kv, 0.0).astype(BF16)))
        diff = gc_all[:, j:j + 1] - gc_all_t[j:j + 1, :]
        decay.append(jnp.where(incl, jnp.exp(jnp.where(incl, diff, 0.0)), 0.0))
    yield

    lows = [jnp.where(strict, kq[h][:c] * decay[h], 0.0) for h in heads]
    attn16 = [(kq[h][c:] * decay[h]).astype(BF16) for h in heads]
    tmat = yield from _unit_tri_inverse_stages(lows, lower=fwd)

    uw, qh = [], []
    for h in heads:
        j = d * H_A + h
        eg = jnp.broadcast_to(eg_all[:, j:j + 1], (c, 2 * DK))
        kv = kv_slab(h)
        t_beta16 = (tmat[h] * beta_all_t[j:j + 1, :]).astype(BF16)
        uw.append(_dot(t_beta16, jnp.where(kmask[h], kv * eg, kv).astype(BF16)))
        qh.append(q_slab(h) * eg)
    yield

    s_old = [s_ref[bb, d, h] for h in heads]
    wq = [_dot(jnp.concatenate([uw[h], qh[h]], axis=0).astype(BF16), s_old[h].astype(BF16))
          for h in heads]
    yield

    v_new16 = [jnp.where(kmask[h], 0.0, uw[h] - wq[h][:c]).astype(BF16) for h in heads]
    o = [wq[h][c:] + _dot(attn16[h], v_new16[h]) for h in heads]
    for h in heads:
        j = d * H_A + h
        k0 = 0 if h % 2 else DK
        kt16 = (kv_slab(h)[:, k0:k0 + DK] * tail_all[:, j:j + 1]).astype(BF16)
        s_ref[bb, d, h, k0:k0 + DK, :] = (s_old[h][k0:k0 + DK] * eglast_all[:, j:j + 1]
                                          + _dot_tn(kt16, v_new16[h]))
    for h in range(0, H_A, 2):
        o_ref[bb, :, h * DV:(h + 2) * DV] = o[h] + o[h + 1]


def _delta_kernel(*refs, zero_init, emit_state):
    (qkv_f_ref, qkv_b_ref, ba_f_ref, ba_b_ref, bat_f_ref, bat_b_ref,
     alog_ref, dtb_ref, alog_t_ref, dtb_t_ref) = refs[:10]
    rest = list(refs[10:])
    s0_ref = None if zero_init else rest.pop(0)
    o_f_ref, o_b_ref = rest.pop(0), rest.pop(0)
    sfin_ref = rest.pop(0) if emit_state else None
    s_ref = rest.pop(0)

    n = pl.program_id(1)
    n_chunks = pl.num_programs(1)
    nb = s_ref.shape[0]
    state_at = lambda h: (0, DV) if h % 2 else (DK, 0)

    @pl.when(n == 0)
    def _():
        s_ref[...] = jnp.zeros(s_ref.shape, F32)
        if not zero_init:
            for bb in range(nb):
                for d in range(2):
                    for h in range(H_A):
                        r0, l0 = state_at(h)
                        s_ref[bb, d, h, r0:r0 + DK, l0:l0 + DV] = s0_ref[bb, d, h]

    c = CHUNK
    row = lax.broadcasted_iota(jnp.int32, (c, c), 0)
    col = lax.broadcasted_iota(jnp.int32, (c, c), 1)
    tril16 = (row >= col).astype(BF16)
    triu16 = (row <= col).astype(BF16)

    programs = []
    for bb in range(nb):
        for d, (qkv_ref, ba_ref, bat_ref, o_ref) in enumerate(
                ((qkv_f_ref, ba_f_ref, bat_f_ref, o_f_ref),
                 (qkv_b_ref, ba_b_ref, bat_b_ref, o_b_ref))):
            fwd = d == 0
            csum16, csum_t16 = (tril16, triu16) if fwd else (triu16, tril16)
            last = c - 1 if fwd else 0
            ba = ba_ref[bb]
            bat = bat_ref[bb, 0]
            beta_all = jax.nn.sigmoid(ba[:, 0:2 * H_A])
            g_all = -jnp.exp(alog_ref[...]) * jax.nn.softplus(ba[:, 2 * H_A:] + dtb_ref[...])
            g_all_t = -jnp.exp(alog_t_ref[...]) * jax.nn.softplus(
                bat[2 * H_A:, :] + dtb_t_ref[...])
            gc_all = sum(_dot(csum16, p) for p in _split3(g_all))
            gc_all_t = sum(_dot(p, csum_t16) for p in _split3(g_all_t))
            glast_all = gc_all[last:last + 1, :]
            beta_all_t = jax.nn.sigmoid(bat[0:2 * H_A, :])
            gates = (beta_all, beta_all_t, gc_all, gc_all_t, jnp.exp(gc_all),
                     jnp.exp(glast_all - gc_all), jnp.exp(glast_all))
            programs.append(_group_stages(qkv_ref, o_ref, s_ref, bb, d, gates))

    _lock_step(programs)

    if emit_state:
        @pl.when(n == n_chunks - 1)
        def _():
            for bb in range(nb):
                for d in range(2):
                    for h in range(H_A):
                        r0, l0 = state_at(h)
                        sfin_ref[bb, d, h] = s_ref[bb, d, h, r0:r0 + DK, l0:l0 + DV]


def _delta_scan(qkv, ba, bat, a_log, dt_bias, s0, *, batch, emit_state):
    n_tok = qkv.shape[0]
    t = n_tok // batch
    nc = t // CHUNK
    nb = DELTA_BATCH
    assert batch % nb == 0
    qkv = qkv.reshape(batch, t, N_QKV)
    ba = ba.reshape(batch, t, 4 * H_A)
    bat = bat.reshape(batch, nc, 4 * H_A, CHUNK)
    zero_init = s0 is None
    alog = a_log.reshape(1, 2 * H_A)
    dtb = dt_bias.reshape(1, 2 * H_A)
    alog_t = a_log.reshape(2 * H_A, 1)
    dtb_t = dt_bias.reshape(2 * H_A, 1)

    fw = lambda b, n: (b, n, 0)
    bw = lambda b, n: (b, nc - 1 - n, 0)
    small = lambda b, n: (0, 0)
    in_specs = [pl.BlockSpec((nb, CHUNK, N_QKV), fw),
                pl.BlockSpec((nb, CHUNK, N_QKV), bw),
                pl.BlockSpec((nb, CHUNK, 4 * H_A), fw),
                pl.BlockSpec((nb, CHUNK, 4 * H_A), bw),
                pl.BlockSpec((nb, 1, 4 * H_A, CHUNK), lambda b, n: (b, n, 0, 0)),
                pl.BlockSpec((nb, 1, 4 * H_A, CHUNK), lambda b, n: (b, nc - 1 - n, 0, 0)),
                pl.BlockSpec((1, 2 * H_A), small),
                pl.BlockSpec((1, 2 * H_A), small),
                pl.BlockSpec((2 * H_A, 1), small),
                pl.BlockSpec((2 * H_A, 1), small)]
    args = [qkv, qkv, ba, ba, bat, bat, alog, dtb, alog_t, dtb_t]
    state_spec = pl.BlockSpec((nb, 2, H_A, DK, DV), lambda b, n: (b, 0, 0, 0, 0))
    if not zero_init:
        in_specs.append(state_spec)
        args.append(s0)
    out_specs = [pl.BlockSpec((nb, CHUNK, W_A), fw), pl.BlockSpec((nb, CHUNK, W_A), bw)]
    out_shape = [jax.ShapeDtypeStruct((batch, t, W_A), F32),
                 jax.ShapeDtypeStruct((batch, t, W_A), F32)]
    if emit_state:
        out_specs.append(state_spec)
        out_shape.append(jax.ShapeDtypeStruct((batch, 2, H_A, DK, DV), F32))
    return pl.pallas_call(
        functools.partial(_delta_kernel, zero_init=zero_init, emit_state=emit_state),
        grid=(batch // nb, nc),
        in_specs=in_specs,
        out_specs=out_specs,
        out_shape=out_shape,
        scratch_shapes=[pltpu.VMEM((nb, 2, H_A, 2 * DK, 2 * DV), F32)],
        compiler_params=pltpu.CompilerParams(dimension_semantics=("arbitrary", "arbitrary"),
                                             vmem_limit_bytes=VMEM_LIMIT),
        name="delta_scan",
    )(*args)


def _mix_kernel(x_ref, mod_ref, gain_ref, of_ref, ob_ref, wz_ref, wsc_ref, wg_ref, conv_ref,
                onorm_ref, wa_ref, wb_ref, wo_ref, out_ref, *, row_len):
    def sub_tile(rows):
        x = x_ref[rows, :]
        h = _rms(x, gain_ref[0:1, :]) * (1.0 + mod_ref[0, 1:2, :]) + mod_ref[0, 0:1, :]
        hb = h.astype(BF16)
        o = of_ref[rows, :] + ob_ref[rows, :]
        slab = 2 * DV
        lo = lax.broadcasted_iota(jnp.int32, (SUB_TILE, slab), 1) < DV
        inv_rms = []
        for s in range(W_A // slab):
            o2 = o[:, s * slab:(s + 1) * slab]
            o2 = o2 * o2
            ms_lo = jnp.sum(jnp.where(lo, o2, 0.0), axis=-1, keepdims=True) * (1.0 / DV)
            ms_hi = jnp.sum(jnp.where(lo, 0.0, o2), axis=-1, keepdims=True) * (1.0 / DV)
            inv_rms.append(jnp.where(lo, lax.rsqrt(ms_lo + EPS), lax.rsqrt(ms_hi + EPS)))
        yield
        z = _dot(hb, wz_ref[...])
        yield
        oa = o * jnp.concatenate(inv_rms, axis=-1) * onorm_ref[...] * _silu(z)
        y_a = _dot(oa.astype(BF16), wa_ref[...])
        sc = _dot(hb, wsc_ref[...])
        yield
        bg, cg, xs = sc[:, 0:W_B], sc[:, W_B:2 * W_B], sc[:, 2 * W_B:3 * W_B]
        yb_in = bg * _conv3_rows(cg * xs, conv_ref, row_len)
        y_b = _dot(yb_in.astype(BF16), wb_ref[...])
        gates = _dot(hb, wg_ref[...])
        yield
        m = (jax.nn.sigmoid(gates[:, 0:D_MODEL]) * y_a
             + jax.nn.sigmoid(gates[:, D_MODEL:]) * y_b)
        y = _dot(m.astype(BF16), wo_ref[...])
        yield
        out_ref[rows, :] = x + mod_ref[0, 2:3, :] * _rms(y, gain_ref[1:2, :])

    _lock_step(sub_tile(slice(r0, r0 + SUB_TILE)) for r0 in range(0, x_ref.shape[0], SUB_TILE))


def _mix(x, mod3, gains, o_f, o_b, w_z, w_sc, w_g, conv_sc, onorm_t, w_a, w_b, w_o,
         *, mod_row_fn, row_len):
    n_tok = x.shape[0]
    tm = WIDE_TILE
    const = lambda i: (0, 0)
    tok = lambda i: (i, 0)
    return pl.pallas_call(
        functools.partial(_mix_kernel, row_len=row_len),
        grid=(n_tok // tm,),
        in_specs=[pl.BlockSpec((tm, D_MODEL), tok),
                  pl.BlockSpec((1, 6, D_MODEL), lambda i: (mod_row_fn(i, tm), 0, 0)),
                  pl.BlockSpec((4, D_MODEL), const),
                  pl.BlockSpec((tm, W_A), tok),
                  pl.BlockSpec((tm, W_A), tok),
                  _resident((D_MODEL, W_A)),
                  _resident((D_MODEL, 3 * W_B)),
                  _resident((D_MODEL, 2 * D_MODEL)),
                  pl.BlockSpec((3, W_B), const),
                  pl.BlockSpec((1, W_A), const),
                  _resident((W_A, D_MODEL)),
                  _resident((W_B, D_MODEL)),
                  _resident((D_MODEL, D_MODEL))],
        out_specs=pl.BlockSpec((tm, D_MODEL), tok),
        out_shape=jax.ShapeDtypeStruct((n_tok, D_MODEL), F32),
        compiler_params=pltpu.CompilerParams(dimension_semantics=("arbitrary",),
                                             vmem_limit_bytes=VMEM_LIMIT),
        name="mix",
    )(x, mod3, gains, o_f, o_b, w_z, w_sc, w_g, conv_sc, onorm_t, w_a, w_b, w_o)


def _ffn_kernel(x_ref, mod_ref, gain_ref, win_ref, wout_ref, out_ref):
    def sub_tile(rows):
        x = x_ref[rows, :]
        h = _rms(x, gain_ref[2:3, :]) * (1.0 + mod_ref[0, 4:5, :]) + mod_ref[0, 3:4, :]
        hb = h.astype(BF16)
        yield
        gate = _dot(hb, win_ref[:, 0:D_FF])
        yield
        up = _dot(hb, win_ref[:, D_FF:])
        yield
        y = _dot((_silu(gate) * up).astype(BF16), wout_ref[...])
        yield
        out_ref[rows, :] = x + mod_ref[0, 5:6, :] * _rms(y, gain_ref[3:4, :])

    _lock_step(sub_tile(slice(r0, r0 + SUB_TILE)) for r0 in range(0, x_ref.shape[0], SUB_TILE))


def _ffn(x, mod3, gains, w_in, w_out, *, mod_row_fn):
    n_tok = x.shape[0]
    tm = WIDE_TILE
    const = lambda i: (0, 0)
    tok = lambda i: (i, 0)
    return pl.pallas_call(
        _ffn_kernel,
        grid=(n_tok // tm,),
        in_specs=[pl.BlockSpec((tm, D_MODEL), tok),
                  pl.BlockSpec((1, 6, D_MODEL), lambda i: (mod_row_fn(i, tm), 0, 0)),
                  pl.BlockSpec((4, D_MODEL), const),
                  _resident((D_MODEL, 2 * D_FF)),
                  _resident((D_FF, D_MODEL))],
        out_specs=pl.BlockSpec((tm, D_MODEL), tok),
        out_shape=jax.ShapeDtypeStruct((n_tok, D_MODEL), F32),
        compiler_params=pltpu.CompilerParams(dimension_semantics=("arbitrary",),
                                             vmem_limit_bytes=VMEM_LIMIT),
        name="ffn",
    )(x, mod3, gains, w_in, w_out)


def _split_w_in_kernel(w_ref, qkv_ref, z_ref, ba_ref, sc_ref, g_ref):
    slab = 2 * DK
    left = lax.broadcasted_iota(jnp.int32, (w_ref.shape[0], slab), 1) < DK
    for p in range(H_A // 2):
        q_p = w_ref[:, p * slab:(p + 1) * slab]
        k_p = w_ref[:, W_A + p * slab:W_A + (p + 1) * slab]
        v_p = w_ref[:, 2 * W_A + p * slab:2 * W_A + (p + 1) * slab]
        k_swapped = pltpu.roll(k_p, DK, 1)
        qkv_ref[:, p * slab:(p + 1) * slab] = pltpu.roll(q_p, DK, 1).astype(BF16)
        even = W_A + 2 * p * slab
        qkv_ref[:, even:even + slab] = jnp.where(left, v_p, k_swapped).astype(BF16)
        qkv_ref[:, even + slab:even + 2 * slab] = jnp.where(left, k_swapped, v_p).astype(BF16)
    o_z, o_ba = N_QKV, N_QKV + W_A
    o_sc = o_ba + 4 * H_A
    o_g = o_sc + 3 * W_B
    z_ref[...] = w_ref[:, o_z:o_ba].astype(BF16)
    ba_ref[...] = w_ref[:, o_ba:o_sc].astype(BF16)
    sc_ref[...] = w_ref[:, o_sc:o_g].astype(BF16)
    g_ref[...] = w_ref[:, o_g:].astype(BF16)


def _split_w_in(w):
    rows, n_in = w.shape
    blk = 128
    widths = (N_QKV, W_A, 4 * H_A, 3 * W_B, 2 * D_MODEL)
    assert sum(widths) == n_in and rows % blk == 0
    return pl.pallas_call(
        _split_w_in_kernel,
        grid=(rows // blk,),
        in_specs=[pl.BlockSpec((blk, n_in), lambda i: (i, 0))],
        out_specs=[pl.BlockSpec((blk, n), lambda i: (i, 0)) for n in widths],
        out_shape=[jax.ShapeDtypeStruct((rows, n), BF16) for n in widths],
        name="split_w_in",
    )(w)


def _to_slab_order(w):
    rows = w.shape[0]
    pairs = lambda a: a.reshape(rows, H_A // 2, 2, DK)
    q, k, v = (pairs(w[:, i * W_A:(i + 1) * W_A]) for i in range(3))
    q_slabs = q[:, :, ::-1, :].reshape(rows, W_A)
    even = jnp.concatenate([v[:, :, 0], k[:, :, 0]], axis=-1)
    odd = jnp.concatenate([k[:, :, 1], v[:, :, 1]], axis=-1)
    kv_slabs = jnp.stack([even, odd], axis=2).reshape(rows, 2 * W_A)
    return jnp.concatenate([q_slabs, kv_slabs], axis=-1)


def kernel(x_prompt, x_sample, state_delta, c, c_ctx, w_mod, b_mod, norm_gains, w_in, conv_qkv,
           a_log, dt_bias, onorm, conv_sc, w_branch_a, w_branch_b, w_out, w_ffn_in, w_ffn_out):
    depth = w_in.shape[0]
    assert depth == 1
    b_ctx, t_ctx, _ = x_prompt.shape
    b_lat, t_lat, _ = x_sample.shape
    assert SUB_TILE % t_ctx == 0 and SUB_TILE % GRID_W == 0 and t_ctx & (t_ctx - 1) == 0
    assert WIDE_TILE % SUB_TILE == 0 and t_lat % WIDE_TILE == 0
    assert (b_ctx * t_ctx) % WIDE_TILE == 0
    assert b_lat + 1 <= MOD_ROWS
    layer = 0

    cvec = jnp.zeros((MOD_ROWS, D_MODEL), F32).at[0].set(c_ctx).at[1:1 + b_lat].set(c)
    mod3 = _modulation(cvec, w_mod[layer], b_mod[layer][None, :]).reshape(MOD_ROWS, 6, D_MODEL)

    w_qkv, w_z, w_ba, w_sc, w_g = _split_w_in(w_in[layer])
    conv_perm = _to_slab_order(conv_qkv[layer])
    w_bat = w_ba.T
    w_a = w_branch_a[layer].astype(BF16)
    w_b = w_branch_b[layer].astype(BF16)
    w_o = w_out[layer].astype(BF16)
    w_f1 = w_ffn_in[layer].astype(BF16)
    w_f2 = w_ffn_out[layer].astype(BF16)
    gains = norm_gains[layer]
    onorm_t = jnp.tile(onorm[layer], H_A)[None, :]

    def run_group(x3, s0, mod_row_fn, row_len, emit_state):
        batch, t, _ = x3.shape
        x = x3.reshape(batch * t, D_MODEL)
        qkv, ba, bat = _inproj(x, mod3, gains, w_qkv, w_ba, w_bat, conv_perm,
                               mod_row_fn=mod_row_fn, row_len=row_len)
        res = _delta_scan(qkv, ba, bat, a_log[layer], dt_bias[layer], s0,
                          batch=batch, emit_state=emit_state)
        o_f = res[0].reshape(batch * t, W_A)
        o_b = res[1].reshape(batch * t, W_A)
        x1 = _mix(x, mod3, gains, o_f, o_b, w_z, w_sc, w_g, conv_sc[layer], onorm_t,
                  w_a, w_b, w_o, mod_row_fn=mod_row_fn, row_len=row_len)
        x2 = _ffn(x1, mod3, gains, w_f1, w_f2, mod_row_fn=mod_row_fn)
        return x2.reshape(batch, t, D_MODEL), (res[2] if emit_state else None)

    y_p, s_fin = run_group(x_prompt, None, lambda i, tm: 0, t_ctx, True)
    y_s, _ = run_group(x_sample, state_delta[:, layer], lambda i, tm: 1 + (i * tm) // t_lat,
                       GRID_W, False)
    new_state = s_fin[:, None].astype(x_prompt.dtype)
    return (y_p, y_s, new_state)
```

```python
import functools

import jax
import jax.numpy as jnp
from jax import lax
from jax.experimental import pallas as pl
from jax.experimental.pallas import tpu as pltpu

D_MODEL = 1024
H_A = 8
DK = 64
DV = 64
W_A = H_A * DV
W_B = 512
CHUNK = 64
GRID_W = 64
EPS = 1e-6
N_QKV = 3 * H_A * DK
D_FF = 2816
SUBLANES = 8
MOD_ROWS = SUBLANES
WIDE_TILE = 512
SUB_TILE = 256
DELTA_BATCH = 4
VMEM_LIMIT = 56 * 1024 * 1024

F32 = jnp.float32
BF16 = jnp.bfloat16
HIGHEST = lax.Precision.HIGHEST


def _dot(a, b, precision=None):
    return jnp.dot(a, b, preferred_element_type=F32, precision=precision)


def _dot_nt(a, b, precision=None):
    return lax.dot_general(a, b, (((1,), (1,)), ((), ())), preferred_element_type=F32,
                           precision=precision)


def _dot_tn(a, b, precision=None):
    return lax.dot_general(a, b, (((0,), (0,)), ((), ())), preferred_element_type=F32,
                           precision=precision)


def _resident(shape):
    return pl.BlockSpec(shape, lambda i: (0,) * len(shape), pipeline_mode=pl.Buffered(1))


def _lock_step(programs):
    programs = list(programs)
    while programs:
        for prog in list(programs):
            try:
                next(prog)
            except StopIteration:
                programs.remove(prog)


def _silu(x):
    return x * jax.nn.sigmoid(x)


def _rms(x, gain):
    return x * lax.rsqrt(jnp.mean(x * x, axis=-1, keepdims=True) + EPS) * gain


def _conv3_rows(x, w_ref, row_len):
    n = x.shape[0]
    pos = lax.broadcasted_iota(jnp.int32, (n, 1), 0) & (row_len - 1)
    prev = jnp.where(pos == 0, 0.0, pltpu.roll(x, 1, 0))
    nxt = jnp.where(pos == row_len - 1, 0.0, pltpu.roll(x, n - 1, 0))
    return prev * w_ref[0:1, :] + x * w_ref[1:2, :] + nxt * w_ref[2:3, :]


def _mod_kernel(c_ref, w_ref, b_ref, o_ref):
    rows = c_ref.shape[0]
    s3 = jnp.concatenate([p.astype(F32) for p in _split3(_silu(c_ref[...]))], axis=0).astype(BF16)
    w = w_ref[...]
    w_hi = w.astype(BF16)
    w_lo = (w - w_hi.astype(F32)).astype(BF16)
    acc = _dot(s3, w_hi) + _dot(s3, w_lo)
    o_ref[...] = acc[0:rows] + acc[rows:2 * rows] + acc[2 * rows:] + b_ref[...]


def _modulation(cvec, w_mod, b_mod):
    n_out = w_mod.shape[1]
    blk = D_MODEL
    return pl.pallas_call(
        _mod_kernel,
        grid=(n_out // blk,),
        in_specs=[pl.BlockSpec((MOD_ROWS, D_MODEL), lambda j: (0, 0)),
                  pl.BlockSpec((D_MODEL, blk), lambda j: (0, j)),
                  pl.BlockSpec((1, blk), lambda j: (0, j))],
        out_specs=pl.BlockSpec((MOD_ROWS, blk), lambda j: (0, j)),
        out_shape=jax.ShapeDtypeStruct((MOD_ROWS, n_out), F32),
        name="modulation",
    )(cvec, w_mod, b_mod)


def _inproj_kernel(x_ref, mod_ref, gain_ref, wqkv_ref, wba_ref, wbat_ref, conv_ref,
                   qkv_ref, ba_ref, bat_ref, *, row_len):
    slab = 2 * DK
    lo = lax.broadcasted_iota(jnp.int32, (SUB_TILE, slab), 1) < DK
    ssq = lambda v, m: jnp.sum(jnp.where(m, v * v, 0.0), axis=-1, keepdims=True)

    def sub_tile(r0):
        rows = slice(r0, r0 + SUB_TILE)
        x = x_ref[rows, :]
        h = _rms(x, gain_ref[0:1, :]) * (1.0 + mod_ref[0, 1:2, :]) + mod_ref[0, 0:1, :]
        hb = h.astype(BF16)
        yield
        for pair in range(N_QKV // (2 * slab)):
            cols2 = slice(pair * 2 * slab, (pair + 1) * 2 * slab)
            v2 = _silu(_conv3_rows(_dot(hb, wqkv_ref[:, cols2]), conv_ref.at[:, cols2], row_len))
            for half in range(2):
                s = 2 * pair + half
                cols = slice(s * slab, (s + 1) * slab)
                v = v2[:, half * slab:(half + 1) * slab]
                if s < W_A // slab:
                    r = jnp.where(lo, lax.rsqrt(ssq(v, lo) + EPS) * (DK ** -0.5),
                                  lax.rsqrt(ssq(v, ~lo) + EPS) * (DK ** -0.5))
                    qkv_ref[rows, cols] = v * r
                else:
                    kmask = lo if (s - W_A // slab) % 2 else ~lo
                    qkv_ref[rows, cols] = v * jnp.where(kmask, lax.rsqrt(ssq(v, kmask) + EPS), 1.0)
            yield
        ba_ref[rows, :] = _dot(hb, wba_ref[...])
        for ci in range(SUB_TILE // CHUNK):
            bat_ref[r0 // CHUNK + ci] = _dot_nt(wbat_ref[...], hb[ci * CHUNK:(ci + 1) * CHUNK, :])

    _lock_step(sub_tile(r0) for r0 in range(0, x_ref.shape[0], SUB_TILE))


def _inproj(x, mod3, gains, w_qkv, w_ba, w_bat, conv_qkv, *, mod_row_fn, row_len):
    n_tok = x.shape[0]
    tm = WIDE_TILE
    const = lambda i: (0, 0)
    return pl.pallas_call(
        functools.partial(_inproj_kernel, row_len=row_len),
        grid=(n_tok // tm,),
        in_specs=[pl.BlockSpec((tm, D_MODEL), lambda i: (i, 0)),
                  pl.BlockSpec((1, 6, D_MODEL), lambda i: (mod_row_fn(i, tm), 0, 0)),
                  pl.BlockSpec((4, D_MODEL), const),
                  pl.BlockSpec((D_MODEL, N_QKV), const),
                  pl.BlockSpec((D_MODEL, 4 * H_A), const),
                  pl.BlockSpec((4 * H_A, D_MODEL), const),
                  pl.BlockSpec((3, N_QKV), const)],
        out_specs=[pl.BlockSpec((tm, N_QKV), lambda i: (i, 0)),
                   pl.BlockSpec((tm, 4 * H_A), lambda i: (i, 0)),
                   pl.BlockSpec((tm // CHUNK, 4 * H_A, CHUNK), lambda i: (i, 0, 0))],
        out_shape=[jax.ShapeDtypeStruct((n_tok, N_QKV), F32),
                   jax.ShapeDtypeStruct((n_tok, 4 * H_A), F32),
                   jax.ShapeDtypeStruct((n_tok // CHUNK, 4 * H_A, CHUNK), F32)],
        compiler_params=pltpu.CompilerParams(dimension_semantics=("arbitrary",),
                                             vmem_limit_bytes=VMEM_LIMIT),
        name="inproj",
    )(x, mod3, gains, w_qkv, w_ba, w_bat, conv_qkv)


def _unit_tri_inverse_stages(lows, lower):
    c = CHUNK
    ri = lax.broadcasted_iota(jnp.int32, (c, c), 0)
    ci = lax.broadcasted_iota(jnp.int32, (c, c), 1)
    eye = (ri == ci).astype(F32)
    rng = range(len(lows))
    joins = lambda b: ((ri // (2 * b)) == (ci // (2 * b))) & ((ri // b) != (ci // b))
    t = [eye - jnp.where(joins(1), lows[i], 0.0) for i in rng]
    b = 2
    while b < c:
        mask = joins(b)
        t16 = [t[i].astype(BF16) for i in rng]
        e16 = [jnp.where(mask, lows[i], 0.0).astype(BF16) for i in rng]
        if b % SUBLANES:
            x = [_dot(t16[i], e16[i]) for i in rng]
            yield
            t = [t[i] - _dot(x[i].astype(BF16), t16[i]) for i in rng]
            yield
        else:
            first = [slice(s, s + b) for s in range(0, c, 2 * b)]
            second = [slice(s + b, s + 2 * b) for s in range(0, c, 2 * b)]
            keep, upd = (first, second) if lower else (second, first)
            t_upd = [jnp.concatenate([t[i][r] for r in upd], axis=0) for i in rng]
            x = [_dot(t_upd[i].astype(BF16), e16[i]) for i in rng]
            yield
            t_upd = [t_upd[i] - _dot(x[i].astype(BF16), t16[i]) for i in rng]
            yield
            pair = lambda kept, new: (kept, new) if lower else (new, kept)
            t = [jnp.concatenate(
                [piece for k, r in enumerate(keep)
                 for piece in pair(t[i][r], t_upd[i][k * b:(k + 1) * b])], axis=0) for i in rng]
        b *= 2
    return t


def _split3(x):
    pieces = []
    for _ in range(3):
        p = x.astype(BF16)
        x = x - p.astype(F32)
        pieces.append(p)
    return pieces


def _group_stages(qkv_ref, o_ref, s_ref, bb, d, gates):
    c = CHUNK
    fwd = d == 0
    row = lax.broadcasted_iota(jnp.int32, (c, c), 0)
    col = lax.broadcasted_iota(jnp.int32, (c, c), 1)
    incl = (row >= col) if fwd else (row <= col)
    strict = (row > col) if fwd else (row < col)
    lane = lax.broadcasted_iota(jnp.int32, (c, 2 * DK), 1)
    beta_all, beta_all_t, gc_all, gc_all_t, eg_all, tail_all, eglast_all = gates
    heads = range(H_A)
    kmask = [(lane < DK) if h % 2 else (lane >= DK) for h in heads]
    kv_slab = lambda h: qkv_ref[bb, :, W_A + h * 2 * DK:W_A + (h + 1) * 2 * DK]
    q_slab = lambda h: qkv_ref[bb, :, (h // 2) * 2 * DK:(h // 2 + 1) * 2 * DK]

    decay, kq = [], []
    for h in heads:
        j = d * H_A + h
        kv = kv_slab(h)
        lhs16 = jnp.concatenate([kv * beta_all[:, j:j + 1], q_slab(h)], axis=0).astype(BF16)
        kq.append(_dot_nt(lhs16, jnp.where(kmask[h], kv, 0.0).astype(BF16)))
        diff = gc_all[:, j:j + 1] - gc_all_t[j:j + 1, :]
        decay.append(jnp.where(incl, jnp.exp(jnp.where(incl, diff, 0.0)), 0.0))
    yield

    lows = [jnp.where(strict, kq[h][:c] * decay[h], 0.0) for h in heads]
    attn16 = [(kq[h][c:] * decay[h]).astype(BF16) for h in heads]
    tmat = yield from _unit_tri_inverse_stages(lows, lower=fwd)

    uw, qh = [], []
    for h in heads:
        j = d * H_A + h
        eg = jnp.broadcast_to(eg_all[:, j:j + 1], (c, 2 * DK))
        kv = kv_slab(h)
        t_beta16 = (tmat[h] * beta_all_t[j:j + 1, :]).astype(BF16)
        uw.append(_dot(t_beta16, jnp.where(kmask[h], kv * eg, kv).astype(BF16)))
        qh.append(q_slab(h) * eg)
    yield

    s_old = [s_ref[bb, d, h] for h in heads]
    wq = [_dot(jnp.concatenate([uw[h], qh[h]], axis=0).astype(BF16), s_old[h].astype(BF16))
          for h in heads]
    yield

    v_new16 = [jnp.where(kmask[h], 0.0, uw[h] - wq[h][:c]).astype(BF16) for h in heads]
    o = [wq[h][c:] + _dot(attn16[h], v_new16[h]) for h in heads]
    for h in heads:
        j = d * H_A + h
        k0 = 0 if h % 2 else DK
        kt16 = (kv_slab(h)[:, k0:k0 + DK] * tail_all[:, j:j + 1]).astype(BF16)
        s_ref[bb, d, h, k0:k0 + DK, :] = (s_old[h][k0:k0 + DK] * eglast_all[:, j:j + 1]
                                          + _dot_tn(kt16, v_new16[h]))
    for h in range(0, H_A, 2):
        o_ref[bb, :, h * DV:(h + 2) * DV] = o[h] + o[h + 1]


def _delta_kernel(*refs, zero_init, emit_state):
    (qkv_f_ref, qkv_b_ref, ba_f_ref, ba_b_ref, bat_f_ref, bat_b_ref,
     alog_ref, dtb_ref, alog_t_ref, dtb_t_ref) = refs[:10]
    rest = list(refs[10:])
    s0_ref = None if zero_init else rest.pop(0)
    o_f_ref, o_b_ref = rest.pop(0), rest.pop(0)
    sfin_ref = rest.pop(0) if emit_state else None
    s_ref = rest.pop(0)

    n = pl.program_id(1)
    n_chunks = pl.num_programs(1)
    nb = s_ref.shape[0]
    state_at = lambda h: (0, DV) if h % 2 else (DK, 0)

    @pl.when(n == 0)
    def _():
        s_ref[...] = jnp.zeros(s_ref.shape, F32)
        if not zero_init:
            for bb in range(nb):
                for d in range(2):
                    for h in range(H_A):
                        r0, l0 = state_at(h)
                        s_ref[bb, d, h, r0:r0 + DK, l0:l0 + DV] = s0_ref[bb, d, h]

    c = CHUNK
    row = lax.broadcasted_iota(jnp.int32, (c, c), 0)
    col = lax.broadcasted_iota(jnp.int32, (c, c), 1)
    tril16 = (row >= col).astype(BF16)
    triu16 = (row <= col).astype(BF16)

    programs = []
    for bb in range(nb):
        for d, (qkv_ref, ba_ref, bat_ref, o_ref) in enumerate(
                ((qkv_f_ref, ba_f_ref, bat_f_ref, o_f_ref),
                 (qkv_b_ref, ba_b_ref, bat_b_ref, o_b_ref))):
            fwd = d == 0
            csum16, csum_t16 = (tril16, triu16) if fwd else (triu16, tril16)
            last = c - 1 if fwd else 0
            ba = ba_ref[bb]
            bat = bat_ref[bb, 0]
            beta_all = jax.nn.sigmoid(ba[:, 0:2 * H_A])
            g_all = -jnp.exp(alog_ref[...]) * jax.nn.softplus(ba[:, 2 * H_A:] + dtb_ref[...])
            g_all_t = -jnp.exp(alog_t_ref[...]) * jax.nn.softplus(
                bat[2 * H_A:, :] + dtb_t_ref[...])
            gc_all = sum(_dot(csum16, p) for p in _split3(g_all))
            gc_all_t = sum(_dot(p, csum_t16) for p in _split3(g_all_t))
            glast_all = gc_all[last:last + 1, :]
            beta_all_t = jax.nn.sigmoid(bat[0:2 * H_A, :])
            gates = (beta_all, beta_all_t, gc_all, gc_all_t, jnp.exp(gc_all),
                     jnp.exp(glast_all - gc_all), jnp.exp(glast_all))
            programs.append(_group_stages(qkv_ref, o_ref, s_ref, bb, d, gates))

    _lock_step(programs)

    if emit_state:
        @pl.when(n == n_chunks - 1)
        def _():
            for bb in range(nb):
                for d in range(2):
                    for h in range(H_A):
                        r0, l0 = state_at(h)
                        sfin_ref[bb, d, h] = s_ref[bb, d, h, r0:r0 + DK, l0:l0 + DV]


def _delta_scan(qkv, ba, bat, a_log, dt_bias, s0, *, batch, emit_state):
    n_tok = qkv.shape[0]
    t = n_tok // batch
    nc = t // CHUNK
    nb = DELTA_BATCH
    assert batch % nb == 0
    qkv = qkv.reshape(batch, t, N_QKV)
    ba = ba.reshape(batch, t, 4 * H_A)
    bat = bat.reshape(batch, nc, 4 * H_A, CHUNK)
    zero_init = s0 is None
    alog = a_log.reshape(1, 2 * H_A)
    dtb = dt_bias.reshape(1, 2 * H_A)
    alog_t = a_log.reshape(2 * H_A, 1)
    dtb_t = dt_bias.reshape(2 * H_A, 1)

    fw = lambda b, n: (b, n, 0)
    bw = lambda b, n: (b, nc - 1 - n, 0)
    small = lambda b, n: (0, 0)
    in_specs = [pl.BlockSpec((nb, CHUNK, N_QKV), fw),
                pl.BlockSpec((nb, CHUNK, N_QKV), bw),
                pl.BlockSpec((nb, CHUNK, 4 * H_A), fw),
                pl.BlockSpec((nb, CHUNK, 4 * H_A), bw),
                pl.BlockSpec((nb, 1, 4 * H_A, CHUNK), lambda b, n: (b, n, 0, 0)),
                pl.BlockSpec((nb, 1, 4 * H_A, CHUNK), lambda b, n: (b, nc - 1 - n, 0, 0)),
                pl.BlockSpec((1, 2 * H_A), small),
                pl.BlockSpec((1, 2 * H_A), small),
                pl.BlockSpec((2 * H_A, 1), small),
                pl.BlockSpec((2 * H_A, 1), small)]
    args = [qkv, qkv, ba, ba, bat, bat, alog, dtb, alog_t, dtb_t]
    state_spec = pl.BlockSpec((nb, 2, H_A, DK, DV), lambda b, n: (b, 0, 0, 0, 0))
    if not zero_init:
        in_specs.append(state_spec)
        args.append(s0)
    out_specs = [pl.BlockSpec((nb, CHUNK, W_A), fw), pl.BlockSpec((nb, CHUNK, W_A), bw)]
    out_shape = [jax.ShapeDtypeStruct((batch, t, W_A), F32),
                 jax.ShapeDtypeStruct((batch, t, W_A), F32)]
    if emit_state:
        out_specs.append(state_spec)
        out_shape.append(jax.ShapeDtypeStruct((batch, 2, H_A, DK, DV), F32))
    return pl.pallas_call(
        functools.partial(_delta_kernel, zero_init=zero_init, emit_state=emit_state),
        grid=(batch // nb, nc),
        in_specs=in_specs,
        out_specs=out_specs,
        out_shape=out_shape,
        scratch_shapes=[pltpu.VMEM((nb, 2, H_A, 2 * DK, 2 * DV), F32)],
        compiler_params=pltpu.CompilerParams(dimension_semantics=("arbitrary", "arbitrary"),
                                             vmem_limit_bytes=VMEM_LIMIT),
        name="delta_scan",
    )(*args)


def _mix_kernel(x_ref, mod_ref, gain_ref, of_ref, ob_ref, wz_ref, wsc_ref, wg_ref, conv_ref,
                onorm_ref, wa_ref, wb_ref, wo_ref, out_ref, *, row_len):
    def sub_tile(rows):
        x = x_ref[rows, :]
        h = _rms(x, gain_ref[0:1, :]) * (1.0 + mod_ref[0, 1:2, :]) + mod_ref[0, 0:1, :]
        hb = h.astype(BF16)
        o = of_ref[rows, :] + ob_ref[rows, :]
        slab = 2 * DV
        lo = lax.broadcasted_iota(jnp.int32, (SUB_TILE, slab), 1) < DV
        inv_rms = []
        for s in range(W_A // slab):
            o2 = o[:, s * slab:(s + 1) * slab]
            o2 = o2 * o2
            ms_lo = jnp.sum(jnp.where(lo, o2, 0.0), axis=-1, keepdims=True) * (1.0 / DV)
            ms_hi = jnp.sum(jnp.where(lo, 0.0, o2), axis=-1, keepdims=True) * (1.0 / DV)
            inv_rms.append(jnp.where(lo, lax.rsqrt(ms_lo + EPS), lax.rsqrt(ms_hi + EPS)))
        yield
        z = _dot(hb, wz_ref[...])
        yield
        oa = o * jnp.concatenate(inv_rms, axis=-1) * onorm_ref[...] * _silu(z)
        y_a = _dot(oa.astype(BF16), wa_ref[...])
        sc = _dot(hb, wsc_ref[...])
        yield
        bg, cg, xs = sc[:, 0:W_B], sc[:, W_B:2 * W_B], sc[:, 2 * W_B:3 * W_B]
        yb_in = bg * _conv3_rows(cg * xs, conv_ref, row_len)
        y_b = _dot(yb_in.astype(BF16), wb_ref[...])
        gates = _dot(hb, wg_ref[...])
        yield
        m = (jax.nn.sigmoid(gates[:, 0:D_MODEL]) * y_a
             + jax.nn.sigmoid(gates[:, D_MODEL:]) * y_b)
        y = _dot(m.astype(BF16), wo_ref[...])
        yield
        out_ref[rows, :] = x + mod_ref[0, 2:3, :] * _rms(y, gain_ref[1:2, :])

    _lock_step(sub_tile(slice(r0, r0 + SUB_TILE)) for r0 in range(0, x_ref.shape[0], SUB_TILE))


def _mix(x, mod3, gains, o_f, o_b, w_z, w_sc, w_g, conv_sc, onorm_t, w_a, w_b, w_o,
         *, mod_row_fn, row_len):
    n_tok = x.shape[0]
    tm = WIDE_TILE
    const = lambda i: (0, 0)
    tok = lambda i: (i, 0)
    return pl.pallas_call(
        functools.partial(_mix_kernel, row_len=row_len),
        grid=(n_tok // tm,),
        in_specs=[pl.BlockSpec((tm, D_MODEL), tok),
                  pl.BlockSpec((1, 6, D_MODEL), lambda i: (mod_row_fn(i, tm), 0, 0)),
                  pl.BlockSpec((4, D_MODEL), const),
                  pl.BlockSpec((tm, W_A), tok),
                  pl.BlockSpec((tm, W_A), tok),
                  _resident((D_MODEL, W_A)),
                  _resident((D_MODEL, 3 * W_B)),
                  _resident((D_MODEL, 2 * D_MODEL)),
                  pl.BlockSpec((3, W_B), const),
                  pl.BlockSpec((1, W_A), const),
                  _resident((W_A, D_MODEL)),
                  _resident((W_B, D_MODEL)),
                  _resident((D_MODEL, D_MODEL))],
        out_specs=pl.BlockSpec((tm, D_MODEL), tok),
        out_shape=jax.ShapeDtypeStruct((n_tok, D_MODEL), F32),
        compiler_params=pltpu.CompilerParams(dimension_semantics=("arbitrary",),
                                             vmem_limit_bytes=VMEM_LIMIT),
        name="mix",
    )(x, mod3, gains, o_f, o_b, w_z, w_sc, w_g, conv_sc, onorm_t, w_a, w_b, w_o)


def _ffn_kernel(x_ref, mod_ref, gain_ref, win_ref, wout_ref, out_ref):
    def sub_tile(rows):
        x = x_ref[rows, :]
        h = _rms(x, gain_ref[2:3, :]) * (1.0 + mod_ref[0, 4:5, :]) + mod_ref[0, 3:4, :]
        hb = h.astype(BF16)
        yield
        gate = _dot(hb, win_ref[:, 0:D_FF])
        yield
        up = _dot(hb, win_ref[:, D_FF:])
        yield
        y = _dot((_silu(gate) * up).astype(BF16), wout_ref[...])
        yield
        out_ref[rows, :] = x + mod_ref[0, 5:6, :] * _rms(y, gain_ref[3:4, :])

    _lock_step(sub_tile(slice(r0, r0 + SUB_TILE)) for r0 in range(0, x_ref.shape[0], SUB_TILE))


def _ffn(x, mod3, gains, w_in, w_out, *, mod_row_fn):
    n_tok = x.shape[0]
    tm = WIDE_TILE
    const = lambda i: (0, 0)
    tok = lambda i: (i, 0)
    return pl.pallas_call(
        _ffn_kernel,
        grid=(n_tok // tm,),
        in_specs=[pl.BlockSpec((tm, D_MODEL), tok),
                  pl.BlockSpec((1, 6, D_MODEL), lambda i: (mod_row_fn(i, tm), 0, 0)),
                  pl.BlockSpec((4, D_MODEL), const),
                  _resident((D_MODEL, 2 * D_FF)),
                  _resident((D_FF, D_MODEL))],
        out_specs=pl.BlockSpec((tm, D_MODEL), tok),
        out_shape=jax.ShapeDtypeStruct((n_tok, D_MODEL), F32),
        compiler_params=pltpu.CompilerParams(dimension_semantics=("arbitrary",),
                                             vmem_limit_bytes=VMEM_LIMIT),
        name="ffn",
    )(x, mod3, gains, w_in, w_out)


def _split_w_in_kernel(w_ref, qkv_ref, z_ref, ba_ref, sc_ref, g_ref):
    slab = 2 * DK
    left = lax.broadcasted_iota(jnp.int32, (w_ref.shape[1], slab), 1) < DK
    for p in range(H_A // 2):
        q_p = w_ref[0, :, p * slab:(p + 1) * slab]
        k_p = w_ref[0, :, W_A + p * slab:W_A + (p + 1) * slab]
        v_p = w_ref[0, :, 2 * W_A + p * slab:2 * W_A + (p + 1) * slab]
        k_swapped = pltpu.roll(k_p, DK, 1)
        qkv_ref[:, p * slab:(p + 1) * slab] = pltpu.roll(q_p, DK, 1).astype(BF16)
        even = W_A + 2 * p * slab
        qkv_ref[:, even:even + slab] = jnp.where(left, v_p, k_swapped).astype(BF16)
        qkv_ref[:, even + slab:even + 2 * slab] = jnp.where(left, k_swapped, v_p).astype(BF16)
    o_z, o_ba = N_QKV, N_QKV + W_A
    o_sc = o_ba + 4 * H_A
    o_g = o_sc + 3 * W_B
    z_ref[...] = w_ref[0, :, o_z:o_ba].astype(BF16)
    ba_ref[...] = w_ref[0, :, o_ba:o_sc].astype(BF16)
    sc_ref[...] = w_ref[0, :, o_sc:o_g].astype(BF16)
    g_ref[...] = w_ref[0, :, o_g:].astype(BF16)


def _split_w_in(w, layer):
    _, rows, n_in = w.shape
    blk = 128
    widths = (N_QKV, W_A, 4 * H_A, 3 * W_B, 2 * D_MODEL)
    assert sum(widths) == n_in and rows % blk == 0
    return pl.pallas_call(
        _split_w_in_kernel,
        grid=(rows // blk,),
        in_specs=[pl.BlockSpec((1, blk, n_in), lambda i: (layer, i, 0))],
        out_specs=[pl.BlockSpec((blk, n), lambda i: (i, 0)) for n in widths],
        out_shape=[jax.ShapeDtypeStruct((rows, n), BF16) for n in widths],
        name="split_w_in",
    )(w)


def _to_slab_order(w):
    rows = w.shape[0]
    pairs = lambda a: a.reshape(rows, H_A // 2, 2, DK)
    q, k, v = (pairs(w[:, i * W_A:(i + 1) * W_A]) for i in range(3))
    q_slabs = q[:, :, ::-1, :].reshape(rows, W_A)
    even = jnp.concatenate([v[:, :, 0], k[:, :, 0]], axis=-1)
    odd = jnp.concatenate([k[:, :, 1], v[:, :, 1]], axis=-1)
    kv_slabs = jnp.stack([even, odd], axis=2).reshape(rows, 2 * W_A)
    return jnp.concatenate([q_slabs, kv_slabs], axis=-1)


def kernel(x_prompt, x_sample, state_delta, c, c_ctx, w_mod, b_mod, norm_gains, w_in, conv_qkv,
           a_log, dt_bias, onorm, conv_sc, w_branch_a, w_branch_b, w_out, w_ffn_in, w_ffn_out):
    depth = w_in.shape[0]
    assert depth == 1
    b_ctx, t_ctx, _ = x_prompt.shape
    b_lat, t_lat, _ = x_sample.shape
    assert SUB_TILE % t_ctx == 0 and SUB_TILE % GRID_W == 0 and t_ctx & (t_ctx - 1) == 0
    assert WIDE_TILE % SUB_TILE == 0 and t_lat % WIDE_TILE == 0
    assert (b_ctx * t_ctx) % WIDE_TILE == 0
    assert b_lat + 1 <= MOD_ROWS
    layer = 0

    cvec = jnp.zeros((MOD_ROWS, D_MODEL), F32).at[0].set(c_ctx).at[1:1 + b_lat].set(c)
    mod3 = _modulation(cvec, w_mod[layer], b_mod[layer][None, :]).reshape(MOD_ROWS, 6, D_MODEL)

    w_qkv, w_z, w_ba, w_sc, w_g = _split_w_in(w_in, layer)
    conv_perm = _to_slab_order(conv_qkv[layer])
    w_bat = w_ba.T
    w_a = w_branch_a[layer].astype(BF16)
    w_b = w_branch_b[layer].astype(BF16)
    w_o = w_out[layer].astype(BF16)
    w_f1 = w_ffn_in[layer].astype(BF16)
    w_f2 = w_ffn_out[layer].astype(BF16)
    gains = norm_gains[layer]
    onorm_t = jnp.tile(onorm[layer], H_A)[None, :]

    def run_group(x3, s0, mod_row_fn, row_len, emit_state):
        batch, t, _ = x3.shape
        x = x3.reshape(batch * t, D_MODEL)
        qkv, ba, bat = _inproj(x, mod3, gains, w_qkv, w_ba, w_bat, conv_perm,
                               mod_row_fn=mod_row_fn, row_len=row_len)
        res = _delta_scan(qkv, ba, bat, a_log[layer], dt_bias[layer], s0,
                          batch=batch, emit_state=emit_state)
        o_f = res[0].reshape(batch * t, W_A)
        o_b = res[1].reshape(batch * t, W_A)
        x1 = _mix(x, mod3, gains, o_f, o_b, w_z, w_sc, w_g, conv_sc[layer], onorm_t,
                  w_a, w_b, w_o, mod_row_fn=mod_row_fn, row_len=row_len)
        x2 = _ffn(x1, mod3, gains, w_f1, w_f2, mod_row_fn=mod_row_fn)
        return x2.reshape(batch, t, D_MODEL), (res[2] if emit_state else None)

    y_p, s_fin = run_group(x_prompt, None, lambda i, tm: 0, t_ctx, True)
    y_s, _ = run_group(x_sample, state_delta[:, layer], lambda i, tm: 1 + (i * tm) // t_lat,
                       GRID_W, False)
    new_state = s_fin[:, None].astype(x_prompt.dtype)
    return (y_p, y_s, new_state)
```

```python
import functools

import jax
import jax.numpy as jnp
from jax import lax
from jax.experimental import pallas as pl
from jax.experimental.pallas import tpu as pltpu

D_MODEL = 1024
H_A = 8
DK = 64
DV = 64
W_A = H_A * DV
W_B = 512
CHUNK = 64
GRID_W = 64
EPS = 1e-6
N_QKV = 3 * H_A * DK
D_FF = 2816
SUBLANES = 8
MOD_ROWS = SUBLANES
WIDE_TILE = 512
SUB_TILE = 256
DELTA_BATCH = 4
VMEM_LIMIT = 56 * 1024 * 1024

F32 = jnp.float32
BF16 = jnp.bfloat16
HIGHEST = lax.Precision.HIGHEST


def _dot(a, b, precision=None):
    return jnp.dot(a, b, preferred_element_type=F32, precision=precision)


def _dot_nt(a, b, precision=None):
    return lax.dot_general(a, b, (((1,), (1,)), ((), ())), preferred_element_type=F32,
                           precision=precision)


def _dot_tn(a, b, precision=None):
    return lax.dot_general(a, b, (((0,), (0,)), ((), ())), preferred_element_type=F32,
                           precision=precision)


def _resident(shape):
    return pl.BlockSpec(shape, lambda i: (0,) * len(shape), pipeline_mode=pl.Buffered(1))


def _lock_step(programs):
    programs = list(programs)
    while programs:
        for prog in list(programs):
            try:
                next(prog)
            except StopIteration:
                programs.remove(prog)


def _silu(x):
    return x * jax.nn.sigmoid(x)


def _rms(x, gain):
    return x * lax.rsqrt(jnp.mean(x * x, axis=-1, keepdims=True) + EPS) * gain


def _conv3_rows(x, w_ref, row_len):
    n = x.shape[0]
    pos = lax.broadcasted_iota(jnp.int32, (n, 1), 0) & (row_len - 1)
    prev = jnp.where(pos == 0, 0.0, pltpu.roll(x, 1, 0))
    nxt = jnp.where(pos == row_len - 1, 0.0, pltpu.roll(x, n - 1, 0))
    return prev * w_ref[0:1, :] + x * w_ref[1:2, :] + nxt * w_ref[2:3, :]


def _mod_kernel(c_ref, w_ref, b_ref, o_ref):
    rows = c_ref.shape[0]
    s3 = jnp.concatenate([p.astype(F32) for p in _split3(_silu(c_ref[...]))], axis=0).astype(BF16)
    w = w_ref[0]
    w_hi = w.astype(BF16)
    w_lo = (w - w_hi.astype(F32)).astype(BF16)
    acc = _dot(s3, w_hi) + _dot(s3, w_lo)
    o_ref[...] = acc[0:rows] + acc[rows:2 * rows] + acc[2 * rows:] + b_ref[...]


def _modulation(cvec, w_mod, b_mod, layer):
    n_out = w_mod.shape[2]
    blk = D_MODEL
    return pl.pallas_call(
        _mod_kernel,
        grid=(n_out // blk,),
        in_specs=[pl.BlockSpec((MOD_ROWS, D_MODEL), lambda j: (0, 0)),
                  pl.BlockSpec((1, D_MODEL, blk), lambda j: (layer, 0, j)),
                  pl.BlockSpec((1, blk), lambda j: (layer, j))],
        out_specs=pl.BlockSpec((MOD_ROWS, blk), lambda j: (0, j)),
        out_shape=jax.ShapeDtypeStruct((MOD_ROWS, n_out), F32),
        name="modulation",
    )(cvec, w_mod, b_mod)


def _inproj_kernel(x_ref, mod_ref, gain_ref, wqkv_ref, wba_ref, wbat_ref, conv_ref,
                   qkv_ref, ba_ref, bat_ref, *, row_len):
    slab = 2 * DK
    lo = lax.broadcasted_iota(jnp.int32, (SUB_TILE, slab), 1) < DK
    ssq = lambda v, m: jnp.sum(jnp.where(m, v * v, 0.0), axis=-1, keepdims=True)

    def sub_tile(r0):
        rows = slice(r0, r0 + SUB_TILE)
        x = x_ref[rows, :]
        h = _rms(x, gain_ref[0:1, :]) * (1.0 + mod_ref[0, 1:2, :]) + mod_ref[0, 0:1, :]
        hb = h.astype(BF16)
        yield
        for pair in range(N_QKV // (2 * slab)):
            cols2 = slice(pair * 2 * slab, (pair + 1) * 2 * slab)
            v2 = _silu(_conv3_rows(_dot(hb, wqkv_ref[:, cols2]), conv_ref.at[:, cols2], row_len))
            for half in range(2):
                s = 2 * pair + half
                cols = slice(s * slab, (s + 1) * slab)
                v = v2[:, half * slab:(half + 1) * slab]
                if s < W_A // slab:
                    r = jnp.where(lo, lax.rsqrt(ssq(v, lo) + EPS) * (DK ** -0.5),
                                  lax.rsqrt(ssq(v, ~lo) + EPS) * (DK ** -0.5))
                    qkv_ref[rows, cols] = v * r
                else:
                    kmask = lo if (s - W_A // slab) % 2 else ~lo
                    qkv_ref[rows, cols] = v * jnp.where(kmask, lax.rsqrt(ssq(v, kmask) + EPS), 1.0)
            yield
        ba_ref[rows, :] = _dot(hb, wba_ref[...])
        for ci in range(SUB_TILE // CHUNK):
            bat_ref[r0 // CHUNK + ci] = _dot_nt(wbat_ref[...], hb[ci * CHUNK:(ci + 1) * CHUNK, :])

    _lock_step(sub_tile(r0) for r0 in range(0, x_ref.shape[0], SUB_TILE))


def _inproj(x, mod3, gains, w_qkv, w_ba, w_bat, conv_qkv, *, mod_row_fn, row_len):
    n_tok = x.shape[0]
    tm = WIDE_TILE
    const = lambda i: (0, 0)
    return pl.pallas_call(
        functools.partial(_inproj_kernel, row_len=row_len),
        grid=(n_tok // tm,),
        in_specs=[pl.BlockSpec((tm, D_MODEL), lambda i: (i, 0)),
                  pl.BlockSpec((1, 6, D_MODEL), lambda i: (mod_row_fn(i, tm), 0, 0)),
                  pl.BlockSpec((4, D_MODEL), const),
                  pl.BlockSpec((D_MODEL, N_QKV), const),
                  pl.BlockSpec((D_MODEL, 4 * H_A), const),
                  pl.BlockSpec((4 * H_A, D_MODEL), const),
                  pl.BlockSpec((3, N_QKV), const)],
        out_specs=[pl.BlockSpec((tm, N_QKV), lambda i: (i, 0)),
                   pl.BlockSpec((tm, 4 * H_A), lambda i: (i, 0)),
                   pl.BlockSpec((tm // CHUNK, 4 * H_A, CHUNK), lambda i: (i, 0, 0))],
        out_shape=[jax.ShapeDtypeStruct((n_tok, N_QKV), F32),
                   jax.ShapeDtypeStruct((n_tok, 4 * H_A), F32),
                   jax.ShapeDtypeStruct((n_tok // CHUNK, 4 * H_A, CHUNK), F32)],
        compiler_params=pltpu.CompilerParams(dimension_semantics=("arbitrary",),
                                             vmem_limit_bytes=VMEM_LIMIT),
        name="inproj",
    )(x, mod3, gains, w_qkv, w_ba, w_bat, conv_qkv)


def _unit_tri_inverse_stages(lows, lower):
    c = CHUNK
    ri = lax.broadcasted_iota(jnp.int32, (c, c), 0)
    ci = lax.broadcasted_iota(jnp.int32, (c, c), 1)
    eye = (ri == ci).astype(F32)
    rng = range(len(lows))
    joins = lambda b: ((ri // (2 * b)) == (ci // (2 * b))) & ((ri // b) != (ci // b))
    t = [eye - jnp.where(joins(1), lows[i], 0.0) for i in rng]
    b = 2
    while b < c:
        mask = joins(b)
        t16 = [t[i].astype(BF16) for i in rng]
        e16 = [jnp.where(mask, lows[i], 0.0).astype(BF16) for i in rng]
        if b % SUBLANES:
            x = [_dot(t16[i], e16[i]) for i in rng]
            yield
            t = [t[i] - _dot(x[i].astype(BF16), t16[i]) for i in rng]
            yield
        else:
            first = [slice(s, s + b) for s in range(0, c, 2 * b)]
            second = [slice(s + b, s + 2 * b) for s in range(0, c, 2 * b)]
            keep, upd = (first, second) if lower else (second, first)
            t_upd = [jnp.concatenate([t[i][r] for r in upd], axis=0) for i in rng]
            x = [_dot(t_upd[i].astype(BF16), e16[i]) for i in rng]
            yield
            t_upd = [t_upd[i] - _dot(x[i].astype(BF16), t16[i]) for i in rng]
            yield
            pair = lambda kept, new: (kept, new) if lower else (new, kept)
            t = [jnp.concatenate(
                [piece for k, r in enumerate(keep)
                 for piece in pair(t[i][r], t_upd[i][k * b:(k + 1) * b])], axis=0) for i in rng]
        b *= 2
    return t


def _split3(x):
    pieces = []
    for _ in range(3):
        p = x.astype(BF16)
        x = x - p.astype(F32)
        pieces.append(p)
    return pieces


def _group_stages(qkv_ref, o_ref, s_ref, bb, d, gates):
    c = CHUNK
    fwd = d == 0
    row = lax.broadcasted_iota(jnp.int32, (c, c), 0)
    col = lax.broadcasted_iota(jnp.int32, (c, c), 1)
    incl = (row >= col) if fwd else (row <= col)
    strict = (row > col) if fwd else (row < col)
    lane = lax.broadcasted_iota(jnp.int32, (c, 2 * DK), 1)
    beta_all, beta_all_t, gc_all, gc_all_t, eg_all, tail_all, eglast_all = gates
    heads = range(H_A)
    kmask = [(lane < DK) if h % 2 else (lane >= DK) for h in heads]
    kv_slab = lambda h: qkv_ref[bb, :, W_A + h * 2 * DK:W_A + (h + 1) * 2 * DK]
    q_slab = lambda h: qkv_ref[bb, :, (h // 2) * 2 * DK:(h // 2 + 1) * 2 * DK]

    decay, kq = [], []
    for h in heads:
        j = d * H_A + h
        kv = kv_slab(h)
        lhs16 = jnp.concatenate([kv * beta_all[:, j:j + 1], q_slab(h)], axis=0).astype(BF16)
        kq.append(_dot_nt(lhs16, jnp.where(kmask[h], kv, 0.0).astype(BF16)))
        diff = gc_all[:, j:j + 1] - gc_all_t[j:j + 1, :]
        decay.append(jnp.where(incl, jnp.exp(jnp.where(incl, diff, 0.0)), 0.0))
    yield

    lows = [jnp.where(strict, kq[h][:c] * decay[h], 0.0) for h in heads]
    attn16 = [(kq[h][c:] * decay[h]).astype(BF16) for h in heads]
    tmat = yield from _unit_tri_inverse_stages(lows, lower=fwd)

    uw, qh = [], []
    for h in heads:
        j = d * H_A + h
        eg = jnp.broadcast_to(eg_all[:, j:j + 1], (c, 2 * DK))
        kv = kv_slab(h)
        t_beta16 = (tmat[h] * beta_all_t[j:j + 1, :]).astype(BF16)
        uw.append(_dot(t_beta16, jnp.where(kmask[h], kv * eg, kv).astype(BF16)))
        qh.append(q_slab(h) * eg)
    yield

    s_old = [s_ref[bb, d, h] for h in heads]
    wq = [_dot(jnp.concatenate([uw[h], qh[h]], axis=0).astype(BF16), s_old[h].astype(BF16))
          for h in heads]
    yield

    v_new16 = [jnp.where(kmask[h], 0.0, uw[h] - wq[h][:c]).astype(BF16) for h in heads]
    o = [wq[h][c:] + _dot(attn16[h], v_new16[h]) for h in heads]
    for h in heads:
        j = d * H_A + h
        k0 = 0 if h % 2 else DK
        kt16 = (kv_slab(h)[:, k0:k0 + DK] * tail_all[:, j:j + 1]).astype(BF16)
        s_ref[bb, d, h, k0:k0 + DK, :] = (s_old[h][k0:k0 + DK] * eglast_all[:, j:j + 1]
                                          + _dot_tn(kt16, v_new16[h]))
    for h in range(0, H_A, 2):
        o_ref[bb, :, h * DV:(h + 2) * DV] = o[h] + o[h + 1]


def _delta_kernel(*refs, zero_init, emit_state):
    (qkv_f_ref, qkv_b_ref, ba_f_ref, ba_b_ref, bat_f_ref, bat_b_ref,
     alog_ref, dtb_ref, alog_t_ref, dtb_t_ref) = refs[:10]
    rest = list(refs[10:])
    s0_ref = None if zero_init else rest.pop(0)
    o_f_ref, o_b_ref = rest.pop(0), rest.pop(0)
    sfin_ref = rest.pop(0) if emit_state else None
    s_ref = rest.pop(0)

    n = pl.program_id(1)
    n_chunks = pl.num_programs(1)
    nb = s_ref.shape[0]
    state_at = lambda h: (0, DV) if h % 2 else (DK, 0)

    @pl.when(n == 0)
    def _():
        s_ref[...] = jnp.zeros(s_ref.shape, F32)
        if not zero_init:
            for bb in range(nb):
                for d in range(2):
                    for h in range(H_A):
                        r0, l0 = state_at(h)
                        s_ref[bb, d, h, r0:r0 + DK, l0:l0 + DV] = s0_ref[bb, d, h]

    c = CHUNK
    row = lax.broadcasted_iota(jnp.int32, (c, c), 0)
    col = lax.broadcasted_iota(jnp.int32, (c, c), 1)
    tril16 = (row >= col).astype(BF16)
    triu16 = (row <= col).astype(BF16)

    programs = []
    for bb in range(nb):
        for d, (qkv_ref, ba_ref, bat_ref, o_ref) in enumerate(
                ((qkv_f_ref, ba_f_ref, bat_f_ref, o_f_ref),
                 (qkv_b_ref, ba_b_ref, bat_b_ref, o_b_ref))):
            fwd = d == 0
            csum16, csum_t16 = (tril16, triu16) if fwd else (triu16, tril16)
            last = c - 1 if fwd else 0
            ba = ba_ref[bb]
            bat = bat_ref[bb, 0]
            beta_all = jax.nn.sigmoid(ba[:, 0:2 * H_A])
            g_all = -jnp.exp(alog_ref[...]) * jax.nn.softplus(ba[:, 2 * H_A:] + dtb_ref[...])
            g_all_t = -jnp.exp(alog_t_ref[...]) * jax.nn.softplus(
                bat[2 * H_A:, :] + dtb_t_ref[...])
            gc_all = sum(_dot(csum16, p) for p in _split3(g_all))
            gc_all_t = sum(_dot(p, csum_t16) for p in _split3(g_all_t))
            glast_all = gc_all[last:last + 1, :]
            beta_all_t = jax.nn.sigmoid(bat[0:2 * H_A, :])
            gates = (beta_all, beta_all_t, gc_all, gc_all_t, jnp.exp(gc_all),
                     jnp.exp(glast_all - gc_all), jnp.exp(glast_all))
            programs.append(_group_stages(qkv_ref, o_ref, s_ref, bb, d, gates))

    _lock_step(programs)

    if emit_state:
        @pl.when(n == n_chunks - 1)
        def _():
            for bb in range(nb):
                for d in range(2):
                    for h in range(H_A):
                        r0, l0 = state_at(h)
                        sfin_ref[bb, d, h] = s_ref[bb, d, h, r0:r0 + DK, l0:l0 + DV]


def _delta_scan(qkv, ba, bat, a_log, dt_bias, s0, *, batch, emit_state):
    n_tok = qkv.shape[0]
    t = n_tok // batch
    nc = t // CHUNK
    nb = min(DELTA_BATCH, batch)
    assert batch % nb == 0
    qkv = qkv.reshape(batch, t, N_QKV)
    ba = ba.reshape(batch, t, 4 * H_A)
    bat = bat.reshape(batch, nc, 4 * H_A, CHUNK)
    zero_init = s0 is None
    alog = a_log.reshape(1, 2 * H_A)
    dtb = dt_bias.reshape(1, 2 * H_A)
    alog_t = a_log.reshape(2 * H_A, 1)
    dtb_t = dt_bias.reshape(2 * H_A, 1)

    fw = lambda b, n: (b, n, 0)
    bw = lambda b, n: (b, nc - 1 - n, 0)
    small = lambda b, n: (0, 0)
    in_specs = [pl.BlockSpec((nb, CHUNK, N_QKV), fw),
                pl.BlockSpec((nb, CHUNK, N_QKV), bw),
                pl.BlockSpec((nb, CHUNK, 4 * H_A), fw),
                pl.BlockSpec((nb, CHUNK, 4 * H_A), bw),
                pl.BlockSpec((nb, 1, 4 * H_A, CHUNK), lambda b, n: (b, n, 0, 0)),
                pl.BlockSpec((nb, 1, 4 * H_A, CHUNK), lambda b, n: (b, nc - 1 - n, 0, 0)),
                pl.BlockSpec((1, 2 * H_A), small),
                pl.BlockSpec((1, 2 * H_A), small),
                pl.BlockSpec((2 * H_A, 1), small),
                pl.BlockSpec((2 * H_A, 1), small)]
    args = [qkv, qkv, ba, ba, bat, bat, alog, dtb, alog_t, dtb_t]
    state_spec = pl.BlockSpec((nb, 2, H_A, DK, DV), lambda b, n: (b, 0, 0, 0, 0))
    if not zero_init:
        in_specs.append(state_spec)
        args.append(s0)
    out_specs = [pl.BlockSpec((nb, CHUNK, W_A), fw), pl.BlockSpec((nb, CHUNK, W_A), bw)]
    out_shape = [jax.ShapeDtypeStruct((batch, t, W_A), F32),
                 jax.ShapeDtypeStruct((batch, t, W_A), F32)]
    if emit_state:
        out_specs.append(state_spec)
        out_shape.append(jax.ShapeDtypeStruct((batch, 2, H_A, DK, DV), F32))
    return pl.pallas_call(
        functools.partial(_delta_kernel, zero_init=zero_init, emit_state=emit_state),
        grid=(batch // nb, nc),
        in_specs=in_specs,
        out_specs=out_specs,
        out_shape=out_shape,
        scratch_shapes=[pltpu.VMEM((nb, 2, H_A, 2 * DK, 2 * DV), F32)],
        compiler_params=pltpu.CompilerParams(dimension_semantics=("arbitrary", "arbitrary"),
                                             vmem_limit_bytes=VMEM_LIMIT),
        name="delta_scan",
    )(*args)


def _mix_kernel(x_ref, mod_ref, gain_ref, of_ref, ob_ref, wz_ref, wsc_ref, wg_ref, conv_ref,
                onorm_ref, wa_ref, wb_ref, wo_ref, out_ref, *, row_len):
    def sub_tile(rows):
        x = x_ref[rows, :]
        h = _rms(x, gain_ref[0:1, :]) * (1.0 + mod_ref[0, 1:2, :]) + mod_ref[0, 0:1, :]
        hb = h.astype(BF16)
        yield
        sc = _dot(hb, wsc_ref[...])
        o = of_ref[rows, :] + ob_ref[rows, :]
        slab = 2 * DV
        lo = lax.broadcasted_iota(jnp.int32, (SUB_TILE, slab), 1) < DV
        inv_rms = []
        for s in range(W_A // slab):
            o2 = o[:, s * slab:(s + 1) * slab]
            o2 = o2 * o2
            ms_lo = jnp.sum(jnp.where(lo, o2, 0.0), axis=-1, keepdims=True) * (1.0 / DV)
            ms_hi = jnp.sum(jnp.where(lo, 0.0, o2), axis=-1, keepdims=True) * (1.0 / DV)
            inv_rms.append(jnp.where(lo, lax.rsqrt(ms_lo + EPS), lax.rsqrt(ms_hi + EPS)))
        yield
        z = _dot(hb, wz_ref[...])
        bg, cg, xs = sc[:, 0:W_B], sc[:, W_B:2 * W_B], sc[:, 2 * W_B:3 * W_B]
        yb_in16 = (bg * _conv3_rows(cg * xs, conv_ref, row_len)).astype(BF16)
        yield
        gates = _dot(hb, wg_ref[...])
        oa16 = (o * jnp.concatenate(inv_rms, axis=-1) * onorm_ref[...] * _silu(z)).astype(BF16)
        yield
        y_a = _dot(oa16, wa_ref[...])
        y_b = _dot(yb_in16, wb_ref[...])
        yield
        m = (jax.nn.sigmoid(gates[:, 0:D_MODEL]) * y_a
             + jax.nn.sigmoid(gates[:, D_MODEL:]) * y_b)
        y = _dot(m.astype(BF16), wo_ref[...])
        yield
        out_ref[rows, :] = x + mod_ref[0, 2:3, :] * _rms(y, gain_ref[1:2, :])

    _lock_step(sub_tile(slice(r0, r0 + SUB_TILE)) for r0 in range(0, x_ref.shape[0], SUB_TILE))


def _mix(x, mod3, gains, o_f, o_b, w_z, w_sc, w_g, conv_sc, onorm_t, w_a, w_b, w_o,
         *, mod_row_fn, row_len):
    n_tok = x.shape[0]
    tm = WIDE_TILE
    const = lambda i: (0, 0)
    tok = lambda i: (i, 0)
    return pl.pallas_call(
        functools.partial(_mix_kernel, row_len=row_len),
        grid=(n_tok // tm,),
        in_specs=[pl.BlockSpec((tm, D_MODEL), tok),
                  pl.BlockSpec((1, 6, D_MODEL), lambda i: (mod_row_fn(i, tm), 0, 0)),
                  pl.BlockSpec((4, D_MODEL), const),
                  pl.BlockSpec((tm, W_A), tok),
                  pl.BlockSpec((tm, W_A), tok),
                  _resident((D_MODEL, W_A)),
                  _resident((D_MODEL, 3 * W_B)),
                  _resident((D_MODEL, 2 * D_MODEL)),
                  pl.BlockSpec((3, W_B), const),
                  pl.BlockSpec((1, W_A), const),
                  _resident((W_A, D_MODEL)),
                  _resident((W_B, D_MODEL)),
                  _resident((D_MODEL, D_MODEL))],
        out_specs=pl.BlockSpec((tm, D_MODEL), tok),
        out_shape=jax.ShapeDtypeStruct((n_tok, D_MODEL), F32),
        compiler_params=pltpu.CompilerParams(dimension_semantics=("arbitrary",),
                                             vmem_limit_bytes=VMEM_LIMIT),
        name="mix",
    )(x, mod3, gains, o_f, o_b, w_z, w_sc, w_g, conv_sc, onorm_t, w_a, w_b, w_o)


def _ffn_kernel(x_ref, mod_ref, gain_ref, win_ref, wout_ref, out_ref):
    def sub_tile(rows):
        x = x_ref[rows, :]
        h = _rms(x, gain_ref[2:3, :]) * (1.0 + mod_ref[0, 4:5, :]) + mod_ref[0, 3:4, :]
        hb = h.astype(BF16)
        yield
        gate = _dot(hb, win_ref[:, 0:D_FF])
        yield
        up = _dot(hb, win_ref[:, D_FF:])
        yield
        y = _dot((_silu(gate) * up).astype(BF16), wout_ref[...])
        yield
        out_ref[rows, :] = x + mod_ref[0, 5:6, :] * _rms(y, gain_ref[3:4, :])

    _lock_step(sub_tile(slice(r0, r0 + SUB_TILE)) for r0 in range(0, x_ref.shape[0], SUB_TILE))


def _ffn(x, mod3, gains, w_in, w_out, *, mod_row_fn):
    n_tok = x.shape[0]
    tm = WIDE_TILE
    const = lambda i: (0, 0)
    tok = lambda i: (i, 0)
    return pl.pallas_call(
        _ffn_kernel,
        grid=(n_tok // tm,),
        in_specs=[pl.BlockSpec((tm, D_MODEL), tok),
                  pl.BlockSpec((1, 6, D_MODEL), lambda i: (mod_row_fn(i, tm), 0, 0)),
                  pl.BlockSpec((4, D_MODEL), const),
                  _resident((D_MODEL, 2 * D_FF)),
                  _resident((D_FF, D_MODEL))],
        out_specs=pl.BlockSpec((tm, D_MODEL), tok),
        out_shape=jax.ShapeDtypeStruct((n_tok, D_MODEL), F32),
        compiler_params=pltpu.CompilerParams(dimension_semantics=("arbitrary",),
                                             vmem_limit_bytes=VMEM_LIMIT),
        name="ffn",
    )(x, mod3, gains, w_in, w_out)


def _to_slab_order(src_ref, dst_ref):
    slab = 2 * DK
    left = lax.broadcasted_iota(jnp.int32, (src_ref.shape[1], slab), 1) < DK
    for p in range(H_A // 2):
        q_p = src_ref[0, :, p * slab:(p + 1) * slab]
        k_p = src_ref[0, :, W_A + p * slab:W_A + (p + 1) * slab]
        v_p = src_ref[0, :, 2 * W_A + p * slab:2 * W_A + (p + 1) * slab]
        k_swapped = pltpu.roll(k_p, DK, 1)
        dst_ref[:, p * slab:(p + 1) * slab] = pltpu.roll(q_p, DK, 1).astype(dst_ref.dtype)
        even = W_A + 2 * p * slab
        dst_ref[:, even:even + slab] = jnp.where(left, v_p, k_swapped).astype(dst_ref.dtype)
        dst_ref[:, even + slab:even + 2 * slab] = (
            jnp.where(left, k_swapped, v_p).astype(dst_ref.dtype))


def _split_w_in_kernel(w_ref, conv_ref, qkv_ref, z_ref, ba_ref, sc_ref, g_ref, conv_out_ref):
    _to_slab_order(w_ref, qkv_ref)
    _to_slab_order(conv_ref, conv_out_ref)
    o_z, o_ba = N_QKV, N_QKV + W_A
    o_sc = o_ba + 4 * H_A
    o_g = o_sc + 3 * W_B
    z_ref[...] = w_ref[0, :, o_z:o_ba].astype(BF16)
    ba_ref[...] = w_ref[0, :, o_ba:o_sc].astype(BF16)
    sc_ref[...] = w_ref[0, :, o_sc:o_g].astype(BF16)
    g_ref[...] = w_ref[0, :, o_g:].astype(BF16)


def _split_w_in(w, conv_qkv, layer):
    _, rows, n_in = w.shape
    taps = conv_qkv.shape[1]
    blk = 128
    widths = (N_QKV, W_A, 4 * H_A, 3 * W_B, 2 * D_MODEL)
    assert sum(widths) == n_in and rows % blk == 0
    return pl.pallas_call(
        _split_w_in_kernel,
        grid=(rows // blk,),
        in_specs=[pl.BlockSpec((1, blk, n_in), lambda i: (layer, i, 0)),
                  pl.BlockSpec((1, taps, N_QKV), lambda i: (layer, 0, 0))],
        out_specs=[pl.BlockSpec((blk, n), lambda i: (i, 0)) for n in widths]
        + [pl.BlockSpec((taps, N_QKV), lambda i: (0, 0))],
        out_shape=[jax.ShapeDtypeStruct((rows, n), BF16) for n in widths]
        + [jax.ShapeDtypeStruct((taps, N_QKV), conv_qkv.dtype)],
        name="split_w_in",
    )(w, conv_qkv)


def kernel(x_prompt, x_sample, state_delta, c, c_ctx, w_mod, b_mod, norm_gains, w_in, conv_qkv,
           a_log, dt_bias, onorm, conv_sc, w_branch_a, w_branch_b, w_out, w_ffn_in, w_ffn_out):
    depth = w_in.shape[0]
    assert depth == 1
    b_ctx, t_ctx, _ = x_prompt.shape
    b_lat, t_lat, _ = x_sample.shape
    assert SUB_TILE % t_ctx == 0 and SUB_TILE % GRID_W == 0 and t_ctx & (t_ctx - 1) == 0
    assert WIDE_TILE % SUB_TILE == 0 and t_lat % WIDE_TILE == 0
    assert (b_ctx * t_ctx) % WIDE_TILE == 0
    assert b_lat + 1 <= MOD_ROWS
    layer = 0

    cvec = jnp.zeros((MOD_ROWS, D_MODEL), F32).at[0].set(c_ctx).at[1:1 + b_lat].set(c)
    mod3 = _modulation(cvec, w_mod, b_mod, layer).reshape(MOD_ROWS, 6, D_MODEL)

    w_qkv, w_z, w_ba, w_sc, w_g, conv_perm = _split_w_in(w_in, conv_qkv, layer)
    w_bat = w_ba.T
    w_a = w_branch_a[layer].astype(BF16)
    w_b = w_branch_b[layer].astype(BF16)
    w_o = w_out[layer].astype(BF16)
    w_f1 = w_ffn_in[layer].astype(BF16)
    w_f2 = w_ffn_out[layer].astype(BF16)
    gains = norm_gains[layer]
    onorm_t = jnp.tile(onorm[layer], H_A)[None, :]

    def run_group(x3, s0, mod_row_fn, row_len, emit_state):
        batch, t, _ = x3.shape
        x = x3.reshape(batch * t, D_MODEL)
        qkv, ba, bat = _inproj(x, mod3, gains, w_qkv, w_ba, w_bat, conv_perm,
                               mod_row_fn=mod_row_fn, row_len=row_len)
        res = _delta_scan(qkv, ba, bat, a_log[layer], dt_bias[layer], s0,
                          batch=batch, emit_state=emit_state)
        o_f = res[0].reshape(batch * t, W_A)
        o_b = res[1].reshape(batch * t, W_A)
        x1 = _mix(x, mod3, gains, o_f, o_b, w_z, w_sc, w_g, conv_sc[layer], onorm_t,
                  w_a, w_b, w_o, mod_row_fn=mod_row_fn, row_len=row_len)
        x2 = _ffn(x1, mod3, gains, w_f1, w_f2, mod_row_fn=mod_row_fn)
        return x2.reshape(batch, t, D_MODEL), (res[2] if emit_state else None)

    y_p, s_fin = run_group(x_prompt, None, lambda i, tm: 0, t_ctx, True)
    y_s, _ = run_group(x_sample, state_delta[:, layer], lambda i, tm: 1 + (i * tm) // t_lat,
                       GRID_W, False)
    new_state = s_fin[:, None].astype(x_prompt.dtype)
    return (y_p, y_s, new_state)
```

```python
import functools

import jax
import jax.numpy as jnp
from jax import lax
from jax.experimental import pallas as pl
from jax.experimental.pallas import tpu as pltpu

D_MODEL = 1024
H_A = 8
DK = 64
DV = 64
W_A = H_A * DV
W_B = 512
CHUNK = 64
GRID_W = 64
EPS = 1e-6
N_QKV = 3 * H_A * DK
D_FF = 2816
SUBLANES = 8
MOD_ROWS = SUBLANES
WIDE_TILE = 512
SUB_TILE = 256
DELTA_BATCH = 4
VMEM_LIMIT = 56 * 1024 * 1024

F32 = jnp.float32
BF16 = jnp.bfloat16
HIGHEST = lax.Precision.HIGHEST


def _dot(a, b, precision=None):
    return jnp.dot(a, b, preferred_element_type=F32, precision=precision)


def _dot_nt(a, b, precision=None):
    return lax.dot_general(a, b, (((1,), (1,)), ((), ())), preferred_element_type=F32,
                           precision=precision)


def _dot_tn(a, b, precision=None):
    return lax.dot_general(a, b, (((0,), (0,)), ((), ())), preferred_element_type=F32,
                           precision=precision)


def _resident(shape):
    return pl.BlockSpec(shape, lambda i: (0,) * len(shape), pipeline_mode=pl.Buffered(1))


def _lock_step(programs):
    programs = list(programs)
    while programs:
        for prog in list(programs):
            try:
                next(prog)
            except StopIteration:
                programs.remove(prog)


def _silu(x):
    return x * jax.nn.sigmoid(x)


def _rms(x, gain):
    return x * lax.rsqrt(jnp.mean(x * x, axis=-1, keepdims=True) + EPS) * gain


def _conv3_rows(x, w_ref, row_len):
    n = x.shape[0]
    pos = lax.broadcasted_iota(jnp.int32, (n, 1), 0) & (row_len - 1)
    prev = jnp.where(pos == 0, 0.0, pltpu.roll(x, 1, 0))
    nxt = jnp.where(pos == row_len - 1, 0.0, pltpu.roll(x, n - 1, 0))
    return prev * w_ref[0:1, :] + x * w_ref[1:2, :] + nxt * w_ref[2:3, :]


def _mod_kernel(c_ref, w_ref, b_ref, o_ref):
    rows = c_ref.shape[0]
    s3 = jnp.concatenate([p.astype(F32) for p in _split3(_silu(c_ref[...]))], axis=0).astype(BF16)
    w = w_ref[0]
    w_hi = w.astype(BF16)
    w_lo = (w - w_hi.astype(F32)).astype(BF16)
    acc = _dot(s3, w_hi) + _dot(s3, w_lo)
    o_ref[...] = acc[0:rows] + acc[rows:2 * rows] + acc[2 * rows:] + b_ref[...]


def _modulation(cvec, w_mod, b_mod, layer):
    n_out = w_mod.shape[2]
    blk = D_MODEL
    return pl.pallas_call(
        _mod_kernel,
        grid=(n_out // blk,),
        in_specs=[pl.BlockSpec((MOD_ROWS, D_MODEL), lambda j: (0, 0)),
                  pl.BlockSpec((1, D_MODEL, blk), lambda j: (layer, 0, j)),
                  pl.BlockSpec((1, blk), lambda j: (layer, j))],
        out_specs=pl.BlockSpec((MOD_ROWS, blk), lambda j: (0, j)),
        out_shape=jax.ShapeDtypeStruct((MOD_ROWS, n_out), F32),
        name="modulation",
    )(cvec, w_mod, b_mod)


def _inproj_kernel(x_ref, mod_ref, gain_ref, wqkv_ref, wba_ref, wbat_ref, conv_ref,
                   qkv_ref, ba_ref, bat_ref, *, row_len):
    slab = 2 * DK
    lo = lax.broadcasted_iota(jnp.int32, (SUB_TILE, slab), 1) < DK
    ssq = lambda v, m: jnp.sum(jnp.where(m, v * v, 0.0), axis=-1, keepdims=True)

    def sub_tile(r0):
        rows = slice(r0, r0 + SUB_TILE)
        x = x_ref[rows, :]
        h = _rms(x, gain_ref[0:1, :]) * (1.0 + mod_ref[0, 1:2, :]) + mod_ref[0, 0:1, :]
        hb = h.astype(BF16)
        yield
        for pair in range(N_QKV // (2 * slab)):
            cols2 = slice(pair * 2 * slab, (pair + 1) * 2 * slab)
            v2 = _silu(_conv3_rows(_dot(hb, wqkv_ref[:, cols2]), conv_ref.at[:, cols2], row_len))
            for half in range(2):
                s = 2 * pair + half
                cols = slice(s * slab, (s + 1) * slab)
                v = v2[:, half * slab:(half + 1) * slab]
                if s < W_A // slab:
                    r = jnp.where(lo, lax.rsqrt(ssq(v, lo) + EPS) * (DK ** -0.5),
                                  lax.rsqrt(ssq(v, ~lo) + EPS) * (DK ** -0.5))
                    qkv_ref[rows, cols] = v * r
                else:
                    kmask = lo if (s - W_A // slab) % 2 else ~lo
                    qkv_ref[rows, cols] = v * jnp.where(kmask, lax.rsqrt(ssq(v, kmask) + EPS), 1.0)
            yield
        ba_ref[rows, :] = _dot(hb, wba_ref[...])
        for ci in range(SUB_TILE // CHUNK):
            bat_ref[r0 // CHUNK + ci] = _dot_nt(wbat_ref[...], hb[ci * CHUNK:(ci + 1) * CHUNK, :])

    _lock_step(sub_tile(r0) for r0 in range(0, x_ref.shape[0], SUB_TILE))


def _inproj(x, mod3, gains, w_qkv, w_ba, w_bat, conv_qkv, *, mod_row_fn, row_len):
    n_tok = x.shape[0]
    tm = WIDE_TILE
    const = lambda i: (0, 0)
    return pl.pallas_call(
        functools.partial(_inproj_kernel, row_len=row_len),
        grid=(n_tok // tm,),
        in_specs=[pl.BlockSpec((tm, D_MODEL), lambda i: (i, 0)),
                  pl.BlockSpec((1, 6, D_MODEL), lambda i: (mod_row_fn(i, tm), 0, 0)),
                  pl.BlockSpec((4, D_MODEL), const),
                  pl.BlockSpec((D_MODEL, N_QKV), const),
                  pl.BlockSpec((D_MODEL, 4 * H_A), const),
                  pl.BlockSpec((4 * H_A, D_MODEL), const),
                  pl.BlockSpec((3, N_QKV), const)],
        out_specs=[pl.BlockSpec((tm, N_QKV), lambda i: (i, 0)),
                   pl.BlockSpec((tm, 4 * H_A), lambda i: (i, 0)),
                   pl.BlockSpec((tm // CHUNK, 4 * H_A, CHUNK), lambda i: (i, 0, 0))],
        out_shape=[jax.ShapeDtypeStruct((n_tok, N_QKV), F32),
                   jax.ShapeDtypeStruct((n_tok, 4 * H_A), F32),
                   jax.ShapeDtypeStruct((n_tok // CHUNK, 4 * H_A, CHUNK), F32)],
        compiler_params=pltpu.CompilerParams(dimension_semantics=("arbitrary",),
                                             vmem_limit_bytes=VMEM_LIMIT),
        name="inproj",
    )(x, mod3, gains, w_qkv, w_ba, w_bat, conv_qkv)


def _unit_tri_inverse_stages(lows, lower):
    c = CHUNK
    ri = lax.broadcasted_iota(jnp.int32, (c, c), 0)
    ci = lax.broadcasted_iota(jnp.int32, (c, c), 1)
    eye = (ri == ci).astype(F32)
    rng = range(len(lows))
    joins = lambda b: ((ri // (2 * b)) == (ci // (2 * b))) & ((ri // b) != (ci // b))
    t = [eye - jnp.where(joins(1), lows[i], 0.0) for i in rng]
    b = 2
    while b < c:
        mask = joins(b)
        t16 = [t[i].astype(BF16) for i in rng]
        e16 = [jnp.where(mask, lows[i], 0.0).astype(BF16) for i in rng]
        if b % SUBLANES:
            x = [_dot(t16[i], e16[i]) for i in rng]
            yield
            t = [t[i] - _dot(x[i].astype(BF16), t16[i]) for i in rng]
            yield
        else:
            first = [slice(s, s + b) for s in range(0, c, 2 * b)]
            second = [slice(s + b, s + 2 * b) for s in range(0, c, 2 * b)]
            keep, upd = (first, second) if lower else (second, first)
            t_upd = [jnp.concatenate([t[i][r] for r in upd], axis=0) for i in rng]
            x = [_dot(t_upd[i].astype(BF16), e16[i]) for i in rng]
            yield
            t_upd = [t_upd[i] - _dot(x[i].astype(BF16), t16[i]) for i in rng]
            yield
            pair = lambda kept, new: (kept, new) if lower else (new, kept)
            t = [jnp.concatenate(
                [piece for k, r in enumerate(keep)
                 for piece in pair(t[i][r], t_upd[i][k * b:(k + 1) * b])], axis=0) for i in rng]
        b *= 2
    return t


def _split3(x):
    pieces = []
    for _ in range(3):
        p = x.astype(BF16)
        x = x - p.astype(F32)
        pieces.append(p)
    return pieces


def _group_stages(qkv_ref, o_ref, s_ref, bb, d, gates):
    c = CHUNK
    fwd = d == 0
    row = lax.broadcasted_iota(jnp.int32, (c, c), 0)
    col = lax.broadcasted_iota(jnp.int32, (c, c), 1)
    incl = (row >= col) if fwd else (row <= col)
    strict = (row > col) if fwd else (row < col)
    lane = lax.broadcasted_iota(jnp.int32, (c, 2 * DK), 1)
    beta_all, beta_all_t, gc_all, gc_all_t, eg_all, tail_all, eglast_all = gates
    heads = range(H_A)
    kmask = [(lane < DK) if h % 2 else (lane >= DK) for h in heads]
    kv_slab = lambda h: qkv_ref[bb, :, W_A + h * 2 * DK:W_A + (h + 1) * 2 * DK]
    q_slab = lambda h: qkv_ref[bb, :, (h // 2) * 2 * DK:(h // 2 + 1) * 2 * DK]

    decay, kq = [], []
    for h in heads:
        j = d * H_A + h
        kv = kv_slab(h)
        lhs16 = jnp.concatenate([kv * beta_all[:, j:j + 1], q_slab(h)], axis=0).astype(BF16)
        kq.append(_dot_nt(lhs16, jnp.where(kmask[h], kv, 0.0).astype(BF16)))
        diff = gc_all[:, j:j + 1] - gc_all_t[j:j + 1, :]
        decay.append(jnp.where(incl, jnp.exp(jnp.where(incl, diff, 0.0)), 0.0))
    yield

    lows = [jnp.where(strict, kq[h][:c] * decay[h], 0.0) for h in heads]
    attn16 = [(kq[h][c:] * decay[h]).astype(BF16) for h in heads]
    tmat = yield from _unit_tri_inverse_stages(lows, lower=fwd)

    uw, eg = [], []
    for h in heads:
        j = d * H_A + h
        eg.append(jnp.broadcast_to(eg_all[:, j:j + 1], (c, 2 * DK)))
        kv = kv_slab(h)
        t_beta16 = (tmat[h] * beta_all_t[j:j + 1, :]).astype(BF16)
        uw.append(_dot(t_beta16, jnp.where(kmask[h], kv * eg[h], kv).astype(BF16)))
    yield

    pairs = range(H_A // 2)
    s_old = [s_ref[bb, d, p] for p in pairs]
    qh = [q_slab(2 * p) * jnp.where(kmask[2 * p + 1], eg[2 * p + 1], eg[2 * p]) for p in pairs]
    wq = [_dot(jnp.concatenate([uw[2 * p], uw[2 * p + 1], qh[p]], axis=0).astype(BF16),
               s_old[p].astype(BF16)) for p in pairs]
    yield

    left = kmask[1]
    v_pair16 = [jnp.where(left, uw[2 * p] - wq[p][:c], uw[2 * p + 1] - wq[p][c:2 * c])
                .astype(BF16) for p in pairs]
    row2 = lax.broadcasted_iota(jnp.int32, (2 * c, 2 * DK), 0)
    lane2 = lax.broadcasted_iota(jnp.int32, (2 * c, 2 * DK), 1)
    own_block = (row2 >= c) == (lane2 < DK)
    for p in pairs:
        je, jo = d * H_A + 2 * p, d * H_A + 2 * p + 1
        av = _dot(jnp.concatenate([attn16[2 * p], attn16[2 * p + 1]], axis=0), v_pair16[p])
        o_ref[bb, :, 2 * p * DV:(2 * p + 2) * DV] = (
            wq[p][2 * c:] + jnp.where(left, av[:c], av[c:]))
        kt16 = jnp.where(left, kv_slab(2 * p + 1) * tail_all[:, jo:jo + 1],
                         kv_slab(2 * p) * tail_all[:, je:je + 1]).astype(BF16)
        decay = jnp.where(row2 >= c, eglast_all[:, je:je + 1], eglast_all[:, jo:jo + 1])
        s_ref[bb, d, p] = s_old[p] * decay + jnp.where(
            own_block, _dot_tn(kt16, v_pair16[p]), 0.0)


def _delta_kernel(*refs, zero_init, emit_state):
    (qkv_f_ref, qkv_b_ref, ba_f_ref, ba_b_ref, bat_f_ref, bat_b_ref,
     alog_ref, dtb_ref, alog_t_ref, dtb_t_ref) = refs[:10]
    rest = list(refs[10:])
    s0_ref = None if zero_init else rest.pop(0)
    o_f_ref, o_b_ref = rest.pop(0), rest.pop(0)
    sfin_ref = rest.pop(0) if emit_state else None
    s_ref = rest.pop(0)

    n = pl.program_id(1)
    n_chunks = pl.num_programs(1)
    nb = s_ref.shape[0]
    state_at = lambda h: (0, DV) if h % 2 else (DK, 0)

    @pl.when(n == 0)
    def _():
        s_ref[...] = jnp.zeros(s_ref.shape, F32)
        if not zero_init:
            for bb in range(nb):
                for d in range(2):
                    for h in range(H_A):
                        r0, l0 = state_at(h)
                        s_ref[bb, d, h // 2, r0:r0 + DK, l0:l0 + DV] = s0_ref[bb, d, h]

    c = CHUNK
    row = lax.broadcasted_iota(jnp.int32, (c, c), 0)
    col = lax.broadcasted_iota(jnp.int32, (c, c), 1)
    tril16 = (row >= col).astype(BF16)
    triu16 = (row <= col).astype(BF16)

    programs = []
    for bb in range(nb):
        for d, (qkv_ref, ba_ref, bat_ref, o_ref) in enumerate(
                ((qkv_f_ref, ba_f_ref, bat_f_ref, o_f_ref),
                 (qkv_b_ref, ba_b_ref, bat_b_ref, o_b_ref))):
            fwd = d == 0
            csum16, csum_t16 = (tril16, triu16) if fwd else (triu16, tril16)
            last = c - 1 if fwd else 0
            ba = ba_ref[bb]
            bat = bat_ref[bb, 0]
            beta_all = jax.nn.sigmoid(ba[:, 0:2 * H_A])
            g_all = -jnp.exp(alog_ref[...]) * jax.nn.softplus(ba[:, 2 * H_A:] + dtb_ref[...])
            g_all_t = -jnp.exp(alog_t_ref[...]) * jax.nn.softplus(
                bat[2 * H_A:, :] + dtb_t_ref[...])
            gc_all = sum(_dot(csum16, p) for p in _split3(g_all))
            gc_all_t = sum(_dot(p, csum_t16) for p in _split3(g_all_t))
            glast_all = gc_all[last:last + 1, :]
            beta_all_t = jax.nn.sigmoid(bat[0:2 * H_A, :])
            gates = (beta_all, beta_all_t, gc_all, gc_all_t, jnp.exp(gc_all),
                     jnp.exp(glast_all - gc_all), jnp.exp(glast_all))
            programs.append(_group_stages(qkv_ref, o_ref, s_ref, bb, d, gates))

    _lock_step(programs)

    if emit_state:
        @pl.when(n == n_chunks - 1)
        def _():
            for bb in range(nb):
                for d in range(2):
                    for h in range(H_A):
                        r0, l0 = state_at(h)
                        sfin_ref[bb, d, h] = s_ref[bb, d, h // 2, r0:r0 + DK, l0:l0 + DV]


def _delta_scan(qkv, ba, bat, a_log, dt_bias, s0, *, batch, emit_state):
    n_tok = qkv.shape[0]
    t = n_tok // batch
    nc = t // CHUNK
    nb = min(DELTA_BATCH, batch)
    assert batch % nb == 0
    qkv = qkv.reshape(batch, t, N_QKV)
    ba = ba.reshape(batch, t, 4 * H_A)
    bat = bat.reshape(batch, nc, 4 * H_A, CHUNK)
    zero_init = s0 is None
    alog = a_log.reshape(1, 2 * H_A)
    dtb = dt_bias.reshape(1, 2 * H_A)
    alog_t = a_log.reshape(2 * H_A, 1)
    dtb_t = dt_bias.reshape(2 * H_A, 1)

    fw = lambda b, n: (b, n, 0)
    bw = lambda b, n: (b, nc - 1 - n, 0)
    small = lambda b, n: (0, 0)
    in_specs = [pl.BlockSpec((nb, CHUNK, N_QKV), fw),
                pl.BlockSpec((nb, CHUNK, N_QKV), bw),
                pl.BlockSpec((nb, CHUNK, 4 * H_A), fw),
                pl.BlockSpec((nb, CHUNK, 4 * H_A), bw),
                pl.BlockSpec((nb, 1, 4 * H_A, CHUNK), lambda b, n: (b, n, 0, 0)),
                pl.BlockSpec((nb, 1, 4 * H_A, CHUNK), lambda b, n: (b, nc - 1 - n, 0, 0)),
                pl.BlockSpec((1, 2 * H_A), small),
                pl.BlockSpec((1, 2 * H_A), small),
                pl.BlockSpec((2 * H_A, 1), small),
                pl.BlockSpec((2 * H_A, 1), small)]
    args = [qkv, qkv, ba, ba, bat, bat, alog, dtb, alog_t, dtb_t]
    state_spec = pl.BlockSpec((nb, 2, H_A, DK, DV), lambda b, n: (b, 0, 0, 0, 0))
    if not zero_init:
        in_specs.append(state_spec)
        args.append(s0)
    out_specs = [pl.BlockSpec((nb, CHUNK, W_A), fw), pl.BlockSpec((nb, CHUNK, W_A), bw)]
    out_shape = [jax.ShapeDtypeStruct((batch, t, W_A), F32),
                 jax.ShapeDtypeStruct((batch, t, W_A), F32)]
    if emit_state:
        out_specs.append(state_spec)
        out_shape.append(jax.ShapeDtypeStruct((batch, 2, H_A, DK, DV), F32))
    return pl.pallas_call(
        functools.partial(_delta_kernel, zero_init=zero_init, emit_state=emit_state),
        grid=(batch // nb, nc),
        in_specs=in_specs,
        out_specs=out_specs,
        out_shape=out_shape,
        scratch_shapes=[pltpu.VMEM((nb, 2, H_A // 2, 2 * DK, 2 * DV), F32)],
        compiler_params=pltpu.CompilerParams(dimension_semantics=("arbitrary", "arbitrary"),
                                             vmem_limit_bytes=VMEM_LIMIT),
        name="delta_scan",
    )(*args)


def _mix_kernel(x_ref, mod_ref, gain_ref, of_ref, ob_ref, wz_ref, wsc_ref, wg_ref, conv_ref,
                onorm_ref, wa_ref, wb_ref, wo_ref, out_ref, *, row_len):
    def sub_tile(rows):
        x = x_ref[rows, :]
        h = _rms(x, gain_ref[0:1, :]) * (1.0 + mod_ref[0, 1:2, :]) + mod_ref[0, 0:1, :]
        hb = h.astype(BF16)
        yield
        sc = _dot(hb, wsc_ref[...])
        o = of_ref[rows, :] + ob_ref[rows, :]
        slab = 2 * DV
        lo = lax.broadcasted_iota(jnp.int32, (SUB_TILE, slab), 1) < DV
        inv_rms = []
        for s in range(W_A // slab):
            o2 = o[:, s * slab:(s + 1) * slab]
            o2 = o2 * o2
            ms_lo = jnp.sum(jnp.where(lo, o2, 0.0), axis=-1, keepdims=True) * (1.0 / DV)
            ms_hi = jnp.sum(jnp.where(lo, 0.0, o2), axis=-1, keepdims=True) * (1.0 / DV)
            inv_rms.append(jnp.where(lo, lax.rsqrt(ms_lo + EPS), lax.rsqrt(ms_hi + EPS)))
        yield
        z = _dot(hb, wz_ref[...])
        bg, cg, xs = sc[:, 0:W_B], sc[:, W_B:2 * W_B], sc[:, 2 * W_B:3 * W_B]
        yb_in16 = (bg * _conv3_rows(cg * xs, conv_ref, row_len)).astype(BF16)
        yield
        gates = _dot(hb, wg_ref[...])
        oa16 = (o * jnp.concatenate(inv_rms, axis=-1) * onorm_ref[...] * _silu(z)).astype(BF16)
        yield
        y_a = _dot(oa16, wa_ref[...])
        y_b = _dot(yb_in16, wb_ref[...])
        yield
        m = (jax.nn.sigmoid(gates[:, 0:D_MODEL]) * y_a
             + jax.nn.sigmoid(gates[:, D_MODEL:]) * y_b)
        y = _dot(m.astype(BF16), wo_ref[...])
        yield
        out_ref[rows, :] = x + mod_ref[0, 2:3, :] * _rms(y, gain_ref[1:2, :])

    _lock_step(sub_tile(slice(r0, r0 + SUB_TILE)) for r0 in range(0, x_ref.shape[0], SUB_TILE))


def _mix(x, mod3, gains, o_f, o_b, w_z, w_sc, w_g, conv_sc, onorm_t, w_a, w_b, w_o,
         *, mod_row_fn, row_len):
    n_tok = x.shape[0]
    tm = WIDE_TILE
    const = lambda i: (0, 0)
    tok = lambda i: (i, 0)
    return pl.pallas_call(
        functools.partial(_mix_kernel, row_len=row_len),
        grid=(n_tok // tm,),
        in_specs=[pl.BlockSpec((tm, D_MODEL), tok),
                  pl.BlockSpec((1, 6, D_MODEL), lambda i: (mod_row_fn(i, tm), 0, 0)),
                  pl.BlockSpec((4, D_MODEL), const),
                  pl.BlockSpec((tm, W_A), tok),
                  pl.BlockSpec((tm, W_A), tok),
                  _resident((D_MODEL, W_A)),
                  _resident((D_MODEL, 3 * W_B)),
                  _resident((D_MODEL, 2 * D_MODEL)),
                  pl.BlockSpec((3, W_B), const),
                  pl.BlockSpec((1, W_A), const),
                  _resident((W_A, D_MODEL)),
                  _resident((W_B, D_MODEL)),
                  _resident((D_MODEL, D_MODEL))],
        out_specs=pl.BlockSpec((tm, D_MODEL), tok),
        out_shape=jax.ShapeDtypeStruct((n_tok, D_MODEL), F32),
        compiler_params=pltpu.CompilerParams(dimension_semantics=("arbitrary",),
                                             vmem_limit_bytes=VMEM_LIMIT),
        name="mix",
    )(x, mod3, gains, o_f, o_b, w_z, w_sc, w_g, conv_sc, onorm_t, w_a, w_b, w_o)


def _ffn_kernel(x_ref, mod_ref, gain_ref, win_ref, wout_ref, out_ref):
    def sub_tile(rows):
        x = x_ref[rows, :]
        h = _rms(x, gain_ref[2:3, :]) * (1.0 + mod_ref[0, 4:5, :]) + mod_ref[0, 3:4, :]
        hb = h.astype(BF16)
        yield
        gate = _dot(hb, win_ref[:, 0:D_FF])
        yield
        up = _dot(hb, win_ref[:, D_FF:])
        yield
        y = _dot((_silu(gate) * up).astype(BF16), wout_ref[...])
        yield
        out_ref[rows, :] = x + mod_ref[0, 5:6, :] * _rms(y, gain_ref[3:4, :])

    _lock_step(sub_tile(slice(r0, r0 + SUB_TILE)) for r0 in range(0, x_ref.shape[0], SUB_TILE))


def _ffn(x, mod3, gains, w_in, w_out, *, mod_row_fn):
    n_tok = x.shape[0]
    tm = WIDE_TILE
    const = lambda i: (0, 0)
    tok = lambda i: (i, 0)
    return pl.pallas_call(
        _ffn_kernel,
        grid=(n_tok // tm,),
        in_specs=[pl.BlockSpec((tm, D_MODEL), tok),
                  pl.BlockSpec((1, 6, D_MODEL), lambda i: (mod_row_fn(i, tm), 0, 0)),
                  pl.BlockSpec((4, D_MODEL), const),
                  _resident((D_MODEL, 2 * D_FF)),
                  _resident((D_FF, D_MODEL))],
        out_specs=pl.BlockSpec((tm, D_MODEL), tok),
        out_shape=jax.ShapeDtypeStruct((n_tok, D_MODEL), F32),
        compiler_params=pltpu.CompilerParams(dimension_semantics=("arbitrary",),
                                             vmem_limit_bytes=VMEM_LIMIT),
        name="ffn",
    )(x, mod3, gains, w_in, w_out)


def _to_slab_order(src_ref, dst_ref):
    slab = 2 * DK
    left = lax.broadcasted_iota(jnp.int32, (src_ref.shape[1], slab), 1) < DK
    for p in range(H_A // 2):
        q_p = src_ref[0, :, p * slab:(p + 1) * slab]
        k_p = src_ref[0, :, W_A + p * slab:W_A + (p + 1) * slab]
        v_p = src_ref[0, :, 2 * W_A + p * slab:2 * W_A + (p + 1) * slab]
        k_swapped = pltpu.roll(k_p, DK, 1)
        dst_ref[:, p * slab:(p + 1) * slab] = pltpu.roll(q_p, DK, 1).astype(dst_ref.dtype)
        even = W_A + 2 * p * slab
        dst_ref[:, even:even + slab] = jnp.where(left, v_p, k_swapped).astype(dst_ref.dtype)
        dst_ref[:, even + slab:even + 2 * slab] = (
            jnp.where(left, k_swapped, v_p).astype(dst_ref.dtype))


def _split_w_in_kernel(w_ref, conv_ref, qkv_ref, z_ref, ba_ref, sc_ref, g_ref, conv_out_ref):
    _to_slab_order(w_ref, qkv_ref)
    _to_slab_order(conv_ref, conv_out_ref)
    o_z, o_ba = N_QKV, N_QKV + W_A
    o_sc = o_ba + 4 * H_A
    o_g = o_sc + 3 * W_B
    z_ref[...] = w_ref[0, :, o_z:o_ba].astype(BF16)
    ba_ref[...] = w_ref[0, :, o_ba:o_sc].astype(BF16)
    sc_ref[...] = w_ref[0, :, o_sc:o_g].astype(BF16)
    g_ref[...] = w_ref[0, :, o_g:].astype(BF16)


def _split_w_in(w, conv_qkv, layer):
    _, rows, n_in = w.shape
    taps = conv_qkv.shape[1]
    blk = 128
    widths = (N_QKV, W_A, 4 * H_A, 3 * W_B, 2 * D_MODEL)
    assert sum(widths) == n_in and rows % blk == 0
    return pl.pallas_call(
        _split_w_in_kernel,
        grid=(rows // blk,),
        in_specs=[pl.BlockSpec((1, blk, n_in), lambda i: (layer, i, 0)),
                  pl.BlockSpec((1, taps, N_QKV), lambda i: (layer, 0, 0))],
        out_specs=[pl.BlockSpec((blk, n), lambda i: (i, 0)) for n in widths]
        + [pl.BlockSpec((taps, N_QKV), lambda i: (0, 0))],
        out_shape=[jax.ShapeDtypeStruct((rows, n), BF16) for n in widths]
        + [jax.ShapeDtypeStruct((taps, N_QKV), conv_qkv.dtype)],
        name="split_w_in",
    )(w, conv_qkv)


def kernel(x_prompt, x_sample, state_delta, c, c_ctx, w_mod, b_mod, norm_gains, w_in, conv_qkv,
           a_log, dt_bias, onorm, conv_sc, w_branch_a, w_branch_b, w_out, w_ffn_in, w_ffn_out):
    depth = w_in.shape[0]
    assert depth == 1
    b_ctx, t_ctx, _ = x_prompt.shape
    b_lat, t_lat, _ = x_sample.shape
    assert SUB_TILE % t_ctx == 0 and SUB_TILE % GRID_W == 0 and t_ctx & (t_ctx - 1) == 0
    assert WIDE_TILE % SUB_TILE == 0 and t_lat % WIDE_TILE == 0
    assert (b_ctx * t_ctx) % WIDE_TILE == 0
    assert b_lat + 1 <= MOD_ROWS
    layer = 0

    cvec = jnp.zeros((MOD_ROWS, D_MODEL), F32).at[0].set(c_ctx).at[1:1 + b_lat].set(c)
    mod3 = _modulation(cvec, w_mod, b_mod, layer).reshape(MOD_ROWS, 6, D_MODEL)

    w_qkv, w_z, w_ba, w_sc, w_g, conv_perm = _split_w_in(w_in, conv_qkv, layer)
    w_bat = w_ba.T
    w_a = w_branch_a[layer].astype(BF16)
    w_b = w_branch_b[layer].astype(BF16)
    w_o = w_out[layer].astype(BF16)
    w_f1 = w_ffn_in[layer].astype(BF16)
    w_f2 = w_ffn_out[layer].astype(BF16)
    gains = norm_gains[layer]
    onorm_t = jnp.tile(onorm[layer], H_A)[None, :]

    def run_group(x3, s0, mod_row_fn, row_len, emit_state):
        batch, t, _ = x3.shape
        x = x3.reshape(batch * t, D_MODEL)
        qkv, ba, bat = _inproj(x, mod3, gains, w_qkv, w_ba, w_bat, conv_perm,
                               mod_row_fn=mod_row_fn, row_len=row_len)
        res = _delta_scan(qkv, ba, bat, a_log[layer], dt_bias[layer], s0,
                          batch=batch, emit_state=emit_state)
        o_f = res[0].reshape(batch * t, W_A)
        o_b = res[1].reshape(batch * t, W_A)
        x1 = _mix(x, mod3, gains, o_f, o_b, w_z, w_sc, w_g, conv_sc[layer], onorm_t,
                  w_a, w_b, w_o, mod_row_fn=mod_row_fn, row_len=row_len)
        x2 = _ffn(x1, mod3, gains, w_f1, w_f2, mod_row_fn=mod_row_fn)
        return x2.reshape(batch, t, D_MODEL), (res[2] if emit_state else None)

    y_p, s_fin = run_group(x_prompt, None, lambda i, tm: 0, t_ctx, True)
    y_s, _ = run_group(x_sample, state_delta[:, layer], lambda i, tm: 1 + (i * tm) // t_lat,
                       GRID_W, False)
    new_state = s_fin[:, None].astype(x_prompt.dtype)
    return (y_p, y_s, new_state)
```

```python
import functools

import jax
import jax.numpy as jnp
from jax import lax
from jax.experimental import pallas as pl
from jax.experimental.pallas import tpu as pltpu

D_MODEL = 1024
H_A = 8
DK = 64
DV = 64
W_A = H_A * DV
W_B = 512
CHUNK = 64
GRID_W = 64
EPS = 1e-6
N_QKV = 3 * H_A * DK
D_FF = 2816
SUBLANES = 8
MOD_ROWS = SUBLANES
WIDE_TILE = 512
SUB_TILE = 256
DELTA_BATCH = 4
VMEM_LIMIT = 56 * 1024 * 1024

F32 = jnp.float32
BF16 = jnp.bfloat16
HIGHEST = lax.Precision.HIGHEST


def _dot(a, b, precision=None):
    return jnp.dot(a, b, preferred_element_type=F32, precision=precision)


def _dot_nt(a, b, precision=None):
    return lax.dot_general(a, b, (((1,), (1,)), ((), ())), preferred_element_type=F32,
                           precision=precision)


def _dot_tn(a, b, precision=None):
    return lax.dot_general(a, b, (((0,), (0,)), ((), ())), preferred_element_type=F32,
                           precision=precision)


def _resident(shape):
    return pl.BlockSpec(shape, lambda i: (0,) * len(shape), pipeline_mode=pl.Buffered(1))


def _lock_step(programs):
    programs = list(programs)
    while programs:
        for prog in list(programs):
            try:
                next(prog)
            except StopIteration:
                programs.remove(prog)


def _silu(x):
    return x * jax.nn.sigmoid(x)


def _rms(x, gain):
    return x * lax.rsqrt(jnp.mean(x * x, axis=-1, keepdims=True) + EPS) * gain


def _conv3_rows(x, w_ref, row_len):
    n = x.shape[0]
    pos = lax.broadcasted_iota(jnp.int32, (n, 1), 0) & (row_len - 1)
    prev = jnp.where(pos == 0, 0.0, pltpu.roll(x, 1, 0))
    nxt = jnp.where(pos == row_len - 1, 0.0, pltpu.roll(x, n - 1, 0))
    return prev * w_ref[0:1, :] + x * w_ref[1:2, :] + nxt * w_ref[2:3, :]


def _mod_kernel(c_ref, w_ref, b_ref, o_ref):
    rows = c_ref.shape[0]
    s3 = jnp.concatenate([p.astype(F32) for p in _split3(_silu(c_ref[...]))], axis=0).astype(BF16)
    w = w_ref[0]
    w_hi = w.astype(BF16)
    w_lo = (w - w_hi.astype(F32)).astype(BF16)
    acc = _dot(s3, w_hi) + _dot(s3, w_lo)
    o_ref[...] = acc[0:rows] + acc[rows:2 * rows] + acc[2 * rows:] + b_ref[...]


def _modulation(cvec, w_mod, b_mod, layer):
    n_out = w_mod.shape[2]
    blk = D_MODEL
    return pl.pallas_call(
        _mod_kernel,
        grid=(n_out // blk,),
        in_specs=[pl.BlockSpec((MOD_ROWS, D_MODEL), lambda j: (0, 0)),
                  pl.BlockSpec((1, D_MODEL, blk), lambda j: (layer, 0, j)),
                  pl.BlockSpec((1, blk), lambda j: (layer, j))],
        out_specs=pl.BlockSpec((MOD_ROWS, blk), lambda j: (0, j)),
        out_shape=jax.ShapeDtypeStruct((MOD_ROWS, n_out), F32),
        name="modulation",
    )(cvec, w_mod, b_mod)


def _inproj_kernel(x_ref, mod_ref, gain_ref, wqkv_ref, wba_ref, wbat_ref, conv_ref,
                   qkv_ref, ba_ref, bat_ref, *, row_len):
    slab = 2 * DK
    lo = lax.broadcasted_iota(jnp.int32, (SUB_TILE, slab), 1) < DK
    ssq = lambda v, m: jnp.sum(jnp.where(m, v * v, 0.0), axis=-1, keepdims=True)

    def sub_tile(r0):
        rows = slice(r0, r0 + SUB_TILE)
        x = x_ref[rows, :]
        h = _rms(x, gain_ref[0:1, :]) * (1.0 + mod_ref[0, 1:2, :]) + mod_ref[0, 0:1, :]
        hb = h.astype(BF16)
        yield
        for pair in range(N_QKV // (2 * slab)):
            cols2 = slice(pair * 2 * slab, (pair + 1) * 2 * slab)
            v2 = _silu(_conv3_rows(_dot(hb, wqkv_ref[:, cols2]), conv_ref.at[:, cols2], row_len))
            for half in range(2):
                s = 2 * pair + half
                cols = slice(s * slab, (s + 1) * slab)
                v = v2[:, half * slab:(half + 1) * slab]
                if s < W_A // slab:
                    r = jnp.where(lo, lax.rsqrt(ssq(v, lo) + EPS) * (DK ** -0.5),
                                  lax.rsqrt(ssq(v, ~lo) + EPS) * (DK ** -0.5))
                    qkv_ref[rows, cols] = v * r
                else:
                    kmask = lo if (s - W_A // slab) % 2 else ~lo
                    qkv_ref[rows, cols] = v * jnp.where(kmask, lax.rsqrt(ssq(v, kmask) + EPS), 1.0)
            yield
        ba_ref[rows, :] = _dot(hb, wba_ref[...])
        for ci in range(SUB_TILE // CHUNK):
            bat_ref[r0 // CHUNK + ci] = _dot_nt(wbat_ref[...], hb[ci * CHUNK:(ci + 1) * CHUNK, :])

    _lock_step(sub_tile(r0) for r0 in range(0, x_ref.shape[0], SUB_TILE))


def _inproj(x, mod3, gains, w_qkv, w_ba, w_bat, conv_qkv, *, mod_row_fn, row_len):
    n_tok = x.shape[0]
    tm = WIDE_TILE
    const = lambda i: (0, 0)
    return pl.pallas_call(
        functools.partial(_inproj_kernel, row_len=row_len),
        grid=(n_tok // tm,),
        in_specs=[pl.BlockSpec((tm, D_MODEL), lambda i: (i, 0)),
                  pl.BlockSpec((1, 6, D_MODEL), lambda i: (mod_row_fn(i, tm), 0, 0)),
                  pl.BlockSpec((4, D_MODEL), const),
                  pl.BlockSpec((D_MODEL, N_QKV), const),
                  pl.BlockSpec((D_MODEL, 4 * H_A), const),
                  pl.BlockSpec((4 * H_A, D_MODEL), const),
                  pl.BlockSpec((3, N_QKV), const)],
        out_specs=[pl.BlockSpec((tm, N_QKV), lambda i: (i, 0)),
                   pl.BlockSpec((tm, 4 * H_A), lambda i: (i, 0)),
                   pl.BlockSpec((tm // CHUNK, 4 * H_A, CHUNK), lambda i: (i, 0, 0))],
        out_shape=[jax.ShapeDtypeStruct((n_tok, N_QKV), F32),
                   jax.ShapeDtypeStruct((n_tok, 4 * H_A), F32),
                   jax.ShapeDtypeStruct((n_tok // CHUNK, 4 * H_A, CHUNK), F32)],
        compiler_params=pltpu.CompilerParams(dimension_semantics=("arbitrary",),
                                             vmem_limit_bytes=VMEM_LIMIT),
        name="inproj",
    )(x, mod3, gains, w_qkv, w_ba, w_bat, conv_qkv)


def _unit_tri_inverse_stages(lows, lower):
    c = CHUNK
    ri = lax.broadcasted_iota(jnp.int32, (c, c), 0)
    ci = lax.broadcasted_iota(jnp.int32, (c, c), 1)
    eye = (ri == ci).astype(F32)
    rng = range(len(lows))
    joins = lambda b: ((ri // (2 * b)) == (ci // (2 * b))) & ((ri // b) != (ci // b))
    t = [eye - jnp.where(joins(1), lows[i], 0.0) for i in rng]
    b = 2
    while b < c:
        mask = joins(b)
        t16 = [t[i].astype(BF16) for i in rng]
        e16 = [jnp.where(mask, lows[i], 0.0).astype(BF16) for i in rng]
        if b % SUBLANES:
            x = [_dot(t16[i], e16[i]) for i in rng]
            yield
            t = [t[i] - _dot(x[i].astype(BF16), t16[i]) for i in rng]
            yield
        else:
            first = [slice(s, s + b) for s in range(0, c, 2 * b)]
            second = [slice(s + b, s + 2 * b) for s in range(0, c, 2 * b)]
            keep, upd = (first, second) if lower else (second, first)
            t_upd = [jnp.concatenate([t[i][r] for r in upd], axis=0) for i in rng]
            x = [_dot(t_upd[i].astype(BF16), e16[i]) for i in rng]
            yield
            t_upd = [t_upd[i] - _dot(x[i].astype(BF16), t16[i]) for i in rng]
            yield
            pair = lambda kept, new: (kept, new) if lower else (new, kept)
            t = [jnp.concatenate(
                [piece for k, r in enumerate(keep)
                 for piece in pair(t[i][r], t_upd[i][k * b:(k + 1) * b])], axis=0) for i in rng]
        b *= 2
    return t


def _split3(x):
    pieces = []
    for _ in range(3):
        p = x.astype(BF16)
        x = x - p.astype(F32)
        pieces.append(p)
    return pieces


def _group_stages(qkv_ref, o_ref, s_ref, bb, d, gates):
    c = CHUNK
    fwd = d == 0
    row = lax.broadcasted_iota(jnp.int32, (c, c), 0)
    col = lax.broadcasted_iota(jnp.int32, (c, c), 1)
    incl = (row >= col) if fwd else (row <= col)
    strict = (row > col) if fwd else (row < col)
    lane = lax.broadcasted_iota(jnp.int32, (c, 2 * DK), 1)
    beta_all, beta_all_t, gc_all, gc_all_t, eg_all, tail_all, eglast_all = gates
    heads = range(H_A)
    kmask = [(lane < DK) if h % 2 else (lane >= DK) for h in heads]
    kv_slab = lambda h: qkv_ref[bb, :, W_A + h * 2 * DK:W_A + (h + 1) * 2 * DK]
    q_slab = lambda h: qkv_ref[bb, :, (h // 2) * 2 * DK:(h // 2 + 1) * 2 * DK]

    decay, kq = [], []
    for h in heads:
        j = d * H_A + h
        kv = kv_slab(h)
        lhs16 = jnp.concatenate([kv * beta_all[:, j:j + 1], q_slab(h)], axis=0).astype(BF16)
        kq.append(_dot_nt(lhs16, jnp.where(kmask[h], kv, 0.0).astype(BF16)))
        diff = gc_all[:, j:j + 1] - gc_all_t[j:j + 1, :]
        decay.append(jnp.where(incl, jnp.exp(jnp.where(incl, diff, 0.0)), 0.0))
    yield

    lows = [jnp.where(strict, kq[h][:c] * decay[h], 0.0) for h in heads]
    attn16 = [(kq[h][c:] * decay[h]).astype(BF16) for h in heads]
    tmat = yield from _unit_tri_inverse_stages(lows, lower=fwd)

    uw, eg = [], []
    for h in heads:
        j = d * H_A + h
        eg.append(jnp.broadcast_to(eg_all[:, j:j + 1], (c, 2 * DK)))
        kv = kv_slab(h)
        t_beta16 = (tmat[h] * beta_all_t[j:j + 1, :]).astype(BF16)
        uw.append(_dot(t_beta16, jnp.where(kmask[h], kv * eg[h], kv).astype(BF16)))
    yield

    pairs = range(H_A // 2)
    s_old = [s_ref[bb, d, p] for p in pairs]
    qh = [q_slab(2 * p) * jnp.where(kmask[2 * p + 1], eg[2 * p + 1], eg[2 * p]) for p in pairs]
    wq = [_dot(jnp.concatenate([uw[2 * p], uw[2 * p + 1], qh[p]], axis=0).astype(BF16),
               s_old[p].astype(BF16)) for p in pairs]
    yield

    v_new16 = [jnp.where(kmask[h], 0.0, uw[h] - wq[h // 2][(h % 2) * c:(h % 2 + 1) * c])
               .astype(BF16) for h in heads]
    for p in pairs:
        o_ref[bb, :, 2 * p * DV:(2 * p + 2) * DV] = (
            wq[p][2 * c:] + _dot(attn16[2 * p], v_new16[2 * p])
            + _dot(attn16[2 * p + 1], v_new16[2 * p + 1]))
    for h in heads:
        j = d * H_A + h
        k0 = 0 if h % 2 else DK
        kt16 = (kv_slab(h)[:, k0:k0 + DK] * tail_all[:, j:j + 1]).astype(BF16)
        s_ref[bb, d, h // 2, k0:k0 + DK, :] = (
            s_old[h // 2][k0:k0 + DK] * eglast_all[:, j:j + 1] + _dot_tn(kt16, v_new16[h]))


def _delta_kernel(*refs, zero_init, emit_state):
    (qkv_f_ref, qkv_b_ref, ba_f_ref, ba_b_ref, bat_f_ref, bat_b_ref,
     alog_ref, dtb_ref, alog_t_ref, dtb_t_ref) = refs[:10]
    rest = list(refs[10:])
    s0_ref = None if zero_init else rest.pop(0)
    o_f_ref, o_b_ref = rest.pop(0), rest.pop(0)
    sfin_ref = rest.pop(0) if emit_state else None
    s_ref = rest.pop(0)

    n = pl.program_id(1)
    n_chunks = pl.num_programs(1)
    nb = s_ref.shape[0]
    state_at = lambda h: (0, DV) if h % 2 else (DK, 0)

    @pl.when(n == 0)
    def _():
        s_ref[...] = jnp.zeros(s_ref.shape, F32)
        if not zero_init:
            for bb in range(nb):
                for d in range(2):
                    for h in range(H_A):
                        r0, l0 = state_at(h)
                        s_ref[bb, d, h // 2, r0:r0 + DK, l0:l0 + DV] = s0_ref[bb, d, h]

    c = CHUNK
    row = lax.broadcasted_iota(jnp.int32, (c, c), 0)
    col = lax.broadcasted_iota(jnp.int32, (c, c), 1)
    tril16 = (row >= col).astype(BF16)
    triu16 = (row <= col).astype(BF16)

    programs = []
    for bb in range(nb):
        for d, (qkv_ref, ba_ref, bat_ref, o_ref) in enumerate(
                ((qkv_f_ref, ba_f_ref, bat_f_ref, o_f_ref),
                 (qkv_b_ref, ba_b_ref, bat_b_ref, o_b_ref))):
            fwd = d == 0
            csum16, csum_t16 = (tril16, triu16) if fwd else (triu16, tril16)
            last = c - 1 if fwd else 0
            ba = ba_ref[bb]
            bat = bat_ref[bb, 0]
            beta_all = jax.nn.sigmoid(ba[:, 0:2 * H_A])
            g_all = -jnp.exp(alog_ref[...]) * jax.nn.softplus(ba[:, 2 * H_A:] + dtb_ref[...])
            g_all_t = -jnp.exp(alog_t_ref[...]) * jax.nn.softplus(
                bat[2 * H_A:, :] + dtb_t_ref[...])
            gc_all = sum(_dot(csum16, p) for p in _split3(g_all))
            gc_all_t = sum(_dot(p, csum_t16) for p in _split3(g_all_t))
            glast_all = gc_all[last:last + 1, :]
            beta_all_t = jax.nn.sigmoid(bat[0:2 * H_A, :])
            gates = (beta_all, beta_all_t, gc_all, gc_all_t, jnp.exp(gc_all),
                     jnp.exp(glast_all - gc_all), jnp.exp(glast_all))
            programs.append(_group_stages(qkv_ref, o_ref, s_ref, bb, d, gates))

    _lock_step(programs)

    if emit_state:
        @pl.when(n == n_chunks - 1)
        def _():
            for bb in range(nb):
                for d in range(2):
                    for h in range(H_A):
                        r0, l0 = state_at(h)
                        sfin_ref[bb, d, h] = s_ref[bb, d, h // 2, r0:r0 + DK, l0:l0 + DV]


def _delta_scan(qkv, ba, bat, a_log, dt_bias, s0, *, batch, emit_state):
    n_tok = qkv.shape[0]
    t = n_tok // batch
    nc = t // CHUNK
    nb = min(DELTA_BATCH, batch)
    assert batch % nb == 0
    qkv = qkv.reshape(batch, t, N_QKV)
    ba = ba.reshape(batch, t, 4 * H_A)
    bat = bat.reshape(batch, nc, 4 * H_A, CHUNK)
    zero_init = s0 is None
    alog = a_log.reshape(1, 2 * H_A)
    dtb = dt_bias.reshape(1, 2 * H_A)
    alog_t = a_log.reshape(2 * H_A, 1)
    dtb_t = dt_bias.reshape(2 * H_A, 1)

    fw = lambda b, n: (b, n, 0)
    bw = lambda b, n: (b, nc - 1 - n, 0)
    small = lambda b, n: (0, 0)
    in_specs = [pl.BlockSpec((nb, CHUNK, N_QKV), fw),
                pl.BlockSpec((nb, CHUNK, N_QKV), bw),
                pl.BlockSpec((nb, CHUNK, 4 * H_A), fw),
                pl.BlockSpec((nb, CHUNK, 4 * H_A), bw),
                pl.BlockSpec((nb, 1, 4 * H_A, CHUNK), lambda b, n: (b, n, 0, 0)),
                pl.BlockSpec((nb, 1, 4 * H_A, CHUNK), lambda b, n: (b, nc - 1 - n, 0, 0)),
                pl.BlockSpec((1, 2 * H_A), small),
                pl.BlockSpec((1, 2 * H_A), small),
                pl.BlockSpec((2 * H_A, 1), small),
                pl.BlockSpec((2 * H_A, 1), small)]
    args = [qkv, qkv, ba, ba, bat, bat, alog, dtb, alog_t, dtb_t]
    state_spec = pl.BlockSpec((nb, 2, H_A, DK, DV), lambda b, n: (b, 0, 0, 0, 0))
    if not zero_init:
        in_specs.append(state_spec)
        args.append(s0)
    out_specs = [pl.BlockSpec((nb, CHUNK, W_A), fw), pl.BlockSpec((nb, CHUNK, W_A), bw)]
    out_shape = [jax.ShapeDtypeStruct((batch, t, W_A), F32),
                 jax.ShapeDtypeStruct((batch, t, W_A), F32)]
    if emit_state:
        out_specs.append(state_spec)
        out_shape.append(jax.ShapeDtypeStruct((batch, 2, H_A, DK, DV), F32))
    return pl.pallas_call(
        functools.partial(_delta_kernel, zero_init=zero_init, emit_state=emit_state),
        grid=(batch // nb, nc),
        in_specs=in_specs,
        out_specs=out_specs,
        out_shape=out_shape,
        scratch_shapes=[pltpu.VMEM((nb, 2, H_A // 2, 2 * DK, 2 * DV), F32)],
        compiler_params=pltpu.CompilerParams(dimension_semantics=("arbitrary", "arbitrary"),
                                             vmem_limit_bytes=VMEM_LIMIT),
        name="delta_scan",
    )(*args)


def _mix_kernel(x_ref, mod_ref, gain_ref, of_ref, ob_ref, wz_ref, wsc_ref, wg_ref, conv_ref,
                onorm_ref, wa_ref, wb_ref, wo_ref, out_ref, *, row_len):
    def sub_tile(rows):
        x = x_ref[rows, :]
        h = _rms(x, gain_ref[0:1, :]) * (1.0 + mod_ref[0, 1:2, :]) + mod_ref[0, 0:1, :]
        hb = h.astype(BF16)
        yield
        sc = _dot(hb, wsc_ref[...])
        o = of_ref[rows, :] + ob_ref[rows, :]
        slab = 2 * DV
        lo = lax.broadcasted_iota(jnp.int32, (SUB_TILE, slab), 1) < DV
        inv_rms = []
        for s in range(W_A // slab):
            o2 = o[:, s * slab:(s + 1) * slab]
            o2 = o2 * o2
            ms_lo = jnp.sum(jnp.where(lo, o2, 0.0), axis=-1, keepdims=True) * (1.0 / DV)
            ms_hi = jnp.sum(jnp.where(lo, 0.0, o2), axis=-1, keepdims=True) * (1.0 / DV)
            inv_rms.append(jnp.where(lo, lax.rsqrt(ms_lo + EPS), lax.rsqrt(ms_hi + EPS)))
        yield
        z = _dot(hb, wz_ref[...])
        bg, cg, xs = sc[:, 0:W_B], sc[:, W_B:2 * W_B], sc[:, 2 * W_B:3 * W_B]
        yb_in16 = (bg * _conv3_rows(cg * xs, conv_ref, row_len)).astype(BF16)
        yield
        gates = _dot(hb, wg_ref[...])
        oa16 = (o * jnp.concatenate(inv_rms, axis=-1) * onorm_ref[...] * _silu(z)).astype(BF16)
        yield
        y_a = _dot(oa16, wa_ref[...])
        y_b = _dot(yb_in16, wb_ref[...])
        yield
        m = (jax.nn.sigmoid(gates[:, 0:D_MODEL]) * y_a
             + jax.nn.sigmoid(gates[:, D_MODEL:]) * y_b)
        y = _dot(m.astype(BF16), wo_ref[...])
        yield
        out_ref[rows, :] = x + mod_ref[0, 2:3, :] * _rms(y, gain_ref[1:2, :])

    _lock_step(sub_tile(slice(r0, r0 + SUB_TILE)) for r0 in range(0, x_ref.shape[0], SUB_TILE))


def _mix(x, mod3, gains, o_f, o_b, w_z, w_sc, w_g, conv_sc, onorm_t, w_a, w_b, w_o,
         *, mod_row_fn, row_len):
    n_tok = x.shape[0]
    tm = WIDE_TILE
    const = lambda i: (0, 0)
    tok = lambda i: (i, 0)
    return pl.pallas_call(
        functools.partial(_mix_kernel, row_len=row_len),
        grid=(n_tok // tm,),
        in_specs=[pl.BlockSpec((tm, D_MODEL), tok),
                  pl.BlockSpec((1, 6, D_MODEL), lambda i: (mod_row_fn(i, tm), 0, 0)),
                  pl.BlockSpec((4, D_MODEL), const),
                  pl.BlockSpec((tm, W_A), tok),
                  pl.BlockSpec((tm, W_A), tok),
                  _resident((D_MODEL, W_A)),
                  _resident((D_MODEL, 3 * W_B)),
                  _resident((D_MODEL, 2 * D_MODEL)),
                  pl.BlockSpec((3, W_B), const),
                  pl.BlockSpec((1, W_A), const),
                  _resident((W_A, D_MODEL)),
                  _resident((W_B, D_MODEL)),
                  _resident((D_MODEL, D_MODEL))],
        out_specs=pl.BlockSpec((tm, D_MODEL), tok),
        out_shape=jax.ShapeDtypeStruct((n_tok, D_MODEL), F32),
        compiler_params=pltpu.CompilerParams(dimension_semantics=("arbitrary",),
                                             vmem_limit_bytes=VMEM_LIMIT),
        name="mix",
    )(x, mod3, gains, o_f, o_b, w_z, w_sc, w_g, conv_sc, onorm_t, w_a, w_b, w_o)


def _ffn_kernel(xc_ref, xl_ref, mod_ref, gain_ref, win_ref, wout_ref, outc_ref, outl_ref, *,
                n_ctx_tiles):
    is_ctx = pl.program_id(0) < n_ctx_tiles

    def sub_tile(rows):
        x = jnp.where(is_ctx, xc_ref[rows, :], xl_ref[rows, :])
        h = _rms(x, gain_ref[2:3, :]) * (1.0 + mod_ref[0, 4:5, :]) + mod_ref[0, 3:4, :]
        hb = h.astype(BF16)
        yield
        gate = _dot(hb, win_ref[:, 0:D_FF])
        yield
        up = _dot(hb, win_ref[:, D_FF:])
        yield
        y = _dot((_silu(gate) * up).astype(BF16), wout_ref[...])
        yield
        out = x + mod_ref[0, 5:6, :] * _rms(y, gain_ref[3:4, :])

        @pl.when(is_ctx)
        def _():
            outc_ref[rows, :] = out

        @pl.when(jnp.logical_not(is_ctx))
        def _():
            outl_ref[rows, :] = out

    _lock_step(sub_tile(slice(r0, r0 + SUB_TILE)) for r0 in range(0, xc_ref.shape[0], SUB_TILE))


def _ffn(x_ctx, x_lat, mod3, gains, w_in, w_out, *, t_lat):
    tm = WIDE_TILE
    n_ctx, n_lat = x_ctx.shape[0] // tm, x_lat.shape[0] // tm
    const = lambda i: (0, 0)
    ctx_tile = lambda i: (jnp.minimum(i, n_ctx - 1), 0)
    lat_tile = lambda i: (jnp.maximum(i - n_ctx, 0), 0)
    mod_row = lambda i: (jnp.where(i < n_ctx, 0, 1 + (jnp.maximum(i - n_ctx, 0) * tm) // t_lat),
                         0, 0)
    return pl.pallas_call(
        functools.partial(_ffn_kernel, n_ctx_tiles=n_ctx),
        grid=(n_ctx + n_lat,),
        in_specs=[pl.BlockSpec((tm, D_MODEL), ctx_tile),
                  pl.BlockSpec((tm, D_MODEL), lat_tile),
                  pl.BlockSpec((1, 6, D_MODEL), mod_row),
                  pl.BlockSpec((4, D_MODEL), const),
                  _resident((D_MODEL, 2 * D_FF)),
                  _resident((D_FF, D_MODEL))],
        out_specs=[pl.BlockSpec((tm, D_MODEL), ctx_tile),
                   pl.BlockSpec((tm, D_MODEL), lat_tile)],
        out_shape=[jax.ShapeDtypeStruct(x_ctx.shape, F32),
                   jax.ShapeDtypeStruct(x_lat.shape, F32)],
        compiler_params=pltpu.CompilerParams(dimension_semantics=("arbitrary",),
                                             vmem_limit_bytes=VMEM_LIMIT),
        name="ffn",
    )(x_ctx, x_lat, mod3, gains, w_in, w_out)


def _to_slab_order(src_ref, dst_ref):
    slab = 2 * DK
    left = lax.broadcasted_iota(jnp.int32, (src_ref.shape[1], slab), 1) < DK
    for p in range(H_A // 2):
        q_p = src_ref[0, :, p * slab:(p + 1) * slab]
        k_p = src_ref[0, :, W_A + p * slab:W_A + (p + 1) * slab]
        v_p = src_ref[0, :, 2 * W_A + p * slab:2 * W_A + (p + 1) * slab]
        k_swapped = pltpu.roll(k_p, DK, 1)
        dst_ref[:, p * slab:(p + 1) * slab] = pltpu.roll(q_p, DK, 1).astype(dst_ref.dtype)
        even = W_A + 2 * p * slab
        dst_ref[:, even:even + slab] = jnp.where(left, v_p, k_swapped).astype(dst_ref.dtype)
        dst_ref[:, even + slab:even + 2 * slab] = (
            jnp.where(left, k_swapped, v_p).astype(dst_ref.dtype))


def _split_w_in_kernel(w_ref, conv_ref, qkv_ref, z_ref, ba_ref, sc_ref, g_ref, conv_out_ref):
    _to_slab_order(w_ref, qkv_ref)
    _to_slab_order(conv_ref, conv_out_ref)
    o_z, o_ba = N_QKV, N_QKV + W_A
    o_sc = o_ba + 4 * H_A
    o_g = o_sc + 3 * W_B
    z_ref[...] = w_ref[0, :, o_z:o_ba].astype(BF16)
    ba_ref[...] = w_ref[0, :, o_ba:o_sc].astype(BF16)
    sc_ref[...] = w_ref[0, :, o_sc:o_g].astype(BF16)
    g_ref[...] = w_ref[0, :, o_g:].astype(BF16)


def _split_w_in(w, conv_qkv, layer):
    _, rows, n_in = w.shape
    taps = conv_qkv.shape[1]
    blk = 128
    widths = (N_QKV, W_A, 4 * H_A, 3 * W_B, 2 * D_MODEL)
    assert sum(widths) == n_in and rows % blk == 0
    return pl.pallas_call(
        _split_w_in_kernel,
        grid=(rows // blk,),
        in_specs=[pl.BlockSpec((1, blk, n_in), lambda i: (layer, i, 0)),
                  pl.BlockSpec((1, taps, N_QKV), lambda i: (layer, 0, 0))],
        out_specs=[pl.BlockSpec((blk, n), lambda i: (i, 0)) for n in widths]
        + [pl.BlockSpec((taps, N_QKV), lambda i: (0, 0))],
        out_shape=[jax.ShapeDtypeStruct((rows, n), BF16) for n in widths]
        + [jax.ShapeDtypeStruct((taps, N_QKV), conv_qkv.dtype)],
        name="split_w_in",
    )(w, conv_qkv)


def kernel(x_prompt, x_sample, state_delta, c, c_ctx, w_mod, b_mod, norm_gains, w_in, conv_qkv,
           a_log, dt_bias, onorm, conv_sc, w_branch_a, w_branch_b, w_out, w_ffn_in, w_ffn_out):
    depth = w_in.shape[0]
    assert depth == 1
    b_ctx, t_ctx, _ = x_prompt.shape
    b_lat, t_lat, _ = x_sample.shape
    assert SUB_TILE % t_ctx == 0 and SUB_TILE % GRID_W == 0 and t_ctx & (t_ctx - 1) == 0
    assert WIDE_TILE % SUB_TILE == 0 and t_lat % WIDE_TILE == 0
    assert (b_ctx * t_ctx) % WIDE_TILE == 0
    assert b_lat + 1 <= MOD_ROWS
    layer = 0

    cvec = jnp.zeros((MOD_ROWS, D_MODEL), F32).at[0].set(c_ctx).at[1:1 + b_lat].set(c)
    mod3 = _modulation(cvec, w_mod, b_mod, layer).reshape(MOD_ROWS, 6, D_MODEL)

    w_qkv, w_z, w_ba, w_sc, w_g, conv_perm = _split_w_in(w_in, conv_qkv, layer)
    w_bat = w_ba.T
    w_a = w_branch_a[layer].astype(BF16)
    w_b = w_branch_b[layer].astype(BF16)
    w_o = w_out[layer].astype(BF16)
    w_f1 = w_ffn_in[layer].astype(BF16)
    w_f2 = w_ffn_out[layer].astype(BF16)
    gains = norm_gains[layer]
    onorm_t = jnp.tile(onorm[layer], H_A)[None, :]

    def run_group(x3, s0, mod_row_fn, row_len, emit_state):
        batch, t, _ = x3.shape
        x = x3.reshape(batch * t, D_MODEL)
        qkv, ba, bat = _inproj(x, mod3, gains, w_qkv, w_ba, w_bat, conv_perm,
                               mod_row_fn=mod_row_fn, row_len=row_len)
        res = _delta_scan(qkv, ba, bat, a_log[layer], dt_bias[layer], s0,
                          batch=batch, emit_state=emit_state)
        o_f = res[0].reshape(batch * t, W_A)
        o_b = res[1].reshape(batch * t, W_A)
        x1 = _mix(x, mod3, gains, o_f, o_b, w_z, w_sc, w_g, conv_sc[layer], onorm_t,
                  w_a, w_b, w_o, mod_row_fn=mod_row_fn, row_len=row_len)
        return x1, (res[2] if emit_state else None)

    x1_p, s_fin = run_group(x_prompt, None, lambda i, tm: 0, t_ctx, True)
    x1_s, _ = run_group(x_sample, state_delta[:, layer], lambda i, tm: 1 + (i * tm) // t_lat,
                        GRID_W, False)
    y_p, y_s = _ffn(x1_p, x1_s, mod3, gains, w_f1, w_f2, t_lat=t_lat)
    new_state = s_fin[:, None].astype(x_prompt.dtype)
    return (y_p.reshape(x_prompt.shape), y_s.reshape(x_sample.shape), new_state)
```

```python
import functools

import jax
import jax.numpy as jnp
from jax import lax
from jax.experimental import pallas as pl
from jax.experimental.pallas import tpu as pltpu

D_MODEL = 1024
H_A = 8
DK = 64
DV = 64
W_A = H_A * DV
W_B = 512
CHUNK = 64
GRID_W = 64
EPS = 1e-6
N_QKV = 3 * H_A * DK
D_FF = 2816
SUBLANES = 8
MOD_ROWS = SUBLANES
WIDE_TILE = 512
SUB_TILE = 256
DELTA_BATCH = 4
VMEM_LIMIT = 56 * 1024 * 1024

F32 = jnp.float32
BF16 = jnp.bfloat16


def _dot(a, b):
    return jnp.dot(a, b, preferred_element_type=F32)


def _dot_nt(a, b):
    return lax.dot_general(a, b, (((1,), (1,)), ((), ())), preferred_element_type=F32)


def _dot_tn(a, b):
    return lax.dot_general(a, b, (((0,), (0,)), ((), ())), preferred_element_type=F32)


def _resident(shape):
    return pl.BlockSpec(shape, lambda i: (0,) * len(shape), pipeline_mode=pl.Buffered(1))


def _lock_step(programs):
    programs = list(programs)
    while programs:
        for prog in list(programs):
            try:
                next(prog)
            except StopIteration:
                programs.remove(prog)


def _silu(x):
    return x * jax.nn.sigmoid(x)


def _rms(x, gain):
    return x * lax.rsqrt(jnp.mean(x * x, axis=-1, keepdims=True) + EPS) * gain


def _conv3_rows(x, w_ref, row_len):
    n = x.shape[0]
    pos = lax.broadcasted_iota(jnp.int32, (n, 1), 0) & (row_len - 1)
    prev = jnp.where(pos == 0, 0.0, pltpu.roll(x, 1, 0))
    nxt = jnp.where(pos == row_len - 1, 0.0, pltpu.roll(x, n - 1, 0))
    return prev * w_ref[0:1, :] + x * w_ref[1:2, :] + nxt * w_ref[2:3, :]


def _mod_kernel(c_ref, w_ref, b_ref, o_ref):
    rows = c_ref.shape[0]
    s3 = jnp.concatenate([p.astype(F32) for p in _split3(_silu(c_ref[...]))], axis=0).astype(BF16)
    w = w_ref[0]
    w_hi = w.astype(BF16)
    w_lo = (w - w_hi.astype(F32)).astype(BF16)
    acc = _dot(s3, w_hi) + _dot(s3, w_lo)
    o_ref[...] = acc[0:rows] + acc[rows:2 * rows] + acc[2 * rows:] + b_ref[...]


def _modulation(cvec, w_mod, b_mod, layer):
    n_out = w_mod.shape[2]
    blk = D_MODEL
    return pl.pallas_call(
        _mod_kernel,
        grid=(n_out // blk,),
        in_specs=[pl.BlockSpec((MOD_ROWS, D_MODEL), lambda j: (0, 0)),
                  pl.BlockSpec((1, D_MODEL, blk), lambda j: (layer, 0, j)),
                  pl.BlockSpec((1, blk), lambda j: (layer, j))],
        out_specs=pl.BlockSpec((MOD_ROWS, blk), lambda j: (0, j)),
        out_shape=jax.ShapeDtypeStruct((MOD_ROWS, n_out), F32),
        name="modulation",
    )(cvec, w_mod, b_mod)


def _inproj_kernel(x_ref, mod_ref, gain_ref, wqkv_ref, wba_ref, wbat_ref, conv_ref,
                   qkv_ref, ba_ref, bat_ref, *, row_len):
    slab = 2 * DK
    lo = lax.broadcasted_iota(jnp.int32, (SUB_TILE, slab), 1) < DK
    ssq = lambda v, m: jnp.sum(jnp.where(m, v * v, 0.0), axis=-1, keepdims=True)

    def sub_tile(r0):
        rows = slice(r0, r0 + SUB_TILE)
        x = x_ref[rows, :]
        h = _rms(x, gain_ref[0:1, :]) * (1.0 + mod_ref[0, 1:2, :]) + mod_ref[0, 0:1, :]
        hb = h.astype(BF16)
        yield
        for pair in range(N_QKV // (2 * slab)):
            cols2 = slice(pair * 2 * slab, (pair + 1) * 2 * slab)
            v2 = _silu(_conv3_rows(_dot(hb, wqkv_ref[:, cols2]), conv_ref.at[:, cols2], row_len))
            for half in range(2):
                s = 2 * pair + half
                cols = slice(s * slab, (s + 1) * slab)
                v = v2[:, half * slab:(half + 1) * slab]
                if s < W_A // slab:
                    r = jnp.where(lo, lax.rsqrt(ssq(v, lo) + EPS) * (DK ** -0.5),
                                  lax.rsqrt(ssq(v, ~lo) + EPS) * (DK ** -0.5))
                    qkv_ref[rows, cols] = v * r
                else:
                    kmask = lo if (s - W_A // slab) % 2 else ~lo
                    qkv_ref[rows, cols] = v * jnp.where(kmask, lax.rsqrt(ssq(v, kmask) + EPS), 1.0)
            yield
        ba_ref[rows, :] = _dot(hb, wba_ref[...])
        for ci in range(SUB_TILE // CHUNK):
            bat_ref[r0 // CHUNK + ci] = _dot_nt(wbat_ref[...], hb[ci * CHUNK:(ci + 1) * CHUNK, :])

    _lock_step(sub_tile(r0) for r0 in range(0, x_ref.shape[0], SUB_TILE))


def _inproj(x, mod3, gains, w_qkv, w_ba, w_bat, conv_qkv, *, mod_row_fn, row_len):
    n_tok = x.shape[0]
    tm = WIDE_TILE
    const = lambda i: (0, 0)
    return pl.pallas_call(
        functools.partial(_inproj_kernel, row_len=row_len),
        grid=(n_tok // tm,),
        in_specs=[pl.BlockSpec((tm, D_MODEL), lambda i: (i, 0)),
                  pl.BlockSpec((1, 6, D_MODEL), lambda i: (mod_row_fn(i, tm), 0, 0)),
                  pl.BlockSpec((4, D_MODEL), const),
                  pl.BlockSpec((D_MODEL, N_QKV), const),
                  pl.BlockSpec((D_MODEL, 4 * H_A), const),
                  pl.BlockSpec((4 * H_A, D_MODEL), const),
                  pl.BlockSpec((3, N_QKV), const)],
        out_specs=[pl.BlockSpec((tm, N_QKV), lambda i: (i, 0)),
                   pl.BlockSpec((tm, 4 * H_A), lambda i: (i, 0)),
                   pl.BlockSpec((tm // CHUNK, 4 * H_A, CHUNK), lambda i: (i, 0, 0))],
        out_shape=[jax.ShapeDtypeStruct((n_tok, N_QKV), F32),
                   jax.ShapeDtypeStruct((n_tok, 4 * H_A), F32),
                   jax.ShapeDtypeStruct((n_tok // CHUNK, 4 * H_A, CHUNK), F32)],
        compiler_params=pltpu.CompilerParams(dimension_semantics=("arbitrary",),
                                             vmem_limit_bytes=VMEM_LIMIT),
        name="inproj",
    )(x, mod3, gains, w_qkv, w_ba, w_bat, conv_qkv)


def _unit_tri_inverse_stages(lows, lower):
    c = CHUNK
    ri = lax.broadcasted_iota(jnp.int32, (c, c), 0)
    ci = lax.broadcasted_iota(jnp.int32, (c, c), 1)
    eye = (ri == ci).astype(F32)
    rng = range(len(lows))
    joins = lambda b: ((ri // (2 * b)) == (ci // (2 * b))) & ((ri // b) != (ci // b))
    t = [eye - jnp.where(joins(1), lows[i], 0.0) for i in rng]
    b = 2
    while b < c:
        mask = joins(b)
        t16 = [t[i].astype(BF16) for i in rng]
        e16 = [jnp.where(mask, lows[i], 0.0).astype(BF16) for i in rng]
        if b % SUBLANES:
            x = [_dot(t16[i], e16[i]) for i in rng]
            yield
            t = [t[i] - _dot(x[i].astype(BF16), t16[i]) for i in rng]
            yield
        else:
            first = [slice(s, s + b) for s in range(0, c, 2 * b)]
            second = [slice(s + b, s + 2 * b) for s in range(0, c, 2 * b)]
            keep, upd = (first, second) if lower else (second, first)
            t_upd = [jnp.concatenate([t[i][r] for r in upd], axis=0) for i in rng]
            x = [_dot(t_upd[i].astype(BF16), e16[i]) for i in rng]
            yield
            t_upd = [t_upd[i] - _dot(x[i].astype(BF16), t16[i]) for i in rng]
            yield
            pair = lambda kept, new: (kept, new) if lower else (new, kept)
            t = [jnp.concatenate(
                [piece for k, r in enumerate(keep)
                 for piece in pair(t[i][r], t_upd[i][k * b:(k + 1) * b])], axis=0) for i in rng]
        b *= 2
    return t


def _split3(x):
    pieces = []
    for _ in range(3):
        p = x.astype(BF16)
        x = x - p.astype(F32)
        pieces.append(p)
    return pieces


def _group_stages(qkv_ref, o_ref, s_ref, bb, d, gates):
    c = CHUNK
    fwd = d == 0
    row = lax.broadcasted_iota(jnp.int32, (c, c), 0)
    col = lax.broadcasted_iota(jnp.int32, (c, c), 1)
    incl = (row >= col) if fwd else (row <= col)
    strict = (row > col) if fwd else (row < col)
    lane = lax.broadcasted_iota(jnp.int32, (c, 2 * DK), 1)
    beta_all, beta_all_t, gc_all, gc_all_t, eg_all, tail_all, eglast_all = gates
    heads = range(H_A)
    kmask = [(lane < DK) if h % 2 else (lane >= DK) for h in heads]
    kv_slab = lambda h: qkv_ref[bb, :, W_A + h * 2 * DK:W_A + (h + 1) * 2 * DK]
    q_slab = lambda h: qkv_ref[bb, :, (h // 2) * 2 * DK:(h // 2 + 1) * 2 * DK]

    decay, kq = [], []
    for h in heads:
        j = d * H_A + h
        kv = kv_slab(h)
        lhs16 = jnp.concatenate([kv * beta_all[:, j:j + 1], q_slab(h)], axis=0).astype(BF16)
        kq.append(_dot_nt(lhs16, jnp.where(kmask[h], kv, 0.0).astype(BF16)))
        diff = gc_all[:, j:j + 1] - gc_all_t[j:j + 1, :]
        decay.append(jnp.where(incl, jnp.exp(jnp.where(incl, diff, 0.0)), 0.0))
    yield

    lows = [jnp.where(strict, kq[h][:c] * decay[h], 0.0) for h in heads]
    attn16 = [(kq[h][c:] * decay[h]).astype(BF16) for h in heads]
    tmat = yield from _unit_tri_inverse_stages(lows, lower=fwd)

    uw, eg = [], []
    for h in heads:
        j = d * H_A + h
        eg.append(jnp.broadcast_to(eg_all[:, j:j + 1], (c, 2 * DK)))
        kv = kv_slab(h)
        t_beta16 = (tmat[h] * beta_all_t[j:j + 1, :]).astype(BF16)
        uw.append(_dot(t_beta16, jnp.where(kmask[h], kv * eg[h], kv).astype(BF16)))
    yield

    pairs = range(H_A // 2)
    s_old = [s_ref[bb, d, p] for p in pairs]
    qh = [q_slab(2 * p) * jnp.where(kmask[2 * p + 1], eg[2 * p + 1], eg[2 * p]) for p in pairs]
    wq = [_dot(jnp.concatenate([uw[2 * p], uw[2 * p + 1], qh[p]], axis=0).astype(BF16),
               s_old[p].astype(BF16)) for p in pairs]
    yield

    v_new16 = [jnp.where(kmask[h], 0.0, uw[h] - wq[h // 2][(h % 2) * c:(h % 2 + 1) * c])
               .astype(BF16) for h in heads]
    for p in pairs:
        o_ref[bb, :, 2 * p * DV:(2 * p + 2) * DV] = (
            wq[p][2 * c:] + _dot(attn16[2 * p], v_new16[2 * p])
            + _dot(attn16[2 * p + 1], v_new16[2 * p + 1]))
    for h in heads:
        j = d * H_A + h
        k0 = 0 if h % 2 else DK
        kt16 = (kv_slab(h)[:, k0:k0 + DK] * tail_all[:, j:j + 1]).astype(BF16)
        s_ref[bb, d, h // 2, k0:k0 + DK, :] = (
            s_old[h // 2][k0:k0 + DK] * eglast_all[:, j:j + 1] + _dot_tn(kt16, v_new16[h]))


def _delta_kernel(*refs, zero_init, emit_state):
    (qkv_f_ref, qkv_b_ref, ba_f_ref, ba_b_ref, bat_f_ref, bat_b_ref,
     alog_ref, dtb_ref, alog_t_ref, dtb_t_ref) = refs[:10]
    rest = list(refs[10:])
    s0_ref = None if zero_init else rest.pop(0)
    o_f_ref, o_b_ref = rest.pop(0), rest.pop(0)
    sfin_ref = rest.pop(0) if emit_state else None
    s_ref = rest.pop(0)

    n = pl.program_id(1)
    n_chunks = pl.num_programs(1)
    nb = s_ref.shape[0]
    state_at = lambda h: (0, DV) if h % 2 else (DK, 0)

    @pl.when(n == 0)
    def _():
        s_ref[...] = jnp.zeros(s_ref.shape, F32)
        if not zero_init:
            for bb in range(nb):
                for d in range(2):
                    for h in range(H_A):
                        r0, l0 = state_at(h)
                        s_ref[bb, d, h // 2, r0:r0 + DK, l0:l0 + DV] = s0_ref[bb, d, h]

    c = CHUNK
    row = lax.broadcasted_iota(jnp.int32, (c, c), 0)
    col = lax.broadcasted_iota(jnp.int32, (c, c), 1)
    tril16 = (row >= col).astype(BF16)
    triu16 = (row <= col).astype(BF16)

    programs = []
    for bb in range(nb):
        for d, (qkv_ref, ba_ref, bat_ref, o_ref) in enumerate(
                ((qkv_f_ref, ba_f_ref, bat_f_ref, o_f_ref),
                 (qkv_b_ref, ba_b_ref, bat_b_ref, o_b_ref))):
            fwd = d == 0
            csum16, csum_t16 = (tril16, triu16) if fwd else (triu16, tril16)
            last = c - 1 if fwd else 0
            ba = ba_ref[bb]
            bat = bat_ref[bb, 0]
            beta_all = jax.nn.sigmoid(ba[:, 0:2 * H_A])
            g_all = -jnp.exp(alog_ref[...]) * jax.nn.softplus(ba[:, 2 * H_A:] + dtb_ref[...])
            g_all_t = -jnp.exp(alog_t_ref[...]) * jax.nn.softplus(
                bat[2 * H_A:, :] + dtb_t_ref[...])
            gc_all = sum(_dot(csum16, p) for p in _split3(g_all))
            gc_all_t = sum(_dot(p, csum_t16) for p in _split3(g_all_t))
            glast_all = gc_all[last:last + 1, :]
            beta_all_t = jax.nn.sigmoid(bat[0:2 * H_A, :])
            gates = (beta_all, beta_all_t, gc_all, gc_all_t, jnp.exp(gc_all),
                     jnp.exp(glast_all - gc_all), jnp.exp(glast_all))
            programs.append(_group_stages(qkv_ref, o_ref, s_ref, bb, d, gates))

    _lock_step(programs)

    if emit_state:
        @pl.when(n == n_chunks - 1)
        def _():
            for bb in range(nb):
                for d in range(2):
                    for h in range(H_A):
                        r0, l0 = state_at(h)
                        sfin_ref[bb, d, h] = s_ref[bb, d, h // 2, r0:r0 + DK, l0:l0 + DV]


def _delta_scan(qkv, ba, bat, a_log, dt_bias, s0, *, batch, emit_state):
    n_tok = qkv.shape[0]
    t = n_tok // batch
    nc = t // CHUNK
    nb = min(DELTA_BATCH, batch)
    assert batch % nb == 0
    qkv = qkv.reshape(batch, t, N_QKV)
    ba = ba.reshape(batch, t, 4 * H_A)
    bat = bat.reshape(batch, nc, 4 * H_A, CHUNK)
    zero_init = s0 is None
    alog = a_log.reshape(1, 2 * H_A)
    dtb = dt_bias.reshape(1, 2 * H_A)
    alog_t = a_log.reshape(2 * H_A, 1)
    dtb_t = dt_bias.reshape(2 * H_A, 1)

    fw = lambda b, n: (b, n, 0)
    bw = lambda b, n: (b, nc - 1 - n, 0)
    small = lambda b, n: (0, 0)
    in_specs = [pl.BlockSpec((nb, CHUNK, N_QKV), fw),
                pl.BlockSpec((nb, CHUNK, N_QKV), bw),
                pl.BlockSpec((nb, CHUNK, 4 * H_A), fw),
                pl.BlockSpec((nb, CHUNK, 4 * H_A), bw),
                pl.BlockSpec((nb, 1, 4 * H_A, CHUNK), lambda b, n: (b, n, 0, 0)),
                pl.BlockSpec((nb, 1, 4 * H_A, CHUNK), lambda b, n: (b, nc - 1 - n, 0, 0)),
                pl.BlockSpec((1, 2 * H_A), small),
                pl.BlockSpec((1, 2 * H_A), small),
                pl.BlockSpec((2 * H_A, 1), small),
                pl.BlockSpec((2 * H_A, 1), small)]
    args = [qkv, qkv, ba, ba, bat, bat, alog, dtb, alog_t, dtb_t]
    state_spec = pl.BlockSpec((nb, 2, H_A, DK, DV), lambda b, n: (b, 0, 0, 0, 0))
    if not zero_init:
        in_specs.append(state_spec)
        args.append(s0)
    out_specs = [pl.BlockSpec((nb, CHUNK, W_A), fw), pl.BlockSpec((nb, CHUNK, W_A), bw)]
    out_shape = [jax.ShapeDtypeStruct((batch, t, W_A), F32),
                 jax.ShapeDtypeStruct((batch, t, W_A), F32)]
    if emit_state:
        out_specs.append(state_spec)
        out_shape.append(jax.ShapeDtypeStruct((batch, 2, H_A, DK, DV), F32))
    return pl.pallas_call(
        functools.partial(_delta_kernel, zero_init=zero_init, emit_state=emit_state),
        grid=(batch // nb, nc),
        in_specs=in_specs,
        out_specs=out_specs,
        out_shape=out_shape,
        scratch_shapes=[pltpu.VMEM((nb, 2, H_A // 2, 2 * DK, 2 * DV), F32)],
        compiler_params=pltpu.CompilerParams(dimension_semantics=("arbitrary", "arbitrary"),
                                             vmem_limit_bytes=VMEM_LIMIT),
        name="delta_scan",
    )(*args)


def _mix_kernel(x_ref, mod_ref, gain_ref, of_ref, ob_ref, wz_ref, wsc_ref, wg_ref, conv_ref,
                onorm_ref, wa_ref, wb_ref, wo_ref, out_ref, *, row_len):
    def sub_tile(rows):
        x = x_ref[rows, :]
        h = _rms(x, gain_ref[0:1, :]) * (1.0 + mod_ref[0, 1:2, :]) + mod_ref[0, 0:1, :]
        hb = h.astype(BF16)
        yield
        sc = _dot(hb, wsc_ref[...])
        o = of_ref[rows, :] + ob_ref[rows, :]
        slab = 2 * DV
        lo = lax.broadcasted_iota(jnp.int32, (SUB_TILE, slab), 1) < DV
        inv_rms = []
        for s in range(W_A // slab):
            o2 = o[:, s * slab:(s + 1) * slab]
            o2 = o2 * o2
            ms_lo = jnp.sum(jnp.where(lo, o2, 0.0), axis=-1, keepdims=True) * (1.0 / DV)
            ms_hi = jnp.sum(jnp.where(lo, 0.0, o2), axis=-1, keepdims=True) * (1.0 / DV)
            inv_rms.append(jnp.where(lo, lax.rsqrt(ms_lo + EPS), lax.rsqrt(ms_hi + EPS)))
        yield
        z = _dot(hb, wz_ref[...])
        bg, cg, xs = sc[:, 0:W_B], sc[:, W_B:2 * W_B], sc[:, 2 * W_B:3 * W_B]
        yb_in16 = (bg * _conv3_rows(cg * xs, conv_ref, row_len)).astype(BF16)
        yield
        gates = _dot(hb, wg_ref[...])
        oa16 = (o * jnp.concatenate(inv_rms, axis=-1) * onorm_ref[...] * _silu(z)).astype(BF16)
        yield
        y_a = _dot(oa16, wa_ref[...])
        y_b = _dot(yb_in16, wb_ref[...])
        yield
        m = (jax.nn.sigmoid(gates[:, 0:D_MODEL]) * y_a
             + jax.nn.sigmoid(gates[:, D_MODEL:]) * y_b)
        y = _dot(m.astype(BF16), wo_ref[...])
        yield
        out_ref[rows, :] = x + mod_ref[0, 2:3, :] * _rms(y, gain_ref[1:2, :])

    _lock_step(sub_tile(slice(r0, r0 + SUB_TILE)) for r0 in range(0, x_ref.shape[0], SUB_TILE))


def _mix(x, mod3, gains, o_f, o_b, w_z, w_sc, w_g, conv_sc, onorm_t, w_a, w_b, w_o,
         *, mod_row_fn, row_len):
    n_tok = x.shape[0]
    tm = WIDE_TILE
    const = lambda i: (0, 0)
    tok = lambda i: (i, 0)
    return pl.pallas_call(
        functools.partial(_mix_kernel, row_len=row_len),
        grid=(n_tok // tm,),
        in_specs=[pl.BlockSpec((tm, D_MODEL), tok),
                  pl.BlockSpec((1, 6, D_MODEL), lambda i: (mod_row_fn(i, tm), 0, 0)),
                  pl.BlockSpec((4, D_MODEL), const),
                  pl.BlockSpec((tm, W_A), tok),
                  pl.BlockSpec((tm, W_A), tok),
                  _resident((D_MODEL, W_A)),
                  _resident((D_MODEL, 3 * W_B)),
                  _resident((D_MODEL, 2 * D_MODEL)),
                  pl.BlockSpec((3, W_B), const),
                  pl.BlockSpec((1, W_A), const),
                  _resident((W_A, D_MODEL)),
                  _resident((W_B, D_MODEL)),
                  _resident((D_MODEL, D_MODEL))],
        out_specs=pl.BlockSpec((tm, D_MODEL), tok),
        out_shape=jax.ShapeDtypeStruct((n_tok, D_MODEL), F32),
        compiler_params=pltpu.CompilerParams(dimension_semantics=("arbitrary",),
                                             vmem_limit_bytes=VMEM_LIMIT),
        name="mix",
    )(x, mod3, gains, o_f, o_b, w_z, w_sc, w_g, conv_sc, onorm_t, w_a, w_b, w_o)


def _ffn_kernel(x_ref, mod_ref, gain_ref, win_ref, wout_ref, out_ref):
    def sub_tile(rows):
        x = x_ref[rows, :]
        h = _rms(x, gain_ref[2:3, :]) * (1.0 + mod_ref[0, 4:5, :]) + mod_ref[0, 3:4, :]
        hb = h.astype(BF16)
        yield
        gate = _dot(hb, win_ref[:, 0:D_FF])
        yield
        up = _dot(hb, win_ref[:, D_FF:])
        yield
        y = _dot((_silu(gate) * up).astype(BF16), wout_ref[...])
        yield
        out_ref[rows, :] = x + mod_ref[0, 5:6, :] * _rms(y, gain_ref[3:4, :])

    _lock_step(sub_tile(slice(r0, r0 + SUB_TILE)) for r0 in range(0, x_ref.shape[0], SUB_TILE))


def _ffn(x, mod3, gains, w_in, w_out, *, mod_row_fn):
    n_tok = x.shape[0]
    tm = WIDE_TILE
    const = lambda i: (0, 0)
    tok = lambda i: (i, 0)
    return pl.pallas_call(
        _ffn_kernel,
        grid=(n_tok // tm,),
        in_specs=[pl.BlockSpec((tm, D_MODEL), tok),
                  pl.BlockSpec((1, 6, D_MODEL), lambda i: (mod_row_fn(i, tm), 0, 0)),
                  pl.BlockSpec((4, D_MODEL), const),
                  _resident((D_MODEL, 2 * D_FF)),
                  _resident((D_FF, D_MODEL))],
        out_specs=pl.BlockSpec((tm, D_MODEL), tok),
        out_shape=jax.ShapeDtypeStruct((n_tok, D_MODEL), F32),
        compiler_params=pltpu.CompilerParams(dimension_semantics=("arbitrary",),
                                             vmem_limit_bytes=VMEM_LIMIT),
        name="ffn",
    )(x, mod3, gains, w_in, w_out)


def _to_slab_order(src_ref, dst_ref):
    slab = 2 * DK
    left = lax.broadcasted_iota(jnp.int32, (src_ref.shape[1], slab), 1) < DK
    for p in range(H_A // 2):
        q_p = src_ref[0, :, p * slab:(p + 1) * slab]
        k_p = src_ref[0, :, W_A + p * slab:W_A + (p + 1) * slab]
        v_p = src_ref[0, :, 2 * W_A + p * slab:2 * W_A + (p + 1) * slab]
        k_swapped = pltpu.roll(k_p, DK, 1)
        dst_ref[:, p * slab:(p + 1) * slab] = pltpu.roll(q_p, DK, 1).astype(dst_ref.dtype)
        even = W_A + 2 * p * slab
        dst_ref[:, even:even + slab] = jnp.where(left, v_p, k_swapped).astype(dst_ref.dtype)
        dst_ref[:, even + slab:even + 2 * slab] = (
            jnp.where(left, k_swapped, v_p).astype(dst_ref.dtype))


def _split_w_in_kernel(w_ref, conv_ref, qkv_ref, z_ref, ba_ref, sc_ref, g_ref, conv_out_ref):
    _to_slab_order(w_ref, qkv_ref)
    _to_slab_order(conv_ref, conv_out_ref)
    o_z, o_ba = N_QKV, N_QKV + W_A
    o_sc = o_ba + 4 * H_A
    o_g = o_sc + 3 * W_B
    z_ref[...] = w_ref[0, :, o_z:o_ba].astype(BF16)
    ba_ref[...] = w_ref[0, :, o_ba:o_sc].astype(BF16)
    sc_ref[...] = w_ref[0, :, o_sc:o_g].astype(BF16)
    g_ref[...] = w_ref[0, :, o_g:].astype(BF16)


def _split_w_in(w, conv_qkv, layer):
    _, rows, n_in = w.shape
    taps = conv_qkv.shape[1]
    blk = 128
    widths = (N_QKV, W_A, 4 * H_A, 3 * W_B, 2 * D_MODEL)
    assert sum(widths) == n_in and rows % blk == 0
    return pl.pallas_call(
        _split_w_in_kernel,
        grid=(rows // blk,),
        in_specs=[pl.BlockSpec((1, blk, n_in), lambda i: (layer, i, 0)),
                  pl.BlockSpec((1, taps, N_QKV), lambda i: (layer, 0, 0))],
        out_specs=[pl.BlockSpec((blk, n), lambda i: (i, 0)) for n in widths]
        + [pl.BlockSpec((taps, N_QKV), lambda i: (0, 0))],
        out_shape=[jax.ShapeDtypeStruct((rows, n), BF16) for n in widths]
        + [jax.ShapeDtypeStruct((taps, N_QKV), conv_qkv.dtype)],
        name="split_w_in",
    )(w, conv_qkv)


def kernel(x_prompt, x_sample, state_delta, c, c_ctx, w_mod, b_mod, norm_gains, w_in, conv_qkv,
           a_log, dt_bias, onorm, conv_sc, w_branch_a, w_branch_b, w_out, w_ffn_in, w_ffn_out):
    depth = w_in.shape[0]
    assert depth == 1
    b_ctx, t_ctx, _ = x_prompt.shape
    b_lat, t_lat, _ = x_sample.shape
    assert SUB_TILE % t_ctx == 0 and SUB_TILE % GRID_W == 0 and t_ctx & (t_ctx - 1) == 0
    assert WIDE_TILE % SUB_TILE == 0 and t_lat % WIDE_TILE == 0
    assert (b_ctx * t_ctx) % WIDE_TILE == 0
    assert b_lat + 1 <= MOD_ROWS
    layer = 0

    cvec = jnp.zeros((MOD_ROWS, D_MODEL), F32).at[0].set(c_ctx).at[1:1 + b_lat].set(c)
    mod3 = _modulation(cvec, w_mod, b_mod, layer).reshape(MOD_ROWS, 6, D_MODEL)

    w_qkv, w_z, w_ba, w_sc, w_g, conv_perm = _split_w_in(w_in, conv_qkv, layer)
    w_bat = w_ba.T
    w_a = w_branch_a[layer].astype(BF16)
    w_b = w_branch_b[layer].astype(BF16)
    w_o = w_out[layer].astype(BF16)
    w_f1 = w_ffn_in[layer].astype(BF16)
    w_f2 = w_ffn_out[layer].astype(BF16)
    gains = norm_gains[layer]
    onorm_t = jnp.tile(onorm[layer], H_A)[None, :]

    def run_group(x3, s0, mod_row_fn, row_len, emit_state):
        batch, t, _ = x3.shape
        x = x3.reshape(batch * t, D_MODEL)
        qkv, ba, bat = _inproj(x, mod3, gains, w_qkv, w_ba, w_bat, conv_perm,
                               mod_row_fn=mod_row_fn, row_len=row_len)
        res = _delta_scan(qkv, ba, bat, a_log[layer], dt_bias[layer], s0,
                          batch=batch, emit_state=emit_state)
        o_f = res[0].reshape(batch * t, W_A)
        o_b = res[1].reshape(batch * t, W_A)
        x1 = _mix(x, mod3, gains, o_f, o_b, w_z, w_sc, w_g, conv_sc[layer], onorm_t,
                  w_a, w_b, w_o, mod_row_fn=mod_row_fn, row_len=row_len)
        x2 = _ffn(x1, mod3, gains, w_f1, w_f2, mod_row_fn=mod_row_fn)
        return x2.reshape(batch, t, D_MODEL), (res[2] if emit_state else None)

    y_p, s_fin = run_group(x_prompt, None, lambda i, tm: 0, t_ctx, True)
    y_s, _ = run_group(x_sample, state_delta[:, layer], lambda i, tm: 1 + (i * tm) // t_lat,
                       GRID_W, False)
    new_state = s_fin[:, None].astype(x_prompt.dtype)
    return (y_p, y_s, new_state)
```

```python
import functools

import jax
import jax.numpy as jnp
from jax import lax
from jax.experimental import pallas as pl
from jax.experimental.pallas import tpu as pltpu

D_MODEL = 1024
H_A = 8
DK = 64
DV = 64
W_A = H_A * DV
W_B = 512
CHUNK = 64
GRID_W = 64
EPS = 1e-6
N_QKV = 3 * H_A * DK
D_FF = 2816
SUBLANES = 8
MOD_ROWS = SUBLANES
WIDE_TILE = 512
SUB_TILE = 256
DELTA_BATCH = 4
VMEM_LIMIT = 56 * 1024 * 1024

F32 = jnp.float32
BF16 = jnp.bfloat16


def _dot(a, b):
    return jnp.dot(a, b, preferred_element_type=F32)


def _dot_nt(a, b):
    return lax.dot_general(a, b, (((1,), (1,)), ((), ())), preferred_element_type=F32)


def _dot_tn(a, b):
    return lax.dot_general(a, b, (((0,), (0,)), ((), ())), preferred_element_type=F32)


def _resident(shape):
    return pl.BlockSpec(shape, lambda i: (0,) * len(shape), pipeline_mode=pl.Buffered(1))


def _lock_step(programs):
    programs = list(programs)
    while programs:
        for prog in list(programs):
            try:
                next(prog)
            except StopIteration:
                programs.remove(prog)


def _silu(x):
    return x * jax.nn.sigmoid(x)


def _rms(x, gain):
    return x * lax.rsqrt(jnp.mean(x * x, axis=-1, keepdims=True) + EPS) * gain


def _conv3_rows(x, w_ref, row_len):
    n = x.shape[0]
    pos = lax.broadcasted_iota(jnp.int32, (n, 1), 0) & (row_len - 1)
    prev = jnp.where(pos == 0, 0.0, pltpu.roll(x, 1, 0))
    nxt = jnp.where(pos == row_len - 1, 0.0, pltpu.roll(x, n - 1, 0))
    return prev * w_ref[0:1, :] + x * w_ref[1:2, :] + nxt * w_ref[2:3, :]


def _mod_kernel(c_ref, w_ref, b_ref, o_ref):
    rows = c_ref.shape[0]
    s3 = jnp.concatenate([p.astype(F32) for p in _split3(_silu(c_ref[...]))], axis=0).astype(BF16)
    w = w_ref[0]
    w_hi = w.astype(BF16)
    w_lo = (w - w_hi.astype(F32)).astype(BF16)
    acc = _dot(s3, w_hi) + _dot(s3, w_lo)
    o_ref[...] = acc[0:rows] + acc[rows:2 * rows] + acc[2 * rows:] + b_ref[...]


def _modulation(cvec, w_mod, b_mod, layer):
    n_out = w_mod.shape[2]
    blk = D_MODEL
    return pl.pallas_call(
        _mod_kernel,
        grid=(n_out // blk,),
        in_specs=[pl.BlockSpec((MOD_ROWS, D_MODEL), lambda j: (0, 0)),
                  pl.BlockSpec((1, D_MODEL, blk), lambda j: (layer, 0, j)),
                  pl.BlockSpec((1, blk), lambda j: (layer, j))],
        out_specs=pl.BlockSpec((MOD_ROWS, blk), lambda j: (0, j)),
        out_shape=jax.ShapeDtypeStruct((MOD_ROWS, n_out), F32),
        name="modulation",
    )(cvec, w_mod, b_mod)


def _inproj_kernel(x_ref, mod_ref, gain_ref, wqkv_ref, wba_ref, wbat_ref, conv_ref,
                   qkv_ref, ba_ref, bat_ref, *, row_len):
    slab = 2 * DK
    lo = lax.broadcasted_iota(jnp.int32, (SUB_TILE, slab), 1) < DK
    ssq = lambda v, m: jnp.sum(jnp.where(m, v * v, 0.0), axis=-1, keepdims=True)

    def sub_tile(r0):
        rows = slice(r0, r0 + SUB_TILE)
        x = x_ref[rows, :]
        h = _rms(x, gain_ref[0:1, :]) * (1.0 + mod_ref[0, 1:2, :]) + mod_ref[0, 0:1, :]
        hb = h.astype(BF16)
        yield
        for pair in range(N_QKV // (2 * slab)):
            cols2 = slice(pair * 2 * slab, (pair + 1) * 2 * slab)
            v2 = _silu(_conv3_rows(_dot(hb, wqkv_ref[:, cols2]), conv_ref.at[:, cols2], row_len))
            for half in range(2):
                s = 2 * pair + half
                cols = slice(s * slab, (s + 1) * slab)
                v = v2[:, half * slab:(half + 1) * slab]
                if s < W_A // slab:
                    r = jnp.where(lo, lax.rsqrt(ssq(v, lo) + EPS) * (DK ** -0.5),
                                  lax.rsqrt(ssq(v, ~lo) + EPS) * (DK ** -0.5))
                    qkv_ref[rows, cols] = v * r
                else:
                    kmask = lo if (s - W_A // slab) % 2 else ~lo
                    qkv_ref[rows, cols] = v * jnp.where(kmask, lax.rsqrt(ssq(v, kmask) + EPS), 1.0)
            yield
        ba_ref[rows, :] = _dot(hb, wba_ref[...])
        for ci in range(SUB_TILE // CHUNK):
            bat_ref[r0 // CHUNK + ci] = _dot_nt(wbat_ref[...], hb[ci * CHUNK:(ci + 1) * CHUNK, :])

    _lock_step(sub_tile(r0) for r0 in range(0, x_ref.shape[0], SUB_TILE))


def _inproj(x, mod3, gains, w_qkv, w_ba, w_bat, conv_qkv, *, mod_row_fn, row_len):
    n_tok = x.shape[0]
    tm = WIDE_TILE
    const = lambda i: (0, 0)
    return pl.pallas_call(
        functools.partial(_inproj_kernel, row_len=row_len),
        grid=(n_tok // tm,),
        in_specs=[pl.BlockSpec((tm, D_MODEL), lambda i: (i, 0)),
                  pl.BlockSpec((1, 6, D_MODEL), lambda i: (mod_row_fn(i, tm), 0, 0)),
                  pl.BlockSpec((4, D_MODEL), const),
                  pl.BlockSpec((D_MODEL, N_QKV), const),
                  pl.BlockSpec((D_MODEL, 4 * H_A), const),
                  pl.BlockSpec((4 * H_A, D_MODEL), const),
                  pl.BlockSpec((3, N_QKV), const)],
        out_specs=[pl.BlockSpec((tm, N_QKV), lambda i: (i, 0)),
                   pl.BlockSpec((tm, 4 * H_A), lambda i: (i, 0)),
                   pl.BlockSpec((tm // CHUNK, 4 * H_A, CHUNK), lambda i: (i, 0, 0))],
        out_shape=[jax.ShapeDtypeStruct((n_tok, N_QKV), F32),
                   jax.ShapeDtypeStruct((n_tok, 4 * H_A), F32),
                   jax.ShapeDtypeStruct((n_tok // CHUNK, 4 * H_A, CHUNK), F32)],
        compiler_params=pltpu.CompilerParams(dimension_semantics=("arbitrary",),
                                             vmem_limit_bytes=VMEM_LIMIT),
        name="inproj",
    )(x, mod3, gains, w_qkv, w_ba, w_bat, conv_qkv)


def _unit_tri_inverse_stages(lows, lower):
    c = CHUNK
    ri = lax.broadcasted_iota(jnp.int32, (c, c), 0)
    ci = lax.broadcasted_iota(jnp.int32, (c, c), 1)
    eye = (ri == ci).astype(F32)
    rng = range(len(lows))
    joins = lambda b: ((ri // (2 * b)) == (ci // (2 * b))) & ((ri // b) != (ci // b))
    t = [eye - jnp.where(joins(1), lows[i], 0.0) for i in rng]
    b = 2
    while b < c:
        mask = joins(b)
        t16 = [t[i].astype(BF16) for i in rng]
        e16 = [jnp.where(mask, lows[i], 0.0).astype(BF16) for i in rng]
        if b % SUBLANES:
            x = [_dot(t16[i], e16[i]) for i in rng]
            yield
            t = [t[i] - _dot(x[i].astype(BF16), t16[i]) for i in rng]
            yield
        else:
            first = [slice(s, s + b) for s in range(0, c, 2 * b)]
            second = [slice(s + b, s + 2 * b) for s in range(0, c, 2 * b)]
            keep, upd = (first, second) if lower else (second, first)
            t_upd = [jnp.concatenate([t[i][r] for r in upd], axis=0) for i in rng]
            x = [_dot(t_upd[i].astype(BF16), e16[i]) for i in rng]
            yield
            t_upd = [t_upd[i] - _dot(x[i].astype(BF16), t16[i]) for i in rng]
            yield
            pair = lambda kept, new: (kept, new) if lower else (new, kept)
            t = [jnp.concatenate(
                [piece for k, r in enumerate(keep)
                 for piece in pair(t[i][r], t_upd[i][k * b:(k + 1) * b])], axis=0) for i in rng]
        b *= 2
    return t


def _split3(x):
    pieces = []
    for _ in range(3):
        p = x.astype(BF16)
        x = x - p.astype(F32)
        pieces.append(p)
    return pieces


def _group_stages(qkv_ref, o_ref, s_ref, bb, d, gates):
    c = CHUNK
    fwd = d == 0
    row = lax.broadcasted_iota(jnp.int32, (c, c), 0)
    col = lax.broadcasted_iota(jnp.int32, (c, c), 1)
    incl = (row >= col) if fwd else (row <= col)
    strict = (row > col) if fwd else (row < col)
    lane = lax.broadcasted_iota(jnp.int32, (c, 2 * DK), 1)
    beta_all, beta_all_t, gc_all, gc_all_t, eg_all, tail_all, eglast_all = gates
    heads = range(H_A)
    kmask = [(lane < DK) if h % 2 else (lane >= DK) for h in heads]
    kv_slab = lambda h: qkv_ref[bb, :, W_A + h * 2 * DK:W_A + (h + 1) * 2 * DK]
    q_slab = lambda h: qkv_ref[bb, :, (h // 2) * 2 * DK:(h // 2 + 1) * 2 * DK]

    decay, kq = [], []
    for h in heads:
        j = d * H_A + h
        kv = kv_slab(h)
        lhs16 = jnp.concatenate([kv * beta_all[:, j:j + 1], q_slab(h)], axis=0).astype(BF16)
        kq.append(_dot_nt(lhs16, jnp.where(kmask[h], kv, 0.0).astype(BF16)))
        diff = gc_all[:, j:j + 1] - gc_all_t[j:j + 1, :]
        decay.append(jnp.where(incl, jnp.exp(jnp.where(incl, diff, 0.0)), 0.0))
    yield

    lows = [jnp.where(strict, kq[h][:c] * decay[h], 0.0) for h in heads]
    attn16 = [(kq[h][c:] * decay[h]).astype(BF16) for h in heads]
    tmat = yield from _unit_tri_inverse_stages(lows, lower=fwd)

    uw, eg = [], []
    for h in heads:
        j = d * H_A + h
        eg.append(jnp.broadcast_to(eg_all[:, j:j + 1], (c, 2 * DK)))
        kv = kv_slab(h)
        t_beta16 = (tmat[h] * beta_all_t[j:j + 1, :]).astype(BF16)
        uw.append(_dot(t_beta16, jnp.where(kmask[h], kv * eg[h], kv).astype(BF16)))
    yield

    pairs = range(H_A // 2)
    s_old = [s_ref[bb, d, p] for p in pairs]
    qh = [q_slab(2 * p) * jnp.where(kmask[2 * p + 1], eg[2 * p + 1], eg[2 * p]) for p in pairs]
    wq = [_dot(jnp.concatenate([uw[2 * p], uw[2 * p + 1], qh[p]], axis=0).astype(BF16),
               s_old[p].astype(BF16)) for p in pairs]
    yield

    v_new16 = [jnp.where(kmask[h], 0.0, uw[h] - wq[h // 2][(h % 2) * c:(h % 2 + 1) * c])
               .astype(BF16) for h in heads]
    for p in pairs:
        o_ref[bb, :, 2 * p * DV:(2 * p + 2) * DV] = (
            wq[p][2 * c:] + _dot(attn16[2 * p], v_new16[2 * p])
            + _dot(attn16[2 * p + 1], v_new16[2 * p + 1]))
    for h in heads:
        j = d * H_A + h
        k0 = 0 if h % 2 else DK
        kt16 = (kv_slab(h)[:, k0:k0 + DK] * tail_all[:, j:j + 1]).astype(BF16)
        s_ref[bb, d, h // 2, k0:k0 + DK, :] = (
            s_old[h // 2][k0:k0 + DK] * eglast_all[:, j:j + 1] + _dot_tn(kt16, v_new16[h]))


def _delta_kernel(*refs, zero_init, emit_state):
    (qkv_f_ref, qkv_b_ref, ba_f_ref, ba_b_ref, bat_f_ref, bat_b_ref,
     alog_ref, dtb_ref, alog_t_ref, dtb_t_ref) = refs[:10]
    rest = list(refs[10:])
    s0_ref = None if zero_init else rest.pop(0)
    o_f_ref, o_b_ref = rest.pop(0), rest.pop(0)
    sfin_ref = rest.pop(0) if emit_state else None
    s_ref = rest.pop(0)

    n = pl.program_id(1)
    n_chunks = pl.num_programs(1)
    nb = s_ref.shape[0]
    state_at = lambda h: (0, DV) if h % 2 else (DK, 0)

    @pl.when(n == 0)
    def _():
        s_ref[...] = jnp.zeros(s_ref.shape, F32)
        if not zero_init:
            for bb in range(nb):
                for d in range(2):
                    for h in range(H_A):
                        r0, l0 = state_at(h)
                        s_ref[bb, d, h // 2, r0:r0 + DK, l0:l0 + DV] = s0_ref[bb, d, h]

    c = CHUNK
    row = lax.broadcasted_iota(jnp.int32, (c, c), 0)
    col = lax.broadcasted_iota(jnp.int32, (c, c), 1)
    tril16 = (row >= col).astype(BF16)
    triu16 = (row <= col).astype(BF16)

    programs = []
    for bb in range(nb):
        for d, (qkv_ref, ba_ref, bat_ref, o_ref) in enumerate(
                ((qkv_f_ref, ba_f_ref, bat_f_ref, o_f_ref),
                 (qkv_b_ref, ba_b_ref, bat_b_ref, o_b_ref))):
            fwd = d == 0
            csum16, csum_t16 = (tril16, triu16) if fwd else (triu16, tril16)
            last = c - 1 if fwd else 0
            ba = ba_ref[bb]
            bat = bat_ref[bb, 0]
            beta_all = jax.nn.sigmoid(ba[:, 0:2 * H_A])
            g_all = -jnp.exp(alog_ref[...]) * jax.nn.softplus(ba[:, 2 * H_A:] + dtb_ref[...])
            g_all_t = -jnp.exp(alog_t_ref[...]) * jax.nn.softplus(
                bat[2 * H_A:, :] + dtb_t_ref[...])
            gc_all = sum(_dot(csum16, p) for p in _split3(g_all))
            gc_all_t = sum(_dot(p, csum_t16) for p in _split3(g_all_t))
            glast_all = gc_all[last:last + 1, :]
            beta_all_t = jax.nn.sigmoid(bat[0:2 * H_A, :])
            gates = (beta_all, beta_all_t, gc_all, gc_all_t, jnp.exp(gc_all),
                     jnp.exp(glast_all - gc_all), jnp.exp(glast_all))
            programs.append(_group_stages(qkv_ref, o_ref, s_ref, bb, d, gates))

    _lock_step(programs)

    if emit_state:
        @pl.when(n == n_chunks - 1)
        def _():
            for bb in range(nb):
                for d in range(2):
                    for h in range(H_A):
                        r0, l0 = state_at(h)
                        sfin_ref[bb, d, h] = s_ref[bb, d, h // 2, r0:r0 + DK, l0:l0 + DV]


def _delta_scan(qkv, ba, bat, a_log, dt_bias, s0, *, batch, emit_state):
    n_tok = qkv.shape[0]
    t = n_tok // batch
    nc = t // CHUNK
    nb = min(DELTA_BATCH, batch)
    assert batch % nb == 0
    qkv = qkv.reshape(batch, t, N_QKV)
    ba = ba.reshape(batch, t, 4 * H_A)
    bat = bat.reshape(batch, nc, 4 * H_A, CHUNK)
    zero_init = s0 is None
    alog = a_log.reshape(1, 2 * H_A)
    dtb = dt_bias.reshape(1, 2 * H_A)
    alog_t = a_log.reshape(2 * H_A, 1)
    dtb_t = dt_bias.reshape(2 * H_A, 1)

    fw = lambda b, n: (b, n, 0)
    bw = lambda b, n: (b, nc - 1 - n, 0)
    small = lambda b, n: (0, 0)
    in_specs = [pl.BlockSpec((nb, CHUNK, N_QKV), fw),
                pl.BlockSpec((nb, CHUNK, N_QKV), bw),
                pl.BlockSpec((nb, CHUNK, 4 * H_A), fw),
                pl.BlockSpec((nb, CHUNK, 4 * H_A), bw),
                pl.BlockSpec((nb, 1, 4 * H_A, CHUNK), lambda b, n: (b, n, 0, 0)),
                pl.BlockSpec((nb, 1, 4 * H_A, CHUNK), lambda b, n: (b, nc - 1 - n, 0, 0)),
                pl.BlockSpec((1, 2 * H_A), small),
                pl.BlockSpec((1, 2 * H_A), small),
                pl.BlockSpec((2 * H_A, 1), small),
                pl.BlockSpec((2 * H_A, 1), small)]
    args = [qkv, qkv, ba, ba, bat, bat, alog, dtb, alog_t, dtb_t]
    state_spec = pl.BlockSpec((nb, 2, H_A, DK, DV), lambda b, n: (b, 0, 0, 0, 0))
    if not zero_init:
        in_specs.append(state_spec)
        args.append(s0)
    out_specs = [pl.BlockSpec((nb, CHUNK, W_A), fw), pl.BlockSpec((nb, CHUNK, W_A), bw)]
    out_shape = [jax.ShapeDtypeStruct((batch, t, W_A), F32),
                 jax.ShapeDtypeStruct((batch, t, W_A), F32)]
    if emit_state:
        out_specs.append(state_spec)
        out_shape.append(jax.ShapeDtypeStruct((batch, 2, H_A, DK, DV), F32))
    return pl.pallas_call(
        functools.partial(_delta_kernel, zero_init=zero_init, emit_state=emit_state),
        grid=(batch // nb, nc),
        in_specs=in_specs,
        out_specs=out_specs,
        out_shape=out_shape,
        scratch_shapes=[pltpu.VMEM((nb, 2, H_A // 2, 2 * DK, 2 * DV), F32)],
        compiler_params=pltpu.CompilerParams(dimension_semantics=("arbitrary", "arbitrary"),
                                             vmem_limit_bytes=VMEM_LIMIT),
        name="delta_scan",
    )(*args)


def _mix_kernel(x_ref, mod_ref, gain_ref, of_ref, ob_ref, wz_ref, wsc_ref, wg_ref, conv_ref,
                onorm_ref, wa_ref, wb_ref, wo_ref, out_ref, *, row_len):
    def sub_tile(rows):
        x = x_ref[rows, :]
        h = _rms(x, gain_ref[0:1, :]) * (1.0 + mod_ref[0, 1:2, :]) + mod_ref[0, 0:1, :]
        hb = h.astype(BF16)
        yield
        sc = _dot(hb, wsc_ref[...])
        o = of_ref[rows, :] + ob_ref[rows, :]
        slab = 2 * DV
        lo = lax.broadcasted_iota(jnp.int32, (SUB_TILE, slab), 1) < DV
        inv_rms = []
        for s in range(W_A // slab):
            o2 = o[:, s * slab:(s + 1) * slab]
            o2 = o2 * o2
            ms_lo = jnp.sum(jnp.where(lo, o2, 0.0), axis=-1, keepdims=True) * (1.0 / DV)
            ms_hi = jnp.sum(jnp.where(lo, 0.0, o2), axis=-1, keepdims=True) * (1.0 / DV)
            inv_rms.append(jnp.where(lo, lax.rsqrt(ms_lo + EPS), lax.rsqrt(ms_hi + EPS)))
        yield
        z = _dot(hb, wz_ref[...])
        bg, cg, xs = sc[:, 0:W_B], sc[:, W_B:2 * W_B], sc[:, 2 * W_B:3 * W_B]
        yb_in16 = (bg * _conv3_rows(cg * xs, conv_ref, row_len)).astype(BF16)
        yield
        gates = _dot(hb, wg_ref[...])
        oa16 = (o * jnp.concatenate(inv_rms, axis=-1) * onorm_ref[...] * _silu(z)).astype(BF16)
        yield
        y_a = _dot(oa16, wa_ref[...])
        y_b = _dot(yb_in16, wb_ref[...])
        yield
        m = (jax.nn.sigmoid(gates[:, 0:D_MODEL]) * y_a
             + jax.nn.sigmoid(gates[:, D_MODEL:]) * y_b)
        y = _dot(m.astype(BF16), wo_ref[...])
        yield
        out_ref[rows, :] = x + mod_ref[0, 2:3, :] * _rms(y, gain_ref[1:2, :])

    _lock_step(sub_tile(slice(r0, r0 + SUB_TILE)) for r0 in range(0, x_ref.shape[0], SUB_TILE))


def _mix(x, mod3, gains, o_f, o_b, w_z, w_sc, w_g, conv_sc, onorm_t, w_a, w_b, w_o,
         *, mod_row_fn, row_len):
    n_tok = x.shape[0]
    tm = WIDE_TILE
    const = lambda i: (0, 0)
    tok = lambda i: (i, 0)
    return pl.pallas_call(
        functools.partial(_mix_kernel, row_len=row_len),
        grid=(n_tok // tm,),
        in_specs=[pl.BlockSpec((tm, D_MODEL), tok),
                  pl.BlockSpec((1, 6, D_MODEL), lambda i: (mod_row_fn(i, tm), 0, 0)),
                  pl.BlockSpec((4, D_MODEL), const),
                  pl.BlockSpec((tm, W_A), tok),
                  pl.BlockSpec((tm, W_A), tok),
                  _resident((D_MODEL, W_A)),
                  _resident((D_MODEL, 3 * W_B)),
                  _resident((D_MODEL, 2 * D_MODEL)),
                  pl.BlockSpec((3, W_B), const),
                  pl.BlockSpec((1, W_A), const),
                  _resident((W_A, D_MODEL)),
                  _resident((W_B, D_MODEL)),
                  _resident((D_MODEL, D_MODEL))],
        out_specs=pl.BlockSpec((tm, D_MODEL), tok),
        out_shape=jax.ShapeDtypeStruct((n_tok, D_MODEL), F32),
        compiler_params=pltpu.CompilerParams(dimension_semantics=("arbitrary",),
                                             vmem_limit_bytes=VMEM_LIMIT),
        name="mix",
    )(x, mod3, gains, o_f, o_b, w_z, w_sc, w_g, conv_sc, onorm_t, w_a, w_b, w_o)


def _ffn_kernel(x_ref, mod_ref, gain_ref, win_ref, wout_ref, out_ref):
    def sub_tile(rows):
        x = x_ref[rows, :]
        h = _rms(x, gain_ref[2:3, :]) * (1.0 + mod_ref[0, 4:5, :]) + mod_ref[0, 3:4, :]
        hb = h.astype(BF16)
        yield
        gate = _dot(hb, win_ref[:, 0:D_FF])
        yield
        up = _dot(hb, win_ref[:, D_FF:])
        yield
        y = _dot((_silu(gate) * up).astype(BF16), wout_ref[...])
        yield
        out_ref[rows, :] = x + mod_ref[0, 5:6, :] * _rms(y, gain_ref[3:4, :])

    for g0 in range(0, x_ref.shape[0], 2 * SUB_TILE):
        _lock_step(sub_tile(slice(r0, r0 + SUB_TILE)) for r0 in (g0, g0 + SUB_TILE))


def _ffn(x, mod3, gains, w_in, w_out, *, mod_row_fn):
    n_tok = x.shape[0]
    tm = 2 * WIDE_TILE
    const = lambda i: (0, 0)
    tok = lambda i: (i, 0)
    return pl.pallas_call(
        _ffn_kernel,
        grid=(n_tok // tm,),
        in_specs=[pl.BlockSpec((tm, D_MODEL), tok),
                  pl.BlockSpec((1, 6, D_MODEL), lambda i: (mod_row_fn(i, tm), 0, 0)),
                  pl.BlockSpec((4, D_MODEL), const),
                  _resident((D_MODEL, 2 * D_FF)),
                  _resident((D_FF, D_MODEL))],
        out_specs=pl.BlockSpec((tm, D_MODEL), tok),
        out_shape=jax.ShapeDtypeStruct((n_tok, D_MODEL), F32),
        compiler_params=pltpu.CompilerParams(dimension_semantics=("arbitrary",),
                                             vmem_limit_bytes=VMEM_LIMIT),
        name="ffn",
    )(x, mod3, gains, w_in, w_out)


def _to_slab_order(src_ref, dst_ref):
    slab = 2 * DK
    left = lax.broadcasted_iota(jnp.int32, (src_ref.shape[1], slab), 1) < DK
    for p in range(H_A // 2):
        q_p = src_ref[0, :, p * slab:(p + 1) * slab]
        k_p = src_ref[0, :, W_A + p * slab:W_A + (p + 1) * slab]
        v_p = src_ref[0, :, 2 * W_A + p * slab:2 * W_A + (p + 1) * slab]
        k_swapped = pltpu.roll(k_p, DK, 1)
        dst_ref[:, p * slab:(p + 1) * slab] = pltpu.roll(q_p, DK, 1).astype(dst_ref.dtype)
        even = W_A + 2 * p * slab
        dst_ref[:, even:even + slab] = jnp.where(left, v_p, k_swapped).astype(dst_ref.dtype)
        dst_ref[:, even + slab:even + 2 * slab] = (
            jnp.where(left, k_swapped, v_p).astype(dst_ref.dtype))


def _split_w_in_kernel(w_ref, conv_ref, qkv_ref, z_ref, ba_ref, sc_ref, g_ref, conv_out_ref):
    _to_slab_order(w_ref, qkv_ref)
    _to_slab_order(conv_ref, conv_out_ref)
    o_z, o_ba = N_QKV, N_QKV + W_A
    o_sc = o_ba + 4 * H_A
    o_g = o_sc + 3 * W_B
    z_ref[...] = w_ref[0, :, o_z:o_ba].astype(BF16)
    ba_ref[...] = w_ref[0, :, o_ba:o_sc].astype(BF16)
    sc_ref[...] = w_ref[0, :, o_sc:o_g].astype(BF16)
    g_ref[...] = w_ref[0, :, o_g:].astype(BF16)


def _split_w_in(w, conv_qkv, layer):
    _, rows, n_in = w.shape
    taps = conv_qkv.shape[1]
    blk = 128
    widths = (N_QKV, W_A, 4 * H_A, 3 * W_B, 2 * D_MODEL)
    assert sum(widths) == n_in and rows % blk == 0
    return pl.pallas_call(
        _split_w_in_kernel,
        grid=(rows // blk,),
        in_specs=[pl.BlockSpec((1, blk, n_in), lambda i: (layer, i, 0)),
                  pl.BlockSpec((1, taps, N_QKV), lambda i: (layer, 0, 0))],
        out_specs=[pl.BlockSpec((blk, n), lambda i: (i, 0)) for n in widths]
        + [pl.BlockSpec((taps, N_QKV), lambda i: (0, 0))],
        out_shape=[jax.ShapeDtypeStruct((rows, n), BF16) for n in widths]
        + [jax.ShapeDtypeStruct((taps, N_QKV), conv_qkv.dtype)],
        name="split_w_in",
    )(w, conv_qkv)


def kernel(x_prompt, x_sample, state_delta, c, c_ctx, w_mod, b_mod, norm_gains, w_in, conv_qkv,
           a_log, dt_bias, onorm, conv_sc, w_branch_a, w_branch_b, w_out, w_ffn_in, w_ffn_out):
    depth = w_in.shape[0]
    assert depth == 1
    b_ctx, t_ctx, _ = x_prompt.shape
    b_lat, t_lat, _ = x_sample.shape
    assert SUB_TILE % t_ctx == 0 and SUB_TILE % GRID_W == 0 and t_ctx & (t_ctx - 1) == 0
    assert WIDE_TILE % SUB_TILE == 0 and t_lat % WIDE_TILE == 0
    assert (b_ctx * t_ctx) % WIDE_TILE == 0
    assert b_lat + 1 <= MOD_ROWS
    layer = 0

    cvec = jnp.zeros((MOD_ROWS, D_MODEL), F32).at[0].set(c_ctx).at[1:1 + b_lat].set(c)
    mod3 = _modulation(cvec, w_mod, b_mod, layer).reshape(MOD_ROWS, 6, D_MODEL)

    w_qkv, w_z, w_ba, w_sc, w_g, conv_perm = _split_w_in(w_in, conv_qkv, layer)
    w_bat = w_ba.T
    w_a = w_branch_a[layer].astype(BF16)
    w_b = w_branch_b[layer].astype(BF16)
    w_o = w_out[layer].astype(BF16)
    w_f1 = w_ffn_in[layer].astype(BF16)
    w_f2 = w_ffn_out[layer].astype(BF16)
    gains = norm_gains[layer]
    onorm_t = jnp.tile(onorm[layer], H_A)[None, :]

    def run_group(x3, s0, mod_row_fn, row_len, emit_state):
        batch, t, _ = x3.shape
        x = x3.reshape(batch * t, D_MODEL)
        qkv, ba, bat = _inproj(x, mod3, gains, w_qkv, w_ba, w_bat, conv_perm,
                               mod_row_fn=mod_row_fn, row_len=row_len)
        res = _delta_scan(qkv, ba, bat, a_log[layer], dt_bias[layer], s0,
                          batch=batch, emit_state=emit_state)
        o_f = res[0].reshape(batch * t, W_A)
        o_b = res[1].reshape(batch * t, W_A)
        x1 = _mix(x, mod3, gains, o_f, o_b, w_z, w_sc, w_g, conv_sc[layer], onorm_t,
                  w_a, w_b, w_o, mod_row_fn=mod_row_fn, row_len=row_len)
        x2 = _ffn(x1, mod3, gains, w_f1, w_f2, mod_row_fn=mod_row_fn)
        return x2.reshape(batch, t, D_MODEL), (res[2] if emit_state else None)

    y_p, s_fin = run_group(x_prompt, None, lambda i, tm: 0, t_ctx, True)
    y_s, _ = run_group(x_sample, state_delta[:, layer], lambda i, tm: 1 + (i * tm) // t_lat,
                       GRID_W, False)
    new_state = s_fin[:, None].astype(x_prompt.dtype)
    return (y_p, y_s, new_state)
```

```python
import functools

import jax
import jax.numpy as jnp
from jax import lax
from jax.experimental import pallas as pl
from jax.experimental.pallas import tpu as pltpu

D_MODEL = 1024
H_A = 8
DK = 64
DV = 64
W_A = H_A * DV
W_B = 512
CHUNK = 64
GRID_W = 64
EPS = 1e-6
N_QKV = 3 * H_A * DK
D_FF = 2816
SUBLANES = 8
MOD_ROWS = SUBLANES
WIDE_TILE = 512
SUB_TILE = 256
DELTA_BATCH = 4
VMEM_LIMIT = 56 * 1024 * 1024

F32 = jnp.float32
BF16 = jnp.bfloat16


def _dot(a, b):
    return jnp.dot(a, b, preferred_element_type=F32)


def _dot_nt(a, b):
    return lax.dot_general(a, b, (((1,), (1,)), ((), ())), preferred_element_type=F32)


def _dot_tn(a, b):
    return lax.dot_general(a, b, (((0,), (0,)), ((), ())), preferred_element_type=F32)


def _resident(shape):
    return pl.BlockSpec(shape, lambda i: (0,) * len(shape), pipeline_mode=pl.Buffered(1))


def _lock_step(programs):
    programs = list(programs)
    while programs:
        for prog in list(programs):
            try:
                next(prog)
            except StopIteration:
                programs.remove(prog)


def _silu(x):
    return x * jax.nn.sigmoid(x)


def _rms(x, gain):
    return x * lax.rsqrt(jnp.mean(x * x, axis=-1, keepdims=True) + EPS) * gain


def _conv3_rows(x, w_ref, row_len):
    n = x.shape[0]
    pos = lax.broadcasted_iota(jnp.int32, (n, 1), 0) & (row_len - 1)
    prev = jnp.where(pos == 0, 0.0, pltpu.roll(x, 1, 0))
    nxt = jnp.where(pos == row_len - 1, 0.0, pltpu.roll(x, n - 1, 0))
    return prev * w_ref[0:1, :] + x * w_ref[1:2, :] + nxt * w_ref[2:3, :]


def _mod_kernel(c_ref, w_ref, b_ref, o_ref):
    rows = c_ref.shape[0]
    s3 = jnp.concatenate([p.astype(F32) for p in _split3(_silu(c_ref[...]))], axis=0).astype(BF16)
    w = w_ref[0]
    w_hi = w.astype(BF16)
    w_lo = (w - w_hi.astype(F32)).astype(BF16)
    acc = _dot(s3, w_hi) + _dot(s3, w_lo)
    o_ref[...] = acc[0:rows] + acc[rows:2 * rows] + acc[2 * rows:] + b_ref[...]


def _modulation(cvec, w_mod, b_mod, layer):
    n_out = w_mod.shape[2]
    blk = D_MODEL
    return pl.pallas_call(
        _mod_kernel,
        grid=(n_out // blk,),
        in_specs=[pl.BlockSpec((MOD_ROWS, D_MODEL), lambda j: (0, 0)),
                  pl.BlockSpec((1, D_MODEL, blk), lambda j: (layer, 0, j)),
                  pl.BlockSpec((1, blk), lambda j: (layer, j))],
        out_specs=pl.BlockSpec((MOD_ROWS, blk), lambda j: (0, j)),
        out_shape=jax.ShapeDtypeStruct((MOD_ROWS, n_out), F32),
        name="modulation",
    )(cvec, w_mod, b_mod)


def _inproj_kernel(x_ref, mod_ref, gain_ref, wqkv_ref, wba_ref, wbat_ref, conv_ref,
                   qkv_ref, ba_ref, bat_ref, *, row_len):
    slab = 2 * DK
    lo = lax.broadcasted_iota(jnp.int32, (SUB_TILE, slab), 1) < DK
    ssq = lambda v, m: jnp.sum(jnp.where(m, v * v, 0.0), axis=-1, keepdims=True)

    def sub_tile(r0):
        rows = slice(r0, r0 + SUB_TILE)
        x = x_ref[rows, :]
        h = _rms(x, gain_ref[0:1, :]) * (1.0 + mod_ref[0, 1:2, :]) + mod_ref[0, 0:1, :]
        hb = h.astype(BF16)
        yield
        for pair in range(N_QKV // (2 * slab)):
            cols2 = slice(pair * 2 * slab, (pair + 1) * 2 * slab)
            v2 = _silu(_conv3_rows(_dot(hb, wqkv_ref[:, cols2]), conv_ref.at[:, cols2], row_len))
            for half in range(2):
                s = 2 * pair + half
                cols = slice(s * slab, (s + 1) * slab)
                v = v2[:, half * slab:(half + 1) * slab]
                if s < W_A // slab:
                    r = jnp.where(lo, lax.rsqrt(ssq(v, lo) + EPS) * (DK ** -0.5),
                                  lax.rsqrt(ssq(v, ~lo) + EPS) * (DK ** -0.5))
                    qkv_ref[rows, cols] = (v * r).astype(qkv_ref.dtype)
                else:
                    kmask = lo if (s - W_A // slab) % 2 else ~lo
                    qkv_ref[rows, cols] = (v * jnp.where(
                        kmask, lax.rsqrt(ssq(v, kmask) + EPS), 1.0)).astype(qkv_ref.dtype)
            yield
        ba_ref[rows, :] = _dot(hb, wba_ref[...])
        for ci in range(SUB_TILE // CHUNK):
            bat_ref[r0 // CHUNK + ci] = _dot_nt(wbat_ref[...], hb[ci * CHUNK:(ci + 1) * CHUNK, :])

    _lock_step(sub_tile(r0) for r0 in range(0, x_ref.shape[0], SUB_TILE))


def _inproj(x, mod3, gains, w_qkv, w_ba, w_bat, conv_qkv, *, mod_row_fn, row_len):
    n_tok = x.shape[0]
    tm = WIDE_TILE
    const = lambda i: (0, 0)
    return pl.pallas_call(
        functools.partial(_inproj_kernel, row_len=row_len),
        grid=(n_tok // tm,),
        in_specs=[pl.BlockSpec((tm, D_MODEL), lambda i: (i, 0)),
                  pl.BlockSpec((1, 6, D_MODEL), lambda i: (mod_row_fn(i, tm), 0, 0)),
                  pl.BlockSpec((4, D_MODEL), const),
                  pl.BlockSpec((D_MODEL, N_QKV), const),
                  pl.BlockSpec((D_MODEL, 4 * H_A), const),
                  pl.BlockSpec((4 * H_A, D_MODEL), const),
                  pl.BlockSpec((3, N_QKV), const)],
        out_specs=[pl.BlockSpec((tm, N_QKV), lambda i: (i, 0)),
                   pl.BlockSpec((tm, 4 * H_A), lambda i: (i, 0)),
                   pl.BlockSpec((tm // CHUNK, 4 * H_A, CHUNK), lambda i: (i, 0, 0))],
        out_shape=[jax.ShapeDtypeStruct((n_tok, N_QKV), BF16),
                   jax.ShapeDtypeStruct((n_tok, 4 * H_A), F32),
                   jax.ShapeDtypeStruct((n_tok // CHUNK, 4 * H_A, CHUNK), F32)],
        compiler_params=pltpu.CompilerParams(dimension_semantics=("arbitrary",),
                                             vmem_limit_bytes=VMEM_LIMIT),
        name="inproj",
    )(x, mod3, gains, w_qkv, w_ba, w_bat, conv_qkv)


def _unit_tri_inverse_stages(lows, lower):
    c = CHUNK
    ri = lax.broadcasted_iota(jnp.int32, (c, c), 0)
    ci = lax.broadcasted_iota(jnp.int32, (c, c), 1)
    eye = (ri == ci).astype(F32)
    rng = range(len(lows))
    joins = lambda b: ((ri // (2 * b)) == (ci // (2 * b))) & ((ri // b) != (ci // b))
    t = [eye - jnp.where(joins(1), lows[i], 0.0) for i in rng]
    b = 2
    while b < c:
        mask = joins(b)
        t16 = [t[i].astype(BF16) for i in rng]
        e16 = [jnp.where(mask, lows[i], 0.0).astype(BF16) for i in rng]
        if b % SUBLANES:
            x = [_dot(t16[i], e16[i]) for i in rng]
            yield
            t = [t[i] - _dot(x[i].astype(BF16), t16[i]) for i in rng]
            yield
        else:
            first = [slice(s, s + b) for s in range(0, c, 2 * b)]
            second = [slice(s + b, s + 2 * b) for s in range(0, c, 2 * b)]
            keep, upd = (first, second) if lower else (second, first)
            t_upd = [jnp.concatenate([t[i][r] for r in upd], axis=0) for i in rng]
            x = [_dot(t_upd[i].astype(BF16), e16[i]) for i in rng]
            yield
            t_upd = [t_upd[i] - _dot(x[i].astype(BF16), t16[i]) for i in rng]
            yield
            pair = lambda kept, new: (kept, new) if lower else (new, kept)
            t = [jnp.concatenate(
                [piece for k, r in enumerate(keep)
                 for piece in pair(t[i][r], t_upd[i][k * b:(k + 1) * b])], axis=0) for i in rng]
        b *= 2
    return t


def _split3(x):
    pieces = []
    for _ in range(3):
        p = x.astype(BF16)
        x = x - p.astype(F32)
        pieces.append(p)
    return pieces


def _group_stages(qkv_ref, o_ref, s_ref, bb, d, gates):
    c = CHUNK
    fwd = d == 0
    row = lax.broadcasted_iota(jnp.int32, (c, c), 0)
    col = lax.broadcasted_iota(jnp.int32, (c, c), 1)
    incl = (row >= col) if fwd else (row <= col)
    strict = (row > col) if fwd else (row < col)
    lane = lax.broadcasted_iota(jnp.int32, (c, 2 * DK), 1)
    beta_all, beta_all_t, gc_all, gc_all_t, eg_all, tail_all, eglast_all = gates
    heads = range(H_A)
    kmask = [(lane < DK) if h % 2 else (lane >= DK) for h in heads]
    kv_slab = lambda h: qkv_ref[bb, :, W_A + h * 2 * DK:W_A + (h + 1) * 2 * DK].astype(F32)
    q_slab = lambda h: qkv_ref[bb, :, (h // 2) * 2 * DK:(h // 2 + 1) * 2 * DK].astype(F32)

    decay, kq = [], []
    for h in heads:
        j = d * H_A + h
        kv = kv_slab(h)
        lhs16 = jnp.concatenate([kv * beta_all[:, j:j + 1], q_slab(h)], axis=0).astype(BF16)
        kq.append(_dot_nt(lhs16, jnp.where(kmask[h], kv, 0.0).astype(BF16)))
        diff = gc_all[:, j:j + 1] - gc_all_t[j:j + 1, :]
        decay.append(jnp.where(incl, jnp.exp(jnp.where(incl, diff, 0.0)), 0.0))
    yield

    lows = [jnp.where(strict, kq[h][:c] * decay[h], 0.0) for h in heads]
    attn16 = [(kq[h][c:] * decay[h]).astype(BF16) for h in heads]
    tmat = yield from _unit_tri_inverse_stages(lows, lower=fwd)

    uw, eg = [], []
    for h in heads:
        j = d * H_A + h
        eg.append(jnp.broadcast_to(eg_all[:, j:j + 1], (c, 2 * DK)))
        kv = kv_slab(h)
        t_beta16 = (tmat[h] * beta_all_t[j:j + 1, :]).astype(BF16)
        uw.append(_dot(t_beta16, jnp.where(kmask[h], kv * eg[h], kv).astype(BF16)))
    yield

    pairs = range(H_A // 2)
    s_old = [s_ref[bb, d, p] for p in pairs]
    qh = [q_slab(2 * p) * jnp.where(kmask[2 * p + 1], eg[2 * p + 1], eg[2 * p]) for p in pairs]
    wq = [_dot(jnp.concatenate([uw[2 * p], uw[2 * p + 1], qh[p]], axis=0).astype(BF16),
               s_old[p].astype(BF16)) for p in pairs]
    yield

    v_new16 = [jnp.where(kmask[h], 0.0, uw[h] - wq[h // 2][(h % 2) * c:(h % 2 + 1) * c])
               .astype(BF16) for h in heads]
    for p in pairs:
        o_ref[bb, :, 2 * p * DV:(2 * p + 2) * DV] = (
            wq[p][2 * c:] + _dot(attn16[2 * p], v_new16[2 * p])
            + _dot(attn16[2 * p + 1], v_new16[2 * p + 1]))
    for h in heads:
        j = d * H_A + h
        k0 = 0 if h % 2 else DK
        kt16 = (kv_slab(h)[:, k0:k0 + DK] * tail_all[:, j:j + 1]).astype(BF16)
        s_ref[bb, d, h // 2, k0:k0 + DK, :] = (
            s_old[h // 2][k0:k0 + DK] * eglast_all[:, j:j + 1] + _dot_tn(kt16, v_new16[h]))


def _delta_kernel(*refs, zero_init, emit_state):
    (qkv_f_ref, qkv_b_ref, ba_f_ref, ba_b_ref, bat_f_ref, bat_b_ref,
     alog_ref, dtb_ref, alog_t_ref, dtb_t_ref) = refs[:10]
    rest = list(refs[10:])
    s0_ref = None if zero_init else rest.pop(0)
    o_f_ref, o_b_ref = rest.pop(0), rest.pop(0)
    sfin_ref = rest.pop(0) if emit_state else None
    s_ref = rest.pop(0)

    n = pl.program_id(1)
    n_chunks = pl.num_programs(1)
    nb = s_ref.shape[0]
    state_at = lambda h: (0, DV) if h % 2 else (DK, 0)

    @pl.when(n == 0)
    def _():
        s_ref[...] = jnp.zeros(s_ref.shape, F32)
        if not zero_init:
            for bb in range(nb):
                for d in range(2):
                    for h in range(H_A):
                        r0, l0 = state_at(h)
                        s_ref[bb, d, h // 2, r0:r0 + DK, l0:l0 + DV] = s0_ref[bb, d, h]

    c = CHUNK
    row = lax.broadcasted_iota(jnp.int32, (c, c), 0)
    col = lax.broadcasted_iota(jnp.int32, (c, c), 1)
    tril16 = (row >= col).astype(BF16)
    triu16 = (row <= col).astype(BF16)

    programs = []
    for bb in range(nb):
        for d, (qkv_ref, ba_ref, bat_ref, o_ref) in enumerate(
                ((qkv_f_ref, ba_f_ref, bat_f_ref, o_f_ref),
                 (qkv_b_ref, ba_b_ref, bat_b_ref, o_b_ref))):
            fwd = d == 0
            csum16, csum_t16 = (tril16, triu16) if fwd else (triu16, tril16)
            last = c - 1 if fwd else 0
            ba = ba_ref[bb]
            bat = bat_ref[bb, 0]
            beta_all = jax.nn.sigmoid(ba[:, 0:2 * H_A])
            g_all = -jnp.exp(alog_ref[...]) * jax.nn.softplus(ba[:, 2 * H_A:] + dtb_ref[...])
            g_all_t = -jnp.exp(alog_t_ref[...]) * jax.nn.softplus(
                bat[2 * H_A:, :] + dtb_t_ref[...])
            gc_all = sum(_dot(csum16, p) for p in _split3(g_all))
            gc_all_t = sum(_dot(p, csum_t16) for p in _split3(g_all_t))
            glast_all = gc_all[last:last + 1, :]
            beta_all_t = jax.nn.sigmoid(bat[0:2 * H_A, :])
            gates = (beta_all, beta_all_t, gc_all, gc_all_t, jnp.exp(gc_all),
                     jnp.exp(glast_all - gc_all), jnp.exp(glast_all))
            programs.append(_group_stages(qkv_ref, o_ref, s_ref, bb, d, gates))

    _lock_step(programs)

    if emit_state:
        @pl.when(n == n_chunks - 1)
        def _():
            for bb in range(nb):
                for d in range(2):
                    for h in range(H_A):
                        r0, l0 = state_at(h)
                        sfin_ref[bb, d, h] = s_ref[bb, d, h // 2, r0:r0 + DK, l0:l0 + DV]


def _delta_scan(qkv, ba, bat, a_log, dt_bias, s0, *, batch, emit_state):
    n_tok = qkv.shape[0]
    t = n_tok // batch
    nc = t // CHUNK
    nb = min(DELTA_BATCH, batch)
    assert batch % nb == 0
    qkv = qkv.reshape(batch, t, N_QKV)
    ba = ba.reshape(batch, t, 4 * H_A)
    bat = bat.reshape(batch, nc, 4 * H_A, CHUNK)
    zero_init = s0 is None
    alog = a_log.reshape(1, 2 * H_A)
    dtb = dt_bias.reshape(1, 2 * H_A)
    alog_t = a_log.reshape(2 * H_A, 1)
    dtb_t = dt_bias.reshape(2 * H_A, 1)

    fw = lambda b, n: (b, n, 0)
    bw = lambda b, n: (b, nc - 1 - n, 0)
    small = lambda b, n: (0, 0)
    in_specs = [pl.BlockSpec((nb, CHUNK, N_QKV), fw),
                pl.BlockSpec((nb, CHUNK, N_QKV), bw),
                pl.BlockSpec((nb, CHUNK, 4 * H_A), fw),
                pl.BlockSpec((nb, CHUNK, 4 * H_A), bw),
                pl.BlockSpec((nb, 1, 4 * H_A, CHUNK), lambda b, n: (b, n, 0, 0)),
                pl.BlockSpec((nb, 1, 4 * H_A, CHUNK), lambda b, n: (b, nc - 1 - n, 0, 0)),
                pl.BlockSpec((1, 2 * H_A), small),
                pl.BlockSpec((1, 2 * H_A), small),
                pl.BlockSpec((2 * H_A, 1), small),
                pl.BlockSpec((2 * H_A, 1), small)]
    args = [qkv, qkv, ba, ba, bat, bat, alog, dtb, alog_t, dtb_t]
    state_spec = pl.BlockSpec((nb, 2, H_A, DK, DV), lambda b, n: (b, 0, 0, 0, 0))
    if not zero_init:
        in_specs.append(state_spec)
        args.append(s0)
    out_specs = [pl.BlockSpec((nb, CHUNK, W_A), fw), pl.BlockSpec((nb, CHUNK, W_A), bw)]
    out_shape = [jax.ShapeDtypeStruct((batch, t, W_A), F32),
                 jax.ShapeDtypeStruct((batch, t, W_A), F32)]
    if emit_state:
        out_specs.append(state_spec)
        out_shape.append(jax.ShapeDtypeStruct((batch, 2, H_A, DK, DV), F32))
    return pl.pallas_call(
        functools.partial(_delta_kernel, zero_init=zero_init, emit_state=emit_state),
        grid=(batch // nb, nc),
        in_specs=in_specs,
        out_specs=out_specs,
        out_shape=out_shape,
        scratch_shapes=[pltpu.VMEM((nb, 2, H_A // 2, 2 * DK, 2 * DV), F32)],
        compiler_params=pltpu.CompilerParams(dimension_semantics=("arbitrary", "arbitrary"),
                                             vmem_limit_bytes=VMEM_LIMIT),
        name="delta_scan",
    )(*args)


def _mix_kernel(x_ref, mod_ref, gain_ref, of_ref, ob_ref, wz_ref, wsc_ref, wg_ref, conv_ref,
                onorm_ref, wa_ref, wb_ref, wo_ref, out_ref, *, row_len):
    def sub_tile(rows):
        x = x_ref[rows, :]
        h = _rms(x, gain_ref[0:1, :]) * (1.0 + mod_ref[0, 1:2, :]) + mod_ref[0, 0:1, :]
        hb = h.astype(BF16)
        yield
        sc = _dot(hb, wsc_ref[...])
        o = of_ref[rows, :] + ob_ref[rows, :]
        slab = 2 * DV
        lo = lax.broadcasted_iota(jnp.int32, (SUB_TILE, slab), 1) < DV
        inv_rms = []
        for s in range(W_A // slab):
            o2 = o[:, s * slab:(s + 1) * slab]
            o2 = o2 * o2
            ms_lo = jnp.sum(jnp.where(lo, o2, 0.0), axis=-1, keepdims=True) * (1.0 / DV)
            ms_hi = jnp.sum(jnp.where(lo, 0.0, o2), axis=-1, keepdims=True) * (1.0 / DV)
            inv_rms.append(jnp.where(lo, lax.rsqrt(ms_lo + EPS), lax.rsqrt(ms_hi + EPS)))
        yield
        z = _dot(hb, wz_ref[...])
        bg, cg, xs = sc[:, 0:W_B], sc[:, W_B:2 * W_B], sc[:, 2 * W_B:3 * W_B]
        yb_in16 = (bg * _conv3_rows(cg * xs, conv_ref, row_len)).astype(BF16)
        yield
        gates = _dot(hb, wg_ref[...])
        oa16 = (o * jnp.concatenate(inv_rms, axis=-1) * onorm_ref[...] * _silu(z)).astype(BF16)
        yield
        y_a = _dot(oa16, wa_ref[...])
        y_b = _dot(yb_in16, wb_ref[...])
        yield
        m = (jax.nn.sigmoid(gates[:, 0:D_MODEL]) * y_a
             + jax.nn.sigmoid(gates[:, D_MODEL:]) * y_b)
        y = _dot(m.astype(BF16), wo_ref[...])
        yield
        out_ref[rows, :] = x + mod_ref[0, 2:3, :] * _rms(y, gain_ref[1:2, :])

    _lock_step(sub_tile(slice(r0, r0 + SUB_TILE)) for r0 in range(0, x_ref.shape[0], SUB_TILE))


def _mix(x, mod3, gains, o_f, o_b, w_z, w_sc, w_g, conv_sc, onorm_t, w_a, w_b, w_o,
         *, mod_row_fn, row_len):
    n_tok = x.shape[0]
    tm = WIDE_TILE
    const = lambda i: (0, 0)
    tok = lambda i: (i, 0)
    return pl.pallas_call(
        functools.partial(_mix_kernel, row_len=row_len),
        grid=(n_tok // tm,),
        in_specs=[pl.BlockSpec((tm, D_MODEL), tok),
                  pl.BlockSpec((1, 6, D_MODEL), lambda i: (mod_row_fn(i, tm), 0, 0)),
                  pl.BlockSpec((4, D_MODEL), const),
                  pl.BlockSpec((tm, W_A), tok),
                  pl.BlockSpec((tm, W_A), tok),
                  _resident((D_MODEL, W_A)),
                  _resident((D_MODEL, 3 * W_B)),
                  _resident((D_MODEL, 2 * D_MODEL)),
                  pl.BlockSpec((3, W_B), const),
                  pl.BlockSpec((1, W_A), const),
                  _resident((W_A, D_MODEL)),
                  _resident((W_B, D_MODEL)),
                  _resident((D_MODEL, D_MODEL))],
        out_specs=pl.BlockSpec((tm, D_MODEL), tok),
        out_shape=jax.ShapeDtypeStruct((n_tok, D_MODEL), F32),
        compiler_params=pltpu.CompilerParams(dimension_semantics=("arbitrary",),
                                             vmem_limit_bytes=VMEM_LIMIT),
        name="mix",
    )(x, mod3, gains, o_f, o_b, w_z, w_sc, w_g, conv_sc, onorm_t, w_a, w_b, w_o)


def _ffn_kernel(x_ref, mod_ref, gain_ref, win_ref, wout_ref, out_ref):
    def sub_tile(rows):
        x = x_ref[rows, :]
        h = _rms(x, gain_ref[2:3, :]) * (1.0 + mod_ref[0, 4:5, :]) + mod_ref[0, 3:4, :]
        hb = h.astype(BF16)
        yield
        gate = _dot(hb, win_ref[:, 0:D_FF])
        yield
        up = _dot(hb, win_ref[:, D_FF:])
        yield
        y = _dot((_silu(gate) * up).astype(BF16), wout_ref[...])
        yield
        out_ref[rows, :] = x + mod_ref[0, 5:6, :] * _rms(y, gain_ref[3:4, :])

    _lock_step(sub_tile(slice(r0, r0 + SUB_TILE)) for r0 in range(0, x_ref.shape[0], SUB_TILE))


def _ffn(x, mod3, gains, w_in, w_out, *, mod_row_fn):
    n_tok = x.shape[0]
    tm = WIDE_TILE
    const = lambda i: (0, 0)
    tok = lambda i: (i, 0)
    return pl.pallas_call(
        _ffn_kernel,
        grid=(n_tok // tm,),
        in_specs=[pl.BlockSpec((tm, D_MODEL), tok),
                  pl.BlockSpec((1, 6, D_MODEL), lambda i: (mod_row_fn(i, tm), 0, 0)),
                  pl.BlockSpec((4, D_MODEL), const),
                  _resident((D_MODEL, 2 * D_FF)),
                  _resident((D_FF, D_MODEL))],
        out_specs=pl.BlockSpec((tm, D_MODEL), tok),
        out_shape=jax.ShapeDtypeStruct((n_tok, D_MODEL), F32),
        compiler_params=pltpu.CompilerParams(dimension_semantics=("arbitrary",),
                                             vmem_limit_bytes=VMEM_LIMIT),
        name="ffn",
    )(x, mod3, gains, w_in, w_out)


def _to_slab_order(src_ref, dst_ref):
    slab = 2 * DK
    left = lax.broadcasted_iota(jnp.int32, (src_ref.shape[1], slab), 1) < DK
    for p in range(H_A // 2):
        q_p = src_ref[0, :, p * slab:(p + 1) * slab]
        k_p = src_ref[0, :, W_A + p * slab:W_A + (p + 1) * slab]
        v_p = src_ref[0, :, 2 * W_A + p * slab:2 * W_A + (p + 1) * slab]
        k_swapped = pltpu.roll(k_p, DK, 1)
        dst_ref[:, p * slab:(p + 1) * slab] = pltpu.roll(q_p, DK, 1).astype(dst_ref.dtype)
        even = W_A + 2 * p * slab
        dst_ref[:, even:even + slab] = jnp.where(left, v_p, k_swapped).astype(dst_ref.dtype)
        dst_ref[:, even + slab:even + 2 * slab] = (
            jnp.where(left, k_swapped, v_p).astype(dst_ref.dtype))


def _split_w_in_kernel(w_ref, conv_ref, qkv_ref, z_ref, ba_ref, sc_ref, g_ref, conv_out_ref):
    _to_slab_order(w_ref, qkv_ref)
    _to_slab_order(conv_ref, conv_out_ref)
    o_z, o_ba = N_QKV, N_QKV + W_A
    o_sc = o_ba + 4 * H_A
    o_g = o_sc + 3 * W_B
    z_ref[...] = w_ref[0, :, o_z:o_ba].astype(BF16)
    ba_ref[...] = w_ref[0, :, o_ba:o_sc].astype(BF16)
    sc_ref[...] = w_ref[0, :, o_sc:o_g].astype(BF16)
    g_ref[...] = w_ref[0, :, o_g:].astype(BF16)


def _split_w_in(w, conv_qkv, layer):
    _, rows, n_in = w.shape
    taps = conv_qkv.shape[1]
    blk = 128
    widths = (N_QKV, W_A, 4 * H_A, 3 * W_B, 2 * D_MODEL)
    assert sum(widths) == n_in and rows % blk == 0
    return pl.pallas_call(
        _split_w_in_kernel,
        grid=(rows // blk,),
        in_specs=[pl.BlockSpec((1, blk, n_in), lambda i: (layer, i, 0)),
                  pl.BlockSpec((1, taps, N_QKV), lambda i: (layer, 0, 0))],
        out_specs=[pl.BlockSpec((blk, n), lambda i: (i, 0)) for n in widths]
        + [pl.BlockSpec((taps, N_QKV), lambda i: (0, 0))],
        out_shape=[jax.ShapeDtypeStruct((rows, n), BF16) for n in widths]
        + [jax.ShapeDtypeStruct((taps, N_QKV), conv_qkv.dtype)],
        name="split_w_in",
    )(w, conv_qkv)


def kernel(x_prompt, x_sample, state_delta, c, c_ctx, w_mod, b_mod, norm_gains, w_in, conv_qkv,
           a_log, dt_bias, onorm, conv_sc, w_branch_a, w_branch_b, w_out, w_ffn_in, w_ffn_out):
    depth = w_in.shape[0]
    assert depth == 1
    b_ctx, t_ctx, _ = x_prompt.shape
    b_lat, t_lat, _ = x_sample.shape
    assert SUB_TILE % t_ctx == 0 and SUB_TILE % GRID_W == 0 and t_ctx & (t_ctx - 1) == 0
    assert WIDE_TILE % SUB_TILE == 0 and t_lat % WIDE_TILE == 0
    assert (b_ctx * t_ctx) % WIDE_TILE == 0
    assert b_lat + 1 <= MOD_ROWS
    layer = 0

    cvec = jnp.zeros((MOD_ROWS, D_MODEL), F32).at[0].set(c_ctx).at[1:1 + b_lat].set(c)
    mod3 = _modulation(cvec, w_mod, b_mod, layer).reshape(MOD_ROWS, 6, D_MODEL)

    w_qkv, w_z, w_ba, w_sc, w_g, conv_perm = _split_w_in(w_in, conv_qkv, layer)
    w_bat = w_ba.T
    w_a = w_branch_a[layer].astype(BF16)
    w_b = w_branch_b[layer].astype(BF16)
    w_o = w_out[layer].astype(BF16)
    w_f1 = w_ffn_in[layer].astype(BF16)
    w_f2 = w_ffn_out[layer].astype(BF16)
    gains = norm_gains[layer]
    onorm_t = jnp.tile(onorm[layer], H_A)[None, :]

    def run_group(x3, s0, mod_row_fn, row_len, emit_state):
        batch, t, _ = x3.shape
        x = x3.reshape(batch * t, D_MODEL)
        qkv, ba, bat = _inproj(x, mod3, gains, w_qkv, w_ba, w_bat, conv_perm,
                               mod_row_fn=mod_row_fn, row_len=row_len)
        res = _delta_scan(qkv, ba, bat, a_log[layer], dt_bias[layer], s0,
                          batch=batch, emit_state=emit_state)
        o_f = res[0].reshape(batch * t, W_A)
        o_b = res[1].reshape(batch * t, W_A)
        x1 = _mix(x, mod3, gains, o_f, o_b, w_z, w_sc, w_g, conv_sc[layer], onorm_t,
                  w_a, w_b, w_o, mod_row_fn=mod_row_fn, row_len=row_len)
        x2 = _ffn(x1, mod3, gains, w_f1, w_f2, mod_row_fn=mod_row_fn)
        return x2.reshape(batch, t, D_MODEL), (res[2] if emit_state else None)

    y_p, s_fin = run_group(x_prompt, None, lambda i, tm: 0, t_ctx, True)
    y_s, _ = run_group(x_sample, state_delta[:, layer], lambda i, tm: 1 + (i * tm) // t_lat,
                       GRID_W, False)
    new_state = s_fin[:, None].astype(x_prompt.dtype)
    return (y_p, y_s, new_state)
```

```python
import functools

import jax
import jax.numpy as jnp
from jax import lax
from jax.experimental import pallas as pl
from jax.experimental.pallas import tpu as pltpu

D_MODEL = 1024
H_A = 8
DK = 64
DV = 64
W_A = H_A * DV
W_B = 512
CHUNK = 64
GRID_W = 64
EPS = 1e-6
N_QKV = 3 * H_A * DK
D_FF = 2816
SUBLANES = 8
MOD_ROWS = SUBLANES
WIDE_TILE = 512
SUB_TILE = 256
DELTA_BATCH = 4
VMEM_LIMIT = 56 * 1024 * 1024

F32 = jnp.float32
BF16 = jnp.bfloat16


def _dot(a, b):
    return jnp.dot(a, b, preferred_element_type=F32)


def _dot_nt(a, b):
    return lax.dot_general(a, b, (((1,), (1,)), ((), ())), preferred_element_type=F32)


def _dot_tn(a, b):
    return lax.dot_general(a, b, (((0,), (0,)), ((), ())), preferred_element_type=F32)


def _resident(shape):
    return pl.BlockSpec(shape, lambda i: (0,) * len(shape), pipeline_mode=pl.Buffered(1))


def _lock_step(programs):
    programs = list(programs)
    while programs:
        for prog in list(programs):
            try:
                next(prog)
            except StopIteration:
                programs.remove(prog)


def _silu(x):
    return x * jax.nn.sigmoid(x)


def _rms(x, gain):
    return x * lax.rsqrt(jnp.mean(x * x, axis=-1, keepdims=True) + EPS) * gain


def _conv3_rows(x, w_ref, row_len):
    n = x.shape[0]
    pos = lax.broadcasted_iota(jnp.int32, (n, 1), 0) & (row_len - 1)
    prev = jnp.where(pos == 0, 0.0, pltpu.roll(x, 1, 0))
    nxt = jnp.where(pos == row_len - 1, 0.0, pltpu.roll(x, n - 1, 0))
    return prev * w_ref[0:1, :] + x * w_ref[1:2, :] + nxt * w_ref[2:3, :]


def _mod_kernel(c_ref, w_ref, b_ref, o_ref):
    rows = c_ref.shape[0]
    s3 = jnp.concatenate([p.astype(F32) for p in _split3(_silu(c_ref[...]))], axis=0).astype(BF16)
    w = w_ref[0]
    w_hi = w.astype(BF16)
    w_lo = (w - w_hi.astype(F32)).astype(BF16)
    acc = _dot(s3, w_hi) + _dot(s3, w_lo)
    o_ref[...] = acc[0:rows] + acc[rows:2 * rows] + acc[2 * rows:] + b_ref[...]


def _modulation(cvec, w_mod, b_mod, layer):
    n_out = w_mod.shape[2]
    blk = D_MODEL
    return pl.pallas_call(
        _mod_kernel,
        grid=(n_out // blk,),
        in_specs=[pl.BlockSpec((MOD_ROWS, D_MODEL), lambda j: (0, 0)),
                  pl.BlockSpec((1, D_MODEL, blk), lambda j: (layer, 0, j)),
                  pl.BlockSpec((1, blk), lambda j: (layer, j))],
        out_specs=pl.BlockSpec((MOD_ROWS, blk), lambda j: (0, j)),
        out_shape=jax.ShapeDtypeStruct((MOD_ROWS, n_out), F32),
        name="modulation",
    )(cvec, w_mod, b_mod)


def _inproj_kernel(x_ref, mod_ref, gain_ref, wqkv_ref, wba_ref, wbat_ref, conv_ref,
                   qkv_ref, ba_ref, bat_ref, *, row_len):
    slab = 2 * DK
    lo = lax.broadcasted_iota(jnp.int32, (SUB_TILE, slab), 1) < DK
    ssq = lambda v, m: jnp.sum(jnp.where(m, v * v, 0.0), axis=-1, keepdims=True)

    def sub_tile(r0):
        rows = slice(r0, r0 + SUB_TILE)
        x = x_ref[rows, :]
        h = _rms(x, gain_ref[0:1, :]) * (1.0 + mod_ref[0, 1:2, :]) + mod_ref[0, 0:1, :]
        hb = h.astype(BF16)
        yield
        for pair in range(N_QKV // (2 * slab)):
            cols2 = slice(pair * 2 * slab, (pair + 1) * 2 * slab)
            v2 = _silu(_conv3_rows(_dot(hb, wqkv_ref[:, cols2]), conv_ref.at[:, cols2], row_len))
            for half in range(2):
                s = 2 * pair + half
                cols = slice(s * slab, (s + 1) * slab)
                v = v2[:, half * slab:(half + 1) * slab]
                if s < W_A // slab:
                    r = jnp.where(lo, lax.rsqrt(ssq(v, lo) + EPS) * (DK ** -0.5),
                                  lax.rsqrt(ssq(v, ~lo) + EPS) * (DK ** -0.5))
                    qkv_ref[rows, cols] = v * r
                else:
                    kmask = lo if (s - W_A // slab) % 2 else ~lo
                    qkv_ref[rows, cols] = v * jnp.where(kmask, lax.rsqrt(ssq(v, kmask) + EPS), 1.0)
            yield
        ba_ref[rows, :] = _dot(hb, wba_ref[...])
        for ci in range(SUB_TILE // CHUNK):
            bat_ref[r0 // CHUNK + ci] = _dot_nt(wbat_ref[...], hb[ci * CHUNK:(ci + 1) * CHUNK, :])

    _lock_step(sub_tile(r0) for r0 in range(0, x_ref.shape[0], SUB_TILE))


def _inproj(x, mod3, gains, w_qkv, w_ba, w_bat, conv_qkv, *, mod_row_fn, row_len):
    n_tok = x.shape[0]
    tm = WIDE_TILE
    const = lambda i: (0, 0)
    return pl.pallas_call(
        functools.partial(_inproj_kernel, row_len=row_len),
        grid=(n_tok // tm,),
        in_specs=[pl.BlockSpec((tm, D_MODEL), lambda i: (i, 0)),
                  pl.BlockSpec((1, 6, D_MODEL), lambda i: (mod_row_fn(i, tm), 0, 0)),
                  pl.BlockSpec((4, D_MODEL), const),
                  pl.BlockSpec((D_MODEL, N_QKV), const),
                  pl.BlockSpec((D_MODEL, 4 * H_A), const),
                  pl.BlockSpec((4 * H_A, D_MODEL), const),
                  pl.BlockSpec((3, N_QKV), const)],
        out_specs=[pl.BlockSpec((tm, N_QKV), lambda i: (i, 0)),
                   pl.BlockSpec((tm, 4 * H_A), lambda i: (i, 0)),
                   pl.BlockSpec((tm // CHUNK, 4 * H_A, CHUNK), lambda i: (i, 0, 0))],
        out_shape=[jax.ShapeDtypeStruct((n_tok, N_QKV), F32),
                   jax.ShapeDtypeStruct((n_tok, 4 * H_A), F32),
                   jax.ShapeDtypeStruct((n_tok // CHUNK, 4 * H_A, CHUNK), F32)],
        compiler_params=pltpu.CompilerParams(dimension_semantics=("arbitrary",),
                                             vmem_limit_bytes=VMEM_LIMIT),
        name="inproj",
    )(x, mod3, gains, w_qkv, w_ba, w_bat, conv_qkv)


def _unit_tri_inverse_stages(lows, lower):
    c = CHUNK
    ri = lax.broadcasted_iota(jnp.int32, (c, c), 0)
    ci = lax.broadcasted_iota(jnp.int32, (c, c), 1)
    eye = (ri == ci).astype(F32)
    rng = range(len(lows))
    joins = lambda b: ((ri // (2 * b)) == (ci // (2 * b))) & ((ri // b) != (ci // b))
    t = [eye - jnp.where(joins(1), lows[i], 0.0) for i in rng]
    b = 2
    while b < c:
        mask = joins(b)
        t16 = [t[i].astype(BF16) for i in rng]
        e16 = [jnp.where(mask, lows[i], 0.0).astype(BF16) for i in rng]
        if b % SUBLANES:
            x = [_dot(t16[i], e16[i]) for i in rng]
            yield
            t = [t[i] - _dot(x[i].astype(BF16), t16[i]) for i in rng]
            yield
        else:
            first = [slice(s, s + b) for s in range(0, c, 2 * b)]
            second = [slice(s + b, s + 2 * b) for s in range(0, c, 2 * b)]
            keep, upd = (first, second) if lower else (second, first)
            t_upd = [jnp.concatenate([t[i][r] for r in upd], axis=0) for i in rng]
            x = [_dot(t_upd[i].astype(BF16), e16[i]) for i in rng]
            yield
            t_upd = [t_upd[i] - _dot(x[i].astype(BF16), t16[i]) for i in rng]
            yield
            pair = lambda kept, new: (kept, new) if lower else (new, kept)
            t = [jnp.concatenate(
                [piece for k, r in enumerate(keep)
                 for piece in pair(t[i][r], t_upd[i][k * b:(k + 1) * b])], axis=0) for i in rng]
        b *= 2
    return t


def _split3(x):
    pieces = []
    for _ in range(3):
        p = x.astype(BF16)
        x = x - p.astype(F32)
        pieces.append(p)
    return pieces


def _group_stages(qkv_ref, o_ref, s_ref, bb, d, gates):
    c = CHUNK
    fwd = d == 0
    row = lax.broadcasted_iota(jnp.int32, (c, c), 0)
    col = lax.broadcasted_iota(jnp.int32, (c, c), 1)
    incl = (row >= col) if fwd else (row <= col)
    strict = (row > col) if fwd else (row < col)
    lane = lax.broadcasted_iota(jnp.int32, (c, 2 * DK), 1)
    beta_all, beta_all_t, gc_all, gc_all_t, eg_all, tail_all, eglast_all = gates
    heads = range(H_A)
    kmask = [(lane < DK) if h % 2 else (lane >= DK) for h in heads]
    kv_slab = lambda h: qkv_ref[bb, :, W_A + h * 2 * DK:W_A + (h + 1) * 2 * DK]
    q_slab = lambda h: qkv_ref[bb, :, (h // 2) * 2 * DK:(h // 2 + 1) * 2 * DK]

    decay, kq = [], []
    for h in heads:
        j = d * H_A + h
        kv = kv_slab(h)
        lhs16 = jnp.concatenate([kv * beta_all[:, j:j + 1], q_slab(h)], axis=0).astype(BF16)
        kq.append(_dot_nt(lhs16, jnp.where(kmask[h], kv, 0.0).astype(BF16)))
        diff = gc_all[:, j:j + 1] - gc_all_t[j:j + 1, :]
        decay.append(jnp.where(incl, jnp.exp(jnp.where(incl, diff, 0.0)), 0.0))
    yield

    lows = [jnp.where(strict, kq[h][:c] * decay[h], 0.0) for h in heads]
    attn16 = [(kq[h][c:] * decay[h]).astype(BF16) for h in heads]
    tmat = yield from _unit_tri_inverse_stages(lows, lower=fwd)

    uw, eg = [], []
    for h in heads:
        j = d * H_A + h
        eg.append(jnp.broadcast_to(eg_all[:, j:j + 1], (c, 2 * DK)))
        kv = kv_slab(h)
        t_beta16 = (tmat[h] * beta_all_t[j:j + 1, :]).astype(BF16)
        uw.append(_dot(t_beta16, jnp.where(kmask[h], kv * eg[h], kv).astype(BF16)))
    yield

    pairs = range(H_A // 2)
    s_old = [s_ref[bb, d, p] for p in pairs]
    qh = [q_slab(2 * p) * jnp.where(kmask[2 * p + 1], eg[2 * p + 1], eg[2 * p]) for p in pairs]
    wq = [_dot(jnp.concatenate([uw[2 * p], uw[2 * p + 1], qh[p]], axis=0).astype(BF16),
               s_old[p].astype(BF16)) for p in pairs]
    yield

    v_new16 = [jnp.where(kmask[h], 0.0, uw[h] - wq[h // 2][(h % 2) * c:(h % 2 + 1) * c])
               .astype(BF16) for h in heads]
    for p in pairs:
        o_ref[bb, :, 2 * p * DV:(2 * p + 2) * DV] = (
            wq[p][2 * c:] + _dot(attn16[2 * p], v_new16[2 * p])
            + _dot(attn16[2 * p + 1], v_new16[2 * p + 1]))
    for h in heads:
        j = d * H_A + h
        k0 = 0 if h % 2 else DK
        kt16 = (kv_slab(h)[:, k0:k0 + DK] * tail_all[:, j:j + 1]).astype(BF16)
        s_ref[bb, d, h // 2, k0:k0 + DK, :] = (
            s_old[h // 2][k0:k0 + DK] * eglast_all[:, j:j + 1] + _dot_tn(kt16, v_new16[h]))


def _delta_kernel(*refs, zero_init, emit_state):
    (qkv_f_ref, qkv_b_ref, ba_f_ref, ba_b_ref, bat_f_ref, bat_b_ref,
     alog_ref, dtb_ref, alog_t_ref, dtb_t_ref) = refs[:10]
    rest = list(refs[10:])
    s0_ref = None if zero_init else rest.pop(0)
    o_f_ref, o_b_ref = rest.pop(0), rest.pop(0)
    sfin_ref = rest.pop(0) if emit_state else None
    s_ref = rest.pop(0)

    n = pl.program_id(1)
    n_chunks = pl.num_programs(1)
    nb = s_ref.shape[0]
    state_at = lambda h: (0, DV) if h % 2 else (DK, 0)

    @pl.when(n == 0)
    def _():
        s_ref[...] = jnp.zeros(s_ref.shape, F32)
        if not zero_init:
            for bb in range(nb):
                for d in range(2):
                    for h in range(H_A):
                        r0, l0 = state_at(h)
                        s_ref[bb, d, h // 2, r0:r0 + DK, l0:l0 + DV] = s0_ref[bb, d, h]

    c = CHUNK
    row = lax.broadcasted_iota(jnp.int32, (c, c), 0)
    col = lax.broadcasted_iota(jnp.int32, (c, c), 1)
    tril16 = (row >= col).astype(BF16)
    triu16 = (row <= col).astype(BF16)

    programs = []
    for bb in range(nb):
        for d, (qkv_ref, ba_ref, bat_ref, o_ref) in enumerate(
                ((qkv_f_ref, ba_f_ref, bat_f_ref, o_f_ref),
                 (qkv_b_ref, ba_b_ref, bat_b_ref, o_b_ref))):
            fwd = d == 0
            csum16, csum_t16 = (tril16, triu16) if fwd else (triu16, tril16)
            last = c - 1 if fwd else 0
            ba = ba_ref[bb]
            bat = bat_ref[bb, 0]
            beta_all = jax.nn.sigmoid(ba[:, 0:2 * H_A])
            g_all = -jnp.exp(alog_ref[...]) * jax.nn.softplus(ba[:, 2 * H_A:] + dtb_ref[...])
            g_all_t = -jnp.exp(alog_t_ref[...]) * jax.nn.softplus(
                bat[2 * H_A:, :] + dtb_t_ref[...])
            gc_all = sum(_dot(csum16, p) for p in _split3(g_all))
            gc_all_t = sum(_dot(p, csum_t16) for p in _split3(g_all_t))
            glast_all = gc_all[last:last + 1, :]
            beta_all_t = jax.nn.sigmoid(bat[0:2 * H_A, :])
            gates = (beta_all, beta_all_t, gc_all, gc_all_t, jnp.exp(gc_all),
                     jnp.exp(glast_all - gc_all), jnp.exp(glast_all))
            programs.append(_group_stages(qkv_ref, o_ref, s_ref, bb, d, gates))

    _lock_step(programs)

    if emit_state:
        @pl.when(n == n_chunks - 1)
        def _():
            for bb in range(nb):
                for d in range(2):
                    for h in range(H_A):
                        r0, l0 = state_at(h)
                        sfin_ref[bb, d, h] = s_ref[bb, d, h // 2, r0:r0 + DK, l0:l0 + DV]


def _delta_scan(qkv, ba, bat, a_log, dt_bias, s0, *, batch, emit_state):
    n_tok = qkv.shape[0]
    t = n_tok // batch
    nc = t // CHUNK
    nb = min(DELTA_BATCH, batch)
    assert batch % nb == 0
    qkv = qkv.reshape(batch, t, N_QKV)
    ba = ba.reshape(batch, t, 4 * H_A)
    bat = bat.reshape(batch, nc, 4 * H_A, CHUNK)
    zero_init = s0 is None
    alog = a_log.reshape(1, 2 * H_A)
    dtb = dt_bias.reshape(1, 2 * H_A)
    alog_t = a_log.reshape(2 * H_A, 1)
    dtb_t = dt_bias.reshape(2 * H_A, 1)

    fw = lambda b, n: (b, n, 0)
    bw = lambda b, n: (b, nc - 1 - n, 0)
    small = lambda b, n: (0, 0)
    in_specs = [pl.BlockSpec((nb, CHUNK, N_QKV), fw),
                pl.BlockSpec((nb, CHUNK, N_QKV), bw),
                pl.BlockSpec((nb, CHUNK, 4 * H_A), fw),
                pl.BlockSpec((nb, CHUNK, 4 * H_A), bw),
                pl.BlockSpec((nb, 1, 4 * H_A, CHUNK), lambda b, n: (b, n, 0, 0)),
                pl.BlockSpec((nb, 1, 4 * H_A, CHUNK), lambda b, n: (b, nc - 1 - n, 0, 0)),
                pl.BlockSpec((1, 2 * H_A), small),
                pl.BlockSpec((1, 2 * H_A), small),
                pl.BlockSpec((2 * H_A, 1), small),
                pl.BlockSpec((2 * H_A, 1), small)]
    args = [qkv, qkv, ba, ba, bat, bat, alog, dtb, alog_t, dtb_t]
    state_spec = pl.BlockSpec((nb, 2, H_A, DK, DV), lambda b, n: (b, 0, 0, 0, 0))
    if not zero_init:
        in_specs.append(state_spec)
        args.append(s0)
    out_specs = [pl.BlockSpec((nb, CHUNK, W_A), fw), pl.BlockSpec((nb, CHUNK, W_A), bw)]
    out_shape = [jax.ShapeDtypeStruct((batch, t, W_A), F32),
                 jax.ShapeDtypeStruct((batch, t, W_A), F32)]
    if emit_state:
        out_specs.append(state_spec)
        out_shape.append(jax.ShapeDtypeStruct((batch, 2, H_A, DK, DV), F32))
    return pl.pallas_call(
        functools.partial(_delta_kernel, zero_init=zero_init, emit_state=emit_state),
        grid=(batch // nb, nc),
        in_specs=in_specs,
        out_specs=out_specs,
        out_shape=out_shape,
        scratch_shapes=[pltpu.VMEM((nb, 2, H_A // 2, 2 * DK, 2 * DV), F32)],
        compiler_params=pltpu.CompilerParams(dimension_semantics=("arbitrary", "arbitrary"),
                                             vmem_limit_bytes=VMEM_LIMIT),
        name="delta_scan",
    )(*args)


def _mix_kernel(x_ref, mod_ref, gain_ref, of_ref, ob_ref, wz_ref, wsc_ref, wg_ref, conv_ref,
                onorm_ref, wa_ref, wb_ref, wo_ref, out_ref, *, row_len):
    def sub_tile(rows):
        x = x_ref[rows, :]
        h = _rms(x, gain_ref[0:1, :]) * (1.0 + mod_ref[0, 1:2, :]) + mod_ref[0, 0:1, :]
        hb = h.astype(BF16)
        yield
        sc = _dot(hb, wsc_ref[...])
        o = of_ref[rows, :] + ob_ref[rows, :]
        slab = 2 * DV
        lo = lax.broadcasted_iota(jnp.int32, (SUB_TILE, slab), 1) < DV
        inv_rms = []
        for s in range(W_A // slab):
            o2 = o[:, s * slab:(s + 1) * slab]
            o2 = o2 * o2
            ms_lo = jnp.sum(jnp.where(lo, o2, 0.0), axis=-1, keepdims=True) * (1.0 / DV)
            ms_hi = jnp.sum(jnp.where(lo, 0.0, o2), axis=-1, keepdims=True) * (1.0 / DV)
            inv_rms.append(jnp.where(lo, lax.rsqrt(ms_lo + EPS), lax.rsqrt(ms_hi + EPS)))
        yield
        z = _dot(hb, wz_ref[...])
        bg, cg, xs = sc[:, 0:W_B], sc[:, W_B:2 * W_B], sc[:, 2 * W_B:3 * W_B]
        yb_in16 = (bg * _conv3_rows(cg * xs, conv_ref, row_len)).astype(BF16)
        yield
        gates = _dot(hb, wg_ref[...])
        oa16 = (o * jnp.concatenate(inv_rms, axis=-1) * onorm_ref[...] * _silu(z)).astype(BF16)
        yield
        y_a = _dot(oa16, wa_ref[...])
        y_b = _dot(yb_in16, wb_ref[...])
        yield
        m = (jax.nn.sigmoid(gates[:, 0:D_MODEL]) * y_a
             + jax.nn.sigmoid(gates[:, D_MODEL:]) * y_b)
        y = _dot(m.astype(BF16), wo_ref[...])
        yield
        out_ref[rows, :] = x + mod_ref[0, 2:3, :] * _rms(y, gain_ref[1:2, :])

    _lock_step(sub_tile(slice(r0, r0 + SUB_TILE)) for r0 in range(0, x_ref.shape[0], SUB_TILE))


def _mix(x, mod3, gains, o_f, o_b, w_z, w_sc, w_g, conv_sc, onorm_t, w_a, w_b, w_o,
         *, mod_row_fn, row_len):
    n_tok = x.shape[0]
    tm = WIDE_TILE
    const = lambda i: (0, 0)
    tok = lambda i: (i, 0)
    return pl.pallas_call(
        functools.partial(_mix_kernel, row_len=row_len),
        grid=(n_tok // tm,),
        in_specs=[pl.BlockSpec((tm, D_MODEL), tok),
                  pl.BlockSpec((1, 6, D_MODEL), lambda i: (mod_row_fn(i, tm), 0, 0)),
                  pl.BlockSpec((4, D_MODEL), const),
                  pl.BlockSpec((tm, W_A), tok),
                  pl.BlockSpec((tm, W_A), tok),
                  _resident((D_MODEL, W_A)),
                  _resident((D_MODEL, 3 * W_B)),
                  _resident((D_MODEL, 2 * D_MODEL)),
                  pl.BlockSpec((3, W_B), const),
                  pl.BlockSpec((1, W_A), const),
                  _resident((W_A, D_MODEL)),
                  _resident((W_B, D_MODEL)),
                  _resident((D_MODEL, D_MODEL))],
        out_specs=pl.BlockSpec((tm, D_MODEL), tok),
        out_shape=jax.ShapeDtypeStruct((n_tok, D_MODEL), F32),
        compiler_params=pltpu.CompilerParams(dimension_semantics=("arbitrary",),
                                             vmem_limit_bytes=VMEM_LIMIT),
        name="mix",
    )(x, mod3, gains, o_f, o_b, w_z, w_sc, w_g, conv_sc, onorm_t, w_a, w_b, w_o)


def _ffn_kernel(x_ref, mod_ref, gain_ref, win_ref, wout_ref, out_ref):
    def sub_tile(rows):
        x = x_ref[rows, :]
        h = _rms(x, gain_ref[2:3, :]) * (1.0 + mod_ref[0, 4:5, :]) + mod_ref[0, 3:4, :]
        hb = h.astype(BF16)
        yield
        gate_up = _dot(hb, win_ref[...])
        yield
        y = _dot((_silu(gate_up[:, 0:D_FF]) * gate_up[:, D_FF:]).astype(BF16), wout_ref[...])
        yield
        out_ref[rows, :] = x + mod_ref[0, 5:6, :] * _rms(y, gain_ref[3:4, :])

    _lock_step(sub_tile(slice(r0, r0 + SUB_TILE)) for r0 in range(0, x_ref.shape[0], SUB_TILE))


def _ffn(x, mod3, gains, w_in, w_out, *, mod_row_fn):
    n_tok = x.shape[0]
    tm = WIDE_TILE
    const = lambda i: (0, 0)
    tok = lambda i: (i, 0)
    return pl.pallas_call(
        _ffn_kernel,
        grid=(n_tok // tm,),
        in_specs=[pl.BlockSpec((tm, D_MODEL), tok),
                  pl.BlockSpec((1, 6, D_MODEL), lambda i: (mod_row_fn(i, tm), 0, 0)),
                  pl.BlockSpec((4, D_MODEL), const),
                  _resident((D_MODEL, 2 * D_FF)),
                  _resident((D_FF, D_MODEL))],
        out_specs=pl.BlockSpec((tm, D_MODEL), tok),
        out_shape=jax.ShapeDtypeStruct((n_tok, D_MODEL), F32),
        compiler_params=pltpu.CompilerParams(dimension_semantics=("arbitrary",),
                                             vmem_limit_bytes=VMEM_LIMIT),
        name="ffn",
    )(x, mod3, gains, w_in, w_out)


def _to_slab_order(src_ref, dst_ref):
    slab = 2 * DK
    left = lax.broadcasted_iota(jnp.int32, (src_ref.shape[1], slab), 1) < DK
    for p in range(H_A // 2):
        q_p = src_ref[0, :, p * slab:(p + 1) * slab]
        k_p = src_ref[0, :, W_A + p * slab:W_A + (p + 1) * slab]
        v_p = src_ref[0, :, 2 * W_A + p * slab:2 * W_A + (p + 1) * slab]
        k_swapped = pltpu.roll(k_p, DK, 1)
        dst_ref[:, p * slab:(p + 1) * slab] = pltpu.roll(q_p, DK, 1).astype(dst_ref.dtype)
        even = W_A + 2 * p * slab
        dst_ref[:, even:even + slab] = jnp.where(left, v_p, k_swapped).astype(dst_ref.dtype)
        dst_ref[:, even + slab:even + 2 * slab] = (
            jnp.where(left, k_swapped, v_p).astype(dst_ref.dtype))


def _split_w_in_kernel(w_ref, conv_ref, qkv_ref, z_ref, ba_ref, sc_ref, g_ref, conv_out_ref):
    _to_slab_order(w_ref, qkv_ref)
    _to_slab_order(conv_ref, conv_out_ref)
    o_z, o_ba = N_QKV, N_QKV + W_A
    o_sc = o_ba + 4 * H_A
    o_g = o_sc + 3 * W_B
    z_ref[...] = w_ref[0, :, o_z:o_ba].astype(BF16)
    ba_ref[...] = w_ref[0, :, o_ba:o_sc].astype(BF16)
    sc_ref[...] = w_ref[0, :, o_sc:o_g].astype(BF16)
    g_ref[...] = w_ref[0, :, o_g:].astype(BF16)


def _split_w_in(w, conv_qkv, layer):
    _, rows, n_in = w.shape
    taps = conv_qkv.shape[1]
    blk = 128
    widths = (N_QKV, W_A, 4 * H_A, 3 * W_B, 2 * D_MODEL)
    assert sum(widths) == n_in and rows % blk == 0
    return pl.pallas_call(
        _split_w_in_kernel,
        grid=(rows // blk,),
        in_specs=[pl.BlockSpec((1, blk, n_in), lambda i: (layer, i, 0)),
                  pl.BlockSpec((1, taps, N_QKV), lambda i: (layer, 0, 0))],
        out_specs=[pl.BlockSpec((blk, n), lambda i: (i, 0)) for n in widths]
        + [pl.BlockSpec((taps, N_QKV), lambda i: (0, 0))],
        out_shape=[jax.ShapeDtypeStruct((rows, n), BF16) for n in widths]
        + [jax.ShapeDtypeStruct((taps, N_QKV), conv_qkv.dtype)],
        name="split_w_in",
    )(w, conv_qkv)


def kernel(x_prompt, x_sample, state_delta, c, c_ctx, w_mod, b_mod, norm_gains, w_in, conv_qkv,
           a_log, dt_bias, onorm, conv_sc, w_branch_a, w_branch_b, w_out, w_ffn_in, w_ffn_out):
    depth = w_in.shape[0]
    assert depth == 1
    b_ctx, t_ctx, _ = x_prompt.shape
    b_lat, t_lat, _ = x_sample.shape
    assert SUB_TILE % t_ctx == 0 and SUB_TILE % GRID_W == 0 and t_ctx & (t_ctx - 1) == 0
    assert WIDE_TILE % SUB_TILE == 0 and t_lat % WIDE_TILE == 0
    assert (b_ctx * t_ctx) % WIDE_TILE == 0
    assert b_lat + 1 <= MOD_ROWS
    layer = 0

    cvec = jnp.zeros((MOD_ROWS, D_MODEL), F32).at[0].set(c_ctx).at[1:1 + b_lat].set(c)
    mod3 = _modulation(cvec, w_mod, b_mod, layer).reshape(MOD_ROWS, 6, D_MODEL)

    w_qkv, w_z, w_ba, w_sc, w_g, conv_perm = _split_w_in(w_in, conv_qkv, layer)
    w_bat = w_ba.T
    w_a = w_branch_a[layer].astype(BF16)
    w_b = w_branch_b[layer].astype(BF16)
    w_o = w_out[layer].astype(BF16)
    w_f1 = w_ffn_in[layer].astype(BF16)
    w_f2 = w_ffn_out[layer].astype(BF16)
    gains = norm_gains[layer]
    onorm_t = jnp.tile(onorm[layer], H_A)[None, :]

    def run_group(x3, s0, mod_row_fn, row_len, emit_state):
        batch, t, _ = x3.shape
        x = x3.reshape(batch * t, D_MODEL)
        qkv, ba, bat = _inproj(x, mod3, gains, w_qkv, w_ba, w_bat, conv_perm,
                               mod_row_fn=mod_row_fn, row_len=row_len)
        res = _delta_scan(qkv, ba, bat, a_log[layer], dt_bias[layer], s0,
                          batch=batch, emit_state=emit_state)
        o_f = res[0].reshape(batch * t, W_A)
        o_b = res[1].reshape(batch * t, W_A)
        x1 = _mix(x, mod3, gains, o_f, o_b, w_z, w_sc, w_g, conv_sc[layer], onorm_t,
                  w_a, w_b, w_o, mod_row_fn=mod_row_fn, row_len=row_len)
        x2 = _ffn(x1, mod3, gains, w_f1, w_f2, mod_row_fn=mod_row_fn)
        return x2.reshape(batch, t, D_MODEL), (res[2] if emit_state else None)

    y_p, s_fin = run_group(x_prompt, None, lambda i, tm: 0, t_ctx, True)
    y_s, _ = run_group(x_sample, state_delta[:, layer], lambda i, tm: 1 + (i * tm) // t_lat,
                       GRID_W, False)
    new_state = s_fin[:, None].astype(x_prompt.dtype)
    return (y_p, y_s, new_state)
```
